```python
import jax, jax.numpy as jnp
from jax import lax
import numpy as np


D_MODEL = 1024
BATCH = 8
SEQ = 4096
DEPTH = 1

MLSTM_HEADS = 4
MLSTM_DIM = D_MODEL
MLSTM_HEAD_DIM = MLSTM_DIM // MLSTM_HEADS
MLSTM_CHUNK = 64
CONV_WIDTH = 3
SGU_DIM = D_MODEL
SGU_GROUPS = 8
SGU_GROUP_DIM = SGU_DIM // SGU_GROUPS
SGU_CHUNK = 128
FF_DIM = 4 * D_MODEL
N_BRANCH = 2
N_GATE_ROWS = 4
F_BIAS_LO = 3.0
F_BIAS_HI = 6.0
EPS = 1e-6
IN_SIZES = (2 * MLSTM_DIM, MLSTM_DIM, MLSTM_DIM, SGU_DIM, SGU_DIM, N_BRANCH * D_MODEL, N_GATE_ROWS * MLSTM_HEADS)
D_IN = 2 * MLSTM_DIM + MLSTM_DIM + MLSTM_DIM + SGU_DIM + SGU_DIM + N_BRANCH * D_MODEL + N_GATE_ROWS * MLSTM_HEADS

kernel_name = 'hybrid_mlstm_sgu_block'


def rmsnorm(x, g):
    xf = x.astype(jnp.float32)
    y = xf * lax.rsqrt(jnp.mean(xf * xf, axis=-1, keepdims=True) + EPS)
    return (y * g.astype(jnp.float32)).astype(x.dtype)


def layernorm(x, g, b):
    xf = x.astype(jnp.float32)
    mu = jnp.mean(xf, axis=-1, keepdims=True)
    var = jnp.mean(jnp.square(xf - mu), axis=-1, keepdims=True)
    y = (xf - mu) * lax.rsqrt(var + EPS) * g.astype(jnp.float32) + b.astype(jnp.float32)
    return y.astype(x.dtype)


def modulate(h, shift, scale):
    return h * (1 + scale[:, None, :]) + shift[:, None, :]


def centred_conv(x, w, b):
    pad = CONV_WIDTH // 2
    S = x.shape[1]
    xp = jnp.pad(x, ((0, 0), (pad, pad), (0, 0)))
    y = b
    for j in range(CONV_WIDTH):
        y = y + xp[:, j:j + S, :] * w[j]
    return y


def mlstm_scan(q, k, v, li, lf):
    B, H, S, dk = q.shape
    dv = v.shape[-1]
    L = MLSTM_CHUNK
    NC = S // L

    def chunks(a):
        return jnp.moveaxis(a.reshape((B, H, NC, L) + a.shape[3:]), 2, 0)

    mask = jnp.tril(jnp.ones((L, L), dtype=bool))

    def step(carry, xs):
        C, n, m = carry
        qc, kc, vc, lic, lfc = xs
        b = jnp.cumsum(lfc, axis=-1)
        bL = b[..., -1]
        Dm = b[..., :, None] - b[..., None, :] + lic[..., None, :]
        Dm = jnp.where(mask, Dm, -jnp.inf)
        inter = b + m[..., None]
        m_t = jnp.maximum(inter, jnp.max(Dm, axis=-1))
        w_inter = jnp.exp(inter - m_t)
        P = jnp.exp(Dm - m_t[..., None]) * jnp.einsum('bhtd,bhsd->bhts', qc, kc)
        num = w_inter[..., None] * jnp.einsum('bhtd,bhde->bhte', qc, C) + jnp.einsum('bhts,bhse->bhte', P, vc)
        den = w_inter * jnp.einsum('bhtd,bhd->bht', qc, n) + jnp.sum(P, axis=-1)
        h = num / jnp.maximum(jnp.abs(den), jnp.exp(-m_t))[..., None]
        g = bL[..., None] - b + lic
        m_new = jnp.maximum(bL + m, jnp.max(g, axis=-1))
        wk = jnp.exp(g - m_new[..., None])[..., None] * kc
        decay = jnp.exp(bL + m - m_new)
        C_new = decay[..., None, None] * C + jnp.einsum('bhsd,bhse->bhde', wk, vc)
        n_new = decay[..., None] * n + jnp.sum(wk, axis=-2)
        return (C_new, n_new, m_new), h

    init = (jnp.zeros((B, H, dk, dv), jnp.float32),
            jnp.zeros((B, H, dk), jnp.float32),
            jnp.zeros((B, H), jnp.float32))
    _, hs = lax.scan(step, init, (chunks(q), chunks(k), chunks(v), chunks(li), chunks(lf)))
    return jnp.moveaxis(hs, 0, 2).reshape(B, H, S, dv)


def mlstm_bidirectional(q, k, v, gates):
    fwd = mlstm_scan(q, k, v, gates[0], jax.nn.log_sigmoid(gates[1]))
    flip = lambda a: jnp.flip(a, axis=2)
    bwd = mlstm_scan(flip(q), flip(k), flip(v), flip(gates[2]), flip(jax.nn.log_sigmoid(gates[3])))
    return fwd + flip(bwd)


def spatial_gating(u, v, ln_g, ln_b, w_s, b_s):
    B, S, _ = v.shape
    NC = S // SGU_CHUNK
    vn = layernorm(v, ln_g, ln_b).reshape(B, NC, SGU_CHUNK, SGU_GROUPS, SGU_GROUP_DIM)
    s = jnp.einsum('gpq,bnqgc->bnpgc', w_s, vn) + b_s.T[None, None, :, :, None]
    return u * s.reshape(B, S, SGU_DIM)


def hybrid_layer(x, mod, norm1_g, norm2_g, w_in, b_if, conv_w, conv_b, mh_g,
                 ln_v_g, ln_v_b, w_s, b_s, w_out, w1, w2):
    B, S, _ = x.shape
    H, dh = MLSTM_HEADS, MLSTM_HEAD_DIM
    sh1, sc1, g1, sh2, sc2, g2 = jnp.split(mod, 6, axis=-1)

    h = modulate(rmsnorm(x, norm1_g), sh1, sc1)
    z = h @ w_in
    cuts = []
    acc = 0
    for sz in IN_SIZES[:-1]:
        acc += sz
        cuts.append(acc)
    qk, v_a, o_a, u_b, v_b, mg_pre, if_pre = jnp.split(z, cuts, axis=-1)

    qk = jax.nn.silu(centred_conv(qk, conv_w, conv_b))
    q_a, k_a = jnp.split(qk, 2, axis=-1)

    def heads(a):
        return a.reshape(B, S, H, dh).transpose(0, 2, 1, 3).astype(jnp.float32)

    qh = heads(q_a) * (dh ** -0.5)
    kh = heads(k_a)
    vh = heads(v_a)
    gates = (if_pre.reshape(B, S, N_GATE_ROWS, H) + b_if).transpose(2, 0, 3, 1).astype(jnp.float32)
    hA = mlstm_bidirectional(qh, kh, vh, gates)
    hA = hA * lax.rsqrt(jnp.mean(hA * hA, axis=-1, keepdims=True) + EPS)
    hA = hA.transpose(0, 2, 1, 3).reshape(B, S, MLSTM_DIM) * mh_g.astype(jnp.float32)
    y_a = hA.astype(x.dtype) * jax.nn.sigmoid(o_a)

    y_b = spatial_gating(jax.nn.gelu(u_b, approximate=False), jax.nn.gelu(v_b, approximate=False),
                         ln_v_g, ln_v_b, w_s, b_s)

    gate_a, gate_b = jnp.split(jax.nn.sigmoid(mg_pre), 2, axis=-1)
    mix = gate_a * y_a + gate_b * y_b
    x = x + g1[:, None, :] * (mix @ w_out)

    h2 = modulate(rmsnorm(x, norm2_g), sh2, sc2)
    x = x + g2[:, None, :] * (jnp.square(jax.nn.relu(h2 @ w1)) @ w2)
    return x


def setup_inputs(seed: int = 0) -> dict:
    key = jax.random.key(seed)
    ks = jax.random.split(key, 20)
    D = D_MODEL
    H = MLSTM_HEADS
    nrm = jax.random.normal
    f32 = jnp.float32
    x = nrm(ks[0], (BATCH, SEQ, D), f32)
    c = nrm(ks[1], (BATCH, D), f32)
    w_ada = nrm(ks[2], (DEPTH, D, 6 * D), f32) * (0.5 * D ** -0.5)
    b_ada = 0.01 * nrm(ks[3], (DEPTH, 6 * D), f32)
    norm1_g = 1.0 + 0.01 * nrm(ks[4], (DEPTH, D), f32)
    norm2_g = 1.0 + 0.01 * nrm(ks[5], (DEPTH, D), f32)
    w_in = nrm(ks[6], (DEPTH, D, D_IN), f32) * D ** -0.5
    f_lin = jnp.linspace(F_BIAS_LO, F_BIAS_HI, H, dtype=f32)
    zeros_h = jnp.zeros((H,), f32)
    b_if = jnp.stack([zeros_h, f_lin, zeros_h, f_lin])[None] + 0.1 * nrm(ks[7], (DEPTH, N_GATE_ROWS, H), f32)
    conv_w = nrm(ks[8], (DEPTH, CONV_WIDTH, 2 * MLSTM_DIM), f32) * CONV_WIDTH ** -0.5
    conv_b = 0.01 * nrm(ks[9], (DEPTH, 2 * MLSTM_DIM), f32)
    mh_g = 1.0 + 0.01 * nrm(ks[10], (DEPTH, MLSTM_DIM), f32)
    ln_v_g = 1.0 + 0.01 * nrm(ks[11], (DEPTH, SGU_DIM), f32)
    ln_v_b = 0.01 * nrm(ks[12], (DEPTH, SGU_DIM), f32)
    w_s = nrm(ks[13], (DEPTH, SGU_GROUPS, SGU_CHUNK, SGU_CHUNK), f32) * SGU_CHUNK ** -0.5
    b_s = 1.0 + 0.1 * nrm(ks[14], (DEPTH, SGU_GROUPS, SGU_CHUNK), f32)
    w_out = nrm(ks[15], (DEPTH, D, D), f32) * D ** -0.5
    w1 = nrm(ks[16], (DEPTH, D, FF_DIM), f32) * D ** -0.5
    w2 = nrm(ks[17], (DEPTH, FF_DIM, D), f32) * FF_DIM ** -0.5
    normf_g = 1.0 + 0.01 * nrm(ks[18], (D,), f32)
    return {'x': x, 'c': c, 'w_ada': w_ada, 'b_ada': b_ada, 'norm1_g': norm1_g, 'norm2_g': norm2_g,
            'w_in': w_in, 'b_if': b_if, 'conv_w': conv_w, 'conv_b': conv_b, 'mh_g': mh_g,
            'ln_v_g': ln_v_g, 'ln_v_b': ln_v_b, 'w_s': w_s, 'b_s': b_s, 'w_out': w_out,
            'w1': w1, 'w2': w2, 'normf_g': normf_g}


def reference(x, c, w_ada, b_ada, norm1_g, norm2_g, w_in, b_if, conv_w, conv_b, mh_g,
              ln_v_g, ln_v_b, w_s, b_s, w_out, w1, w2, normf_g):
    c_act = jax.nn.silu(c)
    for l in range(DEPTH):
        mod = c_act @ w_ada[l] + b_ada[l]
        x = hybrid_layer(x, mod, norm1_g[l], norm2_g[l], w_in[l], b_if[l], conv_w[l], conv_b[l],
                         mh_g[l], ln_v_g[l], ln_v_b[l], w_s[l], b_s[l], w_out[l], w1[l], w2[l])
    return rmsnorm(x, normf_g)
```

```python
import functools

import jax
import jax.numpy as jnp
from jax import lax
from jax.experimental import pallas as pl
from jax.experimental.pallas import tpu as pltpu

D_MODEL = 1024
HEADS = 4
HEAD_DIM = 256
N_GATE = 16
SGU_GROUPS = 8
SGU_GROUP_DIM = 128
SGU_CHUNK = 128
FF_DIM = 4096
EPS = 1e-6
Z_COLS = 8 * D_MODEL

MLSTM_CHUNK = 256

VMEM_LIMIT = 56 * 1024 * 1024

_BF = jnp.bfloat16
_F32 = jnp.float32


def _const_spec(shape):
    nd = len(shape)
    return pl.BlockSpec(shape, lambda *_: (0,) * nd, pipeline_mode=pl.Buffered(1))


def _sigmoid(x):
    return 1.0 / (1.0 + jnp.exp(-x))


def _gelu(x):
    return 0.5 * x * (1.0 + lax.erf(x * (2.0 ** -0.5)))


def _split3(x):
    x1 = x.astype(_BF)
    r1 = x - x1.astype(_F32)
    x2 = r1.astype(_BF)
    r2 = r1 - x2.astype(_F32)
    return x1, x2, r2.astype(_BF)


def _mod_kernel(c_ref, w_ref, b_ref, o_ref):
    c = c_ref[...]
    ca = c * _sigmoid(c)
    o_ref[...] = jnp.dot(ca, w_ref[...], precision=lax.Precision.HIGHEST,
                         preferred_element_type=_F32) + b_ref[...]


def _mod_call(c, w_ada, b_ada):
    B, D = c.shape
    N = w_ada.shape[1]
    tn = 1536
    return pl.pallas_call(
        _mod_kernel,
        grid=(N // tn,),
        in_specs=[pl.BlockSpec((B, D), lambda j: (0, 0)),
                  pl.BlockSpec((D, tn), lambda j: (0, j)),
                  pl.BlockSpec((1, tn), lambda j: (0, j))],
        out_specs=pl.BlockSpec((B, tn), lambda j: (0, j)),
        out_shape=jax.ShapeDtypeStruct((B, N), _F32),
        name="mod",
    )(c, w_ada, b_ada.reshape(1, N))


def _in_proj_kernel(x_ref, g_ref, sc_ref, sh_ref, w_ref, wg_ref, z_ref, gate_ref):
    x = x_ref[0]
    ms = jnp.mean(x * x, axis=-1, keepdims=True)
    h = x * lax.rsqrt(ms + EPS) * g_ref[...]
    h = h * (1.0 + sc_ref[0]) + sh_ref[0]
    hb = h.astype(_BF)
    for j in range(Z_COLS // D_MODEL):
        cols = slice(j * D_MODEL, (j + 1) * D_MODEL)
        z_ref[0, :, cols] = jnp.dot(hb, w_ref[:, cols],
                                    preferred_element_type=_F32).astype(_BF)
    gate_ref[0] = jnp.dot(hb, wg_ref[...], preferred_element_type=_F32)


def _in_proj_call(x, norm_g, sc, sh, w_bf, wg_bf, tm):
    B, S, D = x.shape
    return pl.pallas_call(
        _in_proj_kernel,
        grid=(B, S // tm),
        in_specs=[pl.BlockSpec((1, tm, D), lambda b, i: (b, i, 0)),
                  _const_spec((1, D)),
                  pl.BlockSpec((1, 1, D), lambda b, i: (b, 0, 0)),
                  pl.BlockSpec((1, 1, D), lambda b, i: (b, 0, 0)),
                  _const_spec((D, Z_COLS)),
                  _const_spec((D, 256))],
        out_specs=[pl.BlockSpec((1, tm, Z_COLS), lambda b, i: (b, i, 0)),
                   pl.BlockSpec((1, tm, 256), lambda b, i: (b, i, 0))],
        out_shape=[jax.ShapeDtypeStruct((B, S, Z_COLS), _BF),
                   jax.ShapeDtypeStruct((B, S, 256), _F32)],
        compiler_params=pltpu.CompilerParams(
            dimension_semantics=("arbitrary", "arbitrary"),
            vmem_limit_bytes=VMEM_LIMIT),
        name="in_proj",
    )(x, norm_g, sc, sh, w_bf, wg_bf)


def _gateprep_kernel(g_ref, bias_ref, col_ref, row_ref):
    L = MLSTM_CHUNK
    g = g_ref[0] + bias_ref[...]
    gf = g[:, :128]
    gi = g[:, 128:]
    ls = jnp.minimum(gf, 0.0) - jnp.log(1.0 + jnp.exp(-jnp.abs(gf)))
    r = lax.broadcasted_iota(jnp.int32, (L, L), 0)
    s = lax.broadcasted_iota(jnp.int32, (L, L), 1)
    tri_lo = (s <= r).astype(_BF)
    tri_up = (s >= r).astype(_BF)
    cf = jnp.zeros((L, 128), _F32)
    cb = jnp.zeros((L, 128), _F32)
    for piece in _split3(ls):
        cf = cf + jnp.dot(tri_lo, piece, preferred_element_type=_F32)
        cb = cb + jnp.dot(tri_up, piece, preferred_element_type=_F32)
    lane = lax.broadcasted_iota(jnp.int32, (L, 128), 1)
    cum = jnp.where(lane < 8, cf, cb)
    out = jnp.where((lane & 7) < 4, cum, gi - cum)
    col_ref[0] = out
    row_ref[0] = out.T[:N_GATE, :]


def _gateprep_call(gates, bias):
    B, S, _ = gates.shape
    L = MLSTM_CHUNK
    return pl.pallas_call(
        _gateprep_kernel,
        grid=(B, S // L),
        in_specs=[pl.BlockSpec((1, L, 256), lambda b, i: (b, i, 0)),
                  _const_spec((1, 256))],
        out_specs=[pl.BlockSpec((1, L, 128), lambda b, i: (b, i, 0)),
                   pl.BlockSpec((1, N_GATE, L), lambda b, i: (b, 0, i))],
        out_shape=[jax.ShapeDtypeStruct((B, S, 128), _F32),
                   jax.ShapeDtypeStruct((B, N_GATE, S), _F32)],
        name="gateprep",
    )(gates, bias)


def _mlstm_kernel(zq_ref, zk_ref, zv_ref, zo_ref, gc_ref, gr_ref,
                  cwq_ref, cwk_ref, cbq_ref, cbk_ref, mhg_ref,
                  y_ref,
                  q_s, k_s, acc_s, cf_s, cb_s, nf_s, nb_s, mf_s, mb_s):
    S = zq_ref.shape[1]
    L = MLSTM_CHUNK
    NC = S // L
    dh = HEAD_DIM

    row = lax.broadcasted_iota(jnp.int32, (L, dh), 0)

    def conv_silu(z_ref, cw_ref, cb_ref, t0, i):
        x = z_ref[0, pl.ds(t0, L), :].astype(_F32)
        tp = pl.multiple_of(jnp.maximum(t0 - 16, 0), 16)
        tn = pl.multiple_of(jnp.minimum(t0 + L, S - 16), 16)
        prev = z_ref[0, pl.ds(tp, 16), :].astype(_F32)[15:16, :]
        nxt = z_ref[0, pl.ds(tn, 16), :].astype(_F32)[0:1, :]
        prev = jnp.where(i > 0, prev, 0.0)
        nxt = jnp.where(i < NC - 1, nxt, 0.0)
        x_prev = jnp.where(row == 0, prev, pltpu.roll(x, 1, axis=0))
        x_next = jnp.where(row == L - 1, nxt, pltpu.roll(x, L - 1, axis=0))
        y = cb_ref[...] + x_prev * cw_ref[0:1, :]
        y = y + x * cw_ref[1:2, :]
        y = y + x_next * cw_ref[2:3, :]
        return y * _sigmoid(y)

    def phase_a(i, carry):
        t0 = pl.multiple_of(i * L, L)
        q = conv_silu(zq_ref, cwq_ref, cbq_ref, t0, i) * (dh ** -0.5)
        q_s[pl.ds(t0, L), :] = q.astype(_BF)
        k = conv_silu(zk_ref, cwk_ref, cbk_ref, t0, i)
        k_s[pl.ds(t0, L), :] = k.astype(_BF)
        acc_s[pl.ds(t0, L), :] = jnp.zeros((L, dh), _F32)
        return carry

    lax.fori_loop(0, NC, phase_a, 0)

    cf_s[...] = jnp.zeros_like(cf_s)
    cb_s[...] = jnp.zeros_like(cb_s)
    nf_s[...] = jnp.zeros_like(nf_s)
    nb_s[...] = jnp.zeros_like(nb_s)
    mf_s[...] = jnp.zeros_like(mf_s)
    mb_s[...] = jnp.zeros_like(mb_s)

    ri = lax.broadcasted_iota(jnp.int32, (L, L), 0)
    ci = lax.broadcasted_iota(jnp.int32, (L, L), 1)
    mask_f = ci <= ri
    mask_b = ci >= ri

    def chunk_step(t0, c_s, n_s, m_s, mask, lane0, last_row):
        qc = q_s[pl.ds(t0, L), :]
        kc = k_s[pl.ds(t0, L), :]
        vc = zv_ref[0, pl.ds(t0, L), :]
        gc = gc_ref[0, 0, pl.ds(t0, L), :]
        b_col = gc[:, lane0:lane0 + 1]
        r_col = gc[:, lane0 + 1:lane0 + 2]
        r_row = gr_ref[0, 0, lane0 + 1:lane0 + 2, pl.ds(t0, L)]
        b_tot = gc_ref[0, 0, pl.ds(t0 + last_row, 1), :][:, lane0:lane0 + 1]
        m = m_s[...]

        dm = jnp.where(mask, b_col + r_row, -jnp.inf)
        inter = b_col + m
        m_t = jnp.maximum(inter, jnp.max(dm, axis=1, keepdims=True))
        s_qk = lax.dot_general(qc, kc, (((1,), (1,)), ((), ())),
                               preferred_element_type=_F32)
        p = jnp.exp(dm - m_t) * s_qk
        w_inter = jnp.exp(inter - m_t)
        c_old = c_s[...]
        n_old = n_s[...]
        num = w_inter * jnp.dot(qc, c_old.astype(_BF), preferred_element_type=_F32)
        num = num + jnp.dot(p.astype(_BF), vc, preferred_element_type=_F32)
        den = w_inter * jnp.sum(qc.astype(_F32) * n_old, axis=1, keepdims=True)
        den = den + jnp.sum(p, axis=1, keepdims=True)
        h = num / jnp.maximum(jnp.abs(den), jnp.exp(-m_t))
        acc_s[pl.ds(t0, L), :] += h

        g = b_tot + r_col
        m_new = jnp.maximum(b_tot + m, jnp.max(g, axis=0, keepdims=True))
        wk = jnp.exp(g - m_new) * kc.astype(_F32)
        decay = jnp.exp(b_tot + m - m_new)
        c_s[...] = decay * c_old + lax.dot_general(
            wk.astype(_BF), vc, (((0,), (0,)), ((), ())), preferred_element_type=_F32)
        n_s[...] = decay * n_old + jnp.sum(wk, axis=0, keepdims=True)
        m_s[...] = m_new

    def phase_b(i, carry):
        tf = pl.multiple_of(i * L, L)
        tb = pl.multiple_of((NC - 1 - i) * L, L)
        chunk_step(tf, cf_s, nf_s, mf_s, mask_f, 0, L - 1)
        chunk_step(tb, cb_s, nb_s, mb_s, mask_b, 2, 0)
        return carry

    lax.fori_loop(0, NC, phase_b, 0)

    def phase_c(i, carry):
        t0 = pl.multiple_of(i * L, L)
        h = acc_s[pl.ds(t0, L), :]
        hn = h * lax.rsqrt(jnp.mean(h * h, axis=-1, keepdims=True) + EPS) * mhg_ref[...]
        o = zo_ref[0, pl.ds(t0, L), :].astype(_F32)
        y_ref[0, pl.ds(t0, L), :] = (hn * _sigmoid(o)).astype(_BF)
        return carry

    lax.fori_loop(0, NC, phase_c, 0)


def _mlstm_call(z, gcol, grow, conv_w, conv_b, mh_g):
    B, S, _ = z.shape
    dh = HEAD_DIM
    H = HEADS

    def zspec(off):
        return pl.BlockSpec((1, S, dh), lambda b, h, off=off: (b, 0, off + h))

    def wspec(rows, off):
        return pl.BlockSpec((rows, dh), lambda b, h, off=off: (0, off + h))

    return pl.pallas_call(
        _mlstm_kernel,
        grid=(B, H),
        in_specs=[zspec(0), zspec(H), zspec(2 * H), zspec(3 * H),
                  pl.BlockSpec((1, 1, S, 4), lambda b, h: (b, h, 0, 0)),
                  pl.BlockSpec((1, 1, 4, S), lambda b, h: (b, h, 0, 0)),
                  wspec(3, 0), wspec(3, H), wspec(1, 0), wspec(1, H), wspec(1, 0)],
        out_specs=pl.BlockSpec((1, S, dh), lambda b, h: (b, 0, h)),
        out_shape=jax.ShapeDtypeStruct((B, S, H * dh), _BF),
        scratch_shapes=[pltpu.VMEM((S, dh), _BF), pltpu.VMEM((S, dh), _BF),
                        pltpu.VMEM((S, dh), _F32),
                        pltpu.VMEM((dh, dh), _F32), pltpu.VMEM((dh, dh), _F32),
                        pltpu.VMEM((1, dh), _F32), pltpu.VMEM((1, dh), _F32),
                        pltpu.VMEM((1, 1), _F32), pltpu.VMEM((1, 1), _F32)],
        compiler_params=pltpu.CompilerParams(
            dimension_semantics=("arbitrary", "arbitrary"),
            vmem_limit_bytes=VMEM_LIMIT),
        name="mlstm",
    )(z, z, z, z, gcol, grow, conv_w, conv_w, conv_b, conv_b, mh_g)


def _mix_kernel(x_ref, zu_ref, zv_ref, zg_ref, ya_ref, g1_ref, lng_ref, lnb_ref,
                ws_ref, bs_ref, wo_ref, o_ref, mix_s):
    tm = x_ref.shape[1]
    P = SGU_CHUNK
    C = SGU_GROUP_DIM
    for j in range(tm // P):
        rows = slice(j * P, (j + 1) * P)
        gv = _gelu(zv_ref[0, rows, :].astype(_F32))
        mu = jnp.mean(gv, axis=-1, keepdims=True)
        dv = gv - mu
        var = jnp.mean(dv * dv, axis=-1, keepdims=True)
        vn = (dv * lax.rsqrt(var + EPS) * lng_ref[...] + lnb_ref[...]).astype(_BF)
        s = jnp.concatenate(
            [jnp.dot(ws_ref[g], vn[:, g * C:(g + 1) * C], preferred_element_type=_F32)
             for g in range(SGU_GROUPS)], axis=1) + bs_ref[...]
        yb = _gelu(zu_ref[0, rows, :].astype(_F32)) * s
        ga = _sigmoid(zg_ref[0, rows, :D_MODEL].astype(_F32))
        gb = _sigmoid(zg_ref[0, rows, D_MODEL:].astype(_F32))
        mix_s[rows, :] = (ga * ya_ref[0, rows, :].astype(_F32) + gb * yb).astype(_BF)
    upd = jnp.dot(mix_s[...], wo_ref[...], preferred_element_type=_F32)
    o_ref[0] = x_ref[0] + g1_ref[0] * upd


def _mix_call(x, z, ya, g1, ln_g, ln_b, ws_bf, bs_full, wo_bf, tm):
    B, S, D = x.shape
    return pl.pallas_call(
        _mix_kernel,
        grid=(B, S // tm),
        in_specs=[pl.BlockSpec((1, tm, D), lambda b, i: (b, i, 0)),
                  pl.BlockSpec((1, tm, D), lambda b, i: (b, i, 4)),
                  pl.BlockSpec((1, tm, D), lambda b, i: (b, i, 5)),
                  pl.BlockSpec((1, tm, 2 * D), lambda b, i: (b, i, 3)),
                  pl.BlockSpec((1, tm, D), lambda b, i: (b, i, 0)),
                  pl.BlockSpec((1, 1, D), lambda b, i: (b, 0, 0)),
                  _const_spec((1, D)), _const_spec((1, D)),
                  _const_spec((SGU_GROUPS, SGU_CHUNK, SGU_CHUNK)),
                  _const_spec((SGU_CHUNK, D)),
                  _const_spec((D, D))],
        out_specs=pl.BlockSpec((1, tm, D), lambda b, i: (b, i, 0)),
        out_shape=jax.ShapeDtypeStruct((B, S, D), _F32),
        scratch_shapes=[pltpu.VMEM((tm, D), _BF)],
        compiler_params=pltpu.CompilerParams(
            dimension_semantics=("arbitrary", "arbitrary"),
            vmem_limit_bytes=VMEM_LIMIT),
        name="mix",
    )(x, z, z, z, ya, g1, ln_g, ln_b, ws_bf, bs_full, wo_bf)


def _ffn_kernel(x_ref, g_ref, sc_ref, sh_ref, g2_ref, w1_ref, w2_ref, nf_ref, o_ref):
    x = x_ref[0]
    ms = jnp.mean(x * x, axis=-1, keepdims=True)
    h = x * lax.rsqrt(ms + EPS) * g_ref[...]
    hb = (h * (1.0 + sc_ref[0]) + sh_ref[0]).astype(_BF)
    ff = jnp.zeros(x.shape, _F32)
    for j in range(FF_DIM // D_MODEL):
        cols = slice(j * D_MODEL, (j + 1) * D_MODEL)
        a = jnp.maximum(jnp.dot(hb, w1_ref[:, cols], preferred_element_type=_F32), 0.0)
        ff = ff + jnp.dot((a * a).astype(_BF), w2_ref[cols, :], preferred_element_type=_F32)
    x2 = x + g2_ref[0] * ff
    ms2 = jnp.mean(x2 * x2, axis=-1, keepdims=True)
    o_ref[0] = x2 * lax.rsqrt(ms2 + EPS) * nf_ref[...]


def _ffn_call(x1, norm_g, sc, sh, g2, w1_bf, w2_bf, normf_g, tm):
    B, S, D = x1.shape
    vec = pl.BlockSpec((1, 1, D), lambda b, i: (b, 0, 0))
    return pl.pallas_call(
        _ffn_kernel,
        grid=(B, S // tm),
        in_specs=[pl.BlockSpec((1, tm, D), lambda b, i: (b, i, 0)),
                  _const_spec((1, D)), vec, vec, vec,
                  _const_spec((D, FF_DIM)), _const_spec((FF_DIM, D)),
                  _const_spec((1, D))],
        out_specs=pl.BlockSpec((1, tm, D), lambda b, i: (b, i, 0)),
        out_shape=jax.ShapeDtypeStruct((B, S, D), _F32),
        compiler_params=pltpu.CompilerParams(
            dimension_semantics=("arbitrary", "arbitrary"),
            vmem_limit_bytes=VMEM_LIMIT),
        name="ffn",
    )(x1, norm_g, sc, sh, g2, w1_bf, w2_bf, normf_g)


def _gate_weights(w_if, b_if):
    D = w_if.shape[0]
    H = HEADS
    i_f, f_f, i_b, f_b = (w_if[:, k * H:(k + 1) * H] for k in range(4))
    bi_f, bf_f, bi_b, bf_b = (b_if[k] for k in range(4))
    zw = jnp.zeros((D, H), w_if.dtype)
    zb = jnp.zeros((H,), b_if.dtype)
    padw = jnp.zeros((D, 128 - 4 * H), w_if.dtype)
    padb = jnp.zeros((128 - 4 * H,), b_if.dtype)
    w = jnp.concatenate([f_f, f_f, f_b, f_b, padw, zw, i_f, zw, i_b, padw], axis=1)
    b = jnp.concatenate([bf_f, bf_f, bf_b, bf_b, padb, zb, bi_f, zb, bi_b, padb])
    return w, b.reshape(1, 256)


def kernel(x, c, w_ada, b_ada, norm1_g, norm2_g, w_in, b_if, conv_w, conv_b, mh_g,
           ln_v_g, ln_v_b, w_s, b_s, w_out, w1, w2, normf_g):
    B, S, D = x.shape
    H = HEADS
    assert w_ada.shape[0] == 1, "single layer"

    mod = _mod_call(c, w_ada[0], b_ada[0])
    sh1, sc1, g1, sh2, sc2, g2 = (mod[:, k * D:(k + 1) * D].reshape(B, 1, D) for k in range(6))

    w_in0 = w_in[0]
    wg, bg = _gate_weights(w_in0[:, Z_COLS:], b_if[0])
    z, gates = _in_proj_call(x, norm1_g, sc1, sh1, w_in0[:, :Z_COLS].astype(_BF),
                             wg.astype(_BF), tm=512)

    gcol, grow = _gateprep_call(gates, bg)
    gcol_h = gcol[:, :, :N_GATE].reshape(B, S, 4, H).transpose(0, 3, 1, 2)
    grow_h = grow.reshape(B, 4, H, S).transpose(0, 2, 1, 3)

    ya = _mlstm_call(z, gcol_h, grow_h, conv_w[0], conv_b, mh_g)

    bs_full = jnp.repeat(b_s[0].T, SGU_GROUP_DIM, axis=1)
    x1 = _mix_call(x, z, ya, g1, ln_v_g, ln_v_b, w_s[0].astype(_BF), bs_full,
                   w_out[0].astype(_BF), tm=512)

    return _ffn_call(x1, norm2_g, sc2, sh2, g2, w1[0].astype(_BF), w2[0].astype(_BF),
                     normf_g.reshape(1, D), tm=512)
```

```python
import jax
import jax.numpy as jnp
from jax import lax
from jax.experimental import pallas as pl
from jax.experimental.pallas import tpu as pltpu

D_MODEL = 1024
HEADS = 4
HEAD_DIM = 256
N_GATE = 16
SGU_GROUPS = 8
SGU_GROUP_DIM = 128
SGU_CHUNK = 128
FF_DIM = 4096
EPS = 1e-6
Z_COLS = 7 * D_MODEL

MLSTM_CHUNK = 256
STATE_ROWS = HEAD_DIM + 16

VMEM_LIMIT = 56 * 1024 * 1024

_BF = jnp.bfloat16
_F32 = jnp.float32
_NT = (((1,), (1,)), ((), ()))


def _const_spec(shape):
    nd = len(shape)
    return pl.BlockSpec(shape, lambda *_: (0,) * nd, pipeline_mode=pl.Buffered(1))


def _sigmoid(x):
    return 1.0 / (1.0 + jnp.exp(-x))


def _log_sigmoid(x):
    return jnp.minimum(x, 0.0) - jnp.log(1.0 + jnp.exp(-jnp.abs(x)))


def _gelu(x):
    return 0.5 * x * (1.0 + lax.erf(x * (2.0 ** -0.5)))


def _split3(x):
    x1 = x.astype(_BF)
    r1 = x - x1.astype(_F32)
    x2 = r1.astype(_BF)
    r2 = r1 - x2.astype(_F32)
    return x1, x2, r2.astype(_BF)


def _mod_kernel(c_ref, w_ref, b_ref, o_ref):
    c = c_ref[...]
    ca = c * _sigmoid(c)
    o_ref[...] = jnp.dot(ca, w_ref[...], precision=lax.Precision.HIGHEST,
                         preferred_element_type=_F32) + b_ref[...]


def _mod_call(c, w_ada, b_ada):
    B, D = c.shape
    N = w_ada.shape[1]
    tn = 1536
    return pl.pallas_call(
        _mod_kernel,
        grid=(N // tn,),
        in_specs=[pl.BlockSpec((B, D), lambda j: (0, 0)),
                  pl.BlockSpec((D, tn), lambda j: (0, j)),
                  pl.BlockSpec((1, tn), lambda j: (0, j))],
        out_specs=pl.BlockSpec((B, tn), lambda j: (0, j)),
        out_shape=jax.ShapeDtypeStruct((B, N), _F32),
        name="mod",
    )(c, w_ada, b_ada.reshape(1, N))


def _lane_scan(x, op, ident, reverse):
    L = x.shape[1]
    lane = lax.broadcasted_iota(jnp.int32, x.shape, 1)
    d = 1
    while d < L:
        if reverse:
            shifted = jnp.where(lane < L - d, pltpu.roll(x, L - d, axis=1), ident)
        else:
            shifted = jnp.where(lane >= d, pltpu.roll(x, d, axis=1), ident)
        x = op(x, shifted)
        d *= 2
    return x


def _in_proj_kernel(x_ref, g_ref, sc_ref, sh_ref, w_ref, wvt_ref, wg_ref, wgt_ref,
                    bc_ref, br_ref, z_ref, vt_ref, gcol_ref, grow_ref):
    tm = x_ref.shape[1]
    L = MLSTM_CHUNK
    x = x_ref[0]
    ms = jnp.mean(x * x, axis=-1, keepdims=True)
    h = x * lax.rsqrt(ms + EPS) * g_ref[...]
    h = h * (1.0 + sc_ref[0]) + sh_ref[0]
    hb = h.astype(_BF)

    gc = jnp.dot(hb, wg_ref[...], preferred_element_type=_F32) + bc_ref[...]
    ls = _log_sigmoid(gc[:, :128])
    gi = gc[:, 128:]
    rr = lax.broadcasted_iota(jnp.int32, (L, L), 0)
    ss = lax.broadcasted_iota(jnp.int32, (L, L), 1)
    tri_lo = (ss <= rr).astype(_BF)
    tri_up = (ss >= rr).astype(_BF)
    lane = lax.broadcasted_iota(jnp.int32, (L, 128), 1)
    for c in range(tm // L):
        rows = slice(c * L, (c + 1) * L)
        cf = jnp.zeros((L, 128), _F32)
        cb = jnp.zeros((L, 128), _F32)
        for piece in _split3(ls[rows]):
            cf = cf + jnp.dot(tri_lo, piece, preferred_element_type=_F32)
            cb = cb + jnp.dot(tri_up, piece, preferred_element_type=_F32)
        cum = jnp.where(lane < 8, cf, cb)
        gcol_ref[0, rows, :] = (gi[rows] - cum)[:, :N_GATE]

    gt = lax.dot_general(wgt_ref[...], hb, _NT, preferred_element_type=_F32) + br_ref[...]
    lst = _log_sigmoid(gt[:16])
    git = gt[16:]
    rowi = lax.broadcasted_iota(jnp.int32, (16, L), 0)
    for c in range(tm // L):
        cols = slice(c * L, (c + 1) * L)
        lsc = lst[:, cols]
        cum = jnp.where(rowi < 8, _lane_scan(lsc, jnp.add, 0.0, False),
                        _lane_scan(lsc, jnp.add, 0.0, True))
        out = jnp.where((rowi & 7) < 4, cum, git[:, cols] - cum)
        cmax = jnp.where(rowi < 8, _lane_scan(out, jnp.maximum, -jnp.inf, False),
                         _lane_scan(out, jnp.maximum, -jnp.inf, True))
        grow_ref[0, :16, cols] = out
        grow_ref[0, 16:, cols] = cmax

    for j in range(Z_COLS // D_MODEL):
        cols = slice(j * D_MODEL, (j + 1) * D_MODEL)
        z_ref[0, :, cols] = jnp.dot(hb, w_ref[:, cols],
                                    preferred_element_type=_F32).astype(_BF)
    vt_ref[0] = lax.dot_general(wvt_ref[...], hb, _NT,
                                preferred_element_type=_F32).astype(_BF)


def _in_proj_call(x, norm_g, sc, sh, w_bf, wvt_bf, wg_bf, wgt_bf, bias_col, bias_row, tm):
    B, S, D = x.shape
    return pl.pallas_call(
        _in_proj_kernel,
        grid=(B, S // tm),
        in_specs=[pl.BlockSpec((1, tm, D), lambda b, i: (b, i, 0)),
                  _const_spec((1, D)),
                  pl.BlockSpec((1, 1, D), lambda b, i: (b, 0, 0)),
                  pl.BlockSpec((1, 1, D), lambda b, i: (b, 0, 0)),
                  _const_spec((D, Z_COLS)),
                  _const_spec((D, D)),
                  _const_spec((D, 256)),
                  _const_spec((32, D)),
                  _const_spec((1, 256)),
                  _const_spec((32, 1))],
        out_specs=[pl.BlockSpec((1, tm, Z_COLS), lambda b, i: (b, i, 0)),
                   pl.BlockSpec((1, D, tm), lambda b, i: (b, 0, i)),
                   pl.BlockSpec((1, tm, N_GATE), lambda b, i: (b, i, 0)),
                   pl.BlockSpec((1, 32, tm), lambda b, i: (b, 0, i))],
        out_shape=[jax.ShapeDtypeStruct((B, S, Z_COLS), _BF),
                   jax.ShapeDtypeStruct((B, D, S), _BF),
                   jax.ShapeDtypeStruct((B, S, N_GATE), _F32),
                   jax.ShapeDtypeStruct((B, 32, S), _F32)],
        compiler_params=pltpu.CompilerParams(
            dimension_semantics=("arbitrary", "arbitrary"),
            vmem_limit_bytes=VMEM_LIMIT),
        name="in_proj",
    )(x, norm_g, sc, sh, w_bf, wvt_bf, wg_bf, wgt_bf, bias_col, bias_row)


def _mlstm_kernel(zq_ref, zk_ref, vt_ref, zo_ref, gr_ref, gc_ref,
                  cwq_ref, cwk_ref, cbq_ref, cbk_ref, mhg_ref,
                  y_ref,
                  q_s, k_s, acc_s, ctf_s, ctb_s, mf_s, mb_s):
    S = zq_ref.shape[1]
    L = MLSTM_CHUNK
    NC = S // L
    dh = HEAD_DIM

    row = lax.broadcasted_iota(jnp.int32, (L, dh), 0)

    def conv_silu(z_ref, cw_ref, cb_ref, t0, i):
        x = z_ref[0, pl.ds(t0, L), :].astype(_F32)
        tp = pl.multiple_of(jnp.maximum(t0 - 16, 0), 16)
        tn = pl.multiple_of(jnp.minimum(t0 + L, S - 16), 16)
        prev = z_ref[0, pl.ds(tp, 16), :].astype(_F32)[15:16, :]
        nxt = z_ref[0, pl.ds(tn, 16), :].astype(_F32)[0:1, :]
        prev = jnp.where(i > 0, prev, 0.0)
        nxt = jnp.where(i < NC - 1, nxt, 0.0)
        x_prev = jnp.where(row == 0, prev, pltpu.roll(x, 1, axis=0))
        x_next = jnp.where(row == L - 1, nxt, pltpu.roll(x, L - 1, axis=0))
        y = cb_ref[...] + x_prev * cw_ref[0:1, :]
        y = y + x * cw_ref[1:2, :]
        y = y + x_next * cw_ref[2:3, :]
        return y * _sigmoid(y)

    def phase_a(i, carry):
        t0 = pl.multiple_of(i * L, L)
        q = conv_silu(zq_ref, cwq_ref, cbq_ref, t0, i) * (dh ** -0.5)
        q_s[pl.ds(t0, L), :] = q.astype(_BF)
        k = conv_silu(zk_ref, cwk_ref, cbk_ref, t0, i)
        k_s[pl.ds(t0, L), :] = k.astype(_BF)
        acc_s[:, pl.ds(t0, L)] = jnp.zeros((dh, L), _F32)
        return carry

    lax.fori_loop(0, NC, phase_a, 0)

    ctf_s[...] = jnp.zeros_like(ctf_s)
    ctb_s[...] = jnp.zeros_like(ctb_s)
    mf_s[...] = jnp.zeros_like(mf_s)
    mb_s[...] = jnp.zeros_like(mb_s)

    si = lax.broadcasted_iota(jnp.int32, (L, L), 0)
    ti = lax.broadcasted_iota(jnp.int32, (L, L), 1)
    mask_f = si <= ti
    mask_b = si >= ti
    ones_row = (lax.broadcasted_iota(jnp.int32, (STATE_ROWS - dh, L), 0) == 0).astype(_BF)

    def chunk_step(t0, ct_s, m_s, mask, k0, j, last):
        qc = q_s[pl.ds(t0, L), :]
        kc = k_s[pl.ds(t0, L), :]
        vta = jnp.concatenate([vt_ref[0, :, pl.ds(t0, L)], ones_row], axis=0)
        gr = gr_ref[0, 0, :, pl.ds(t0, L)]
        b_row = gr[k0:k0 + 1]
        r_row = gr[k0 + 1:k0 + 2]
        cmax_row = gr[k0 + 2:k0 + 3]
        r_col = gc_ref[0, 0, pl.ds(t0, L), j:j + 1]
        m = m_s[...]

        a_row = jnp.maximum(m, cmax_row)
        s_t = lax.dot_general(kc, qc, _NT, preferred_element_type=_F32)
        p_t = jnp.exp(jnp.where(mask, r_col - a_row, -jnp.inf)) * s_t
        w_inter = jnp.exp(m - a_row)
        ct_old = ct_s[...]
        num_t = w_inter * lax.dot_general(ct_old.astype(_BF), qc, _NT,
                                          preferred_element_type=_F32)
        num_t = num_t + jnp.dot(vta, p_t.astype(_BF), preferred_element_type=_F32)
        den = num_t[dh:dh + 1, :]
        scale = 1.0 / jnp.maximum(jnp.abs(den), jnp.exp(-(b_row + a_row)))
        acc_s[:, pl.ds(t0, L)] += num_t[:dh, :] * scale

        a_last = a_row[:, last:last + 1]
        w_row = jnp.exp(r_row - a_last)
        decay = jnp.exp(m - a_last)
        vtw = (vta.astype(_F32) * w_row).astype(_BF)
        ct_s[...] = decay * ct_old + jnp.dot(vtw, kc, preferred_element_type=_F32)
        m_s[...] = b_row[:, last:last + 1] + a_last

    def phase_b(i, carry):
        tf = pl.multiple_of(i * L, L)
        tb = pl.multiple_of((NC - 1 - i) * L, L)
        chunk_step(tf, ctf_s, mf_s, mask_f, 0, 0, L - 1)
        chunk_step(tb, ctb_s, mb_s, mask_b, 3, 1, 0)
        return carry

    lax.fori_loop(0, NC, phase_b, 0)

    def phase_c(i, carry):
        t0 = pl.multiple_of(i * L, L)
        h = acc_s[:, pl.ds(t0, L)].T
        hn = h * lax.rsqrt(jnp.mean(h * h, axis=-1, keepdims=True) + EPS) * mhg_ref[...]
        o = zo_ref[0, pl.ds(t0, L), :].astype(_F32)
        y_ref[0, pl.ds(t0, L), :] = (hn * _sigmoid(o)).astype(_BF)
        return carry

    lax.fori_loop(0, NC, phase_c, 0)


def _mlstm_call(z, vt, grow, gcol, conv_w, conv_b, mh_g):
    B, S, _ = z.shape
    dh = HEAD_DIM
    H = HEADS

    def zspec(off):
        return pl.BlockSpec((1, S, dh), lambda b, h, off=off: (b, 0, off + h))

    def wspec(rows, off):
        return pl.BlockSpec((rows, dh), lambda b, h, off=off: (0, off + h))

    return pl.pallas_call(
        _mlstm_kernel,
        grid=(B, H),
        in_specs=[zspec(0), zspec(H),
                  pl.BlockSpec((1, dh, S), lambda b, h: (b, h, 0)),
                  zspec(2 * H),
                  pl.BlockSpec((1, 1, 8, S), lambda b, h: (b, h, 0, 0)),
                  pl.BlockSpec((1, 1, S, 2), lambda b, h: (b, h, 0, 0)),
                  wspec(3, 0), wspec(3, H), wspec(1, 0), wspec(1, H), wspec(1, 0)],
        out_specs=pl.BlockSpec((1, S, dh), lambda b, h: (b, 0, h)),
        out_shape=jax.ShapeDtypeStruct((B, S, H * dh), _BF),
        scratch_shapes=[pltpu.VMEM((S, dh), _BF), pltpu.VMEM((S, dh), _BF),
                        pltpu.VMEM((dh, S), _F32),
                        pltpu.VMEM((STATE_ROWS, dh), _F32), pltpu.VMEM((STATE_ROWS, dh), _F32),
                        pltpu.VMEM((1, 1), _F32), pltpu.VMEM((1, 1), _F32)],
        compiler_params=pltpu.CompilerParams(
            dimension_semantics=("arbitrary", "arbitrary"),
            vmem_limit_bytes=VMEM_LIMIT),
        name="mlstm",
    )(z, z, vt, z, grow, gcol, conv_w, conv_w, conv_b, conv_b, mh_g)


def _mix_kernel(x_ref, zu_ref, zv_ref, zga_ref, zgb_ref, ya_ref, g1_ref, lng_ref, lnb_ref,
                ws_ref, bs_ref, wo_ref, o_ref, mix_s):
    tm = x_ref.shape[1]
    P = SGU_CHUNK
    C = SGU_GROUP_DIM
    for j in range(tm // P):
        rows = slice(j * P, (j + 1) * P)
        gv = _gelu(zv_ref[0, rows, :].astype(_F32))
        mu = jnp.mean(gv, axis=-1, keepdims=True)
        dv = gv - mu
        var = jnp.mean(dv * dv, axis=-1, keepdims=True)
        vn = (dv * lax.rsqrt(var + EPS) * lng_ref[...] + lnb_ref[...]).astype(_BF)
        s = jnp.concatenate(
            [jnp.dot(ws_ref[g], vn[:, g * C:(g + 1) * C], preferred_element_type=_F32)
             for g in range(SGU_GROUPS)], axis=1) + bs_ref[...]
        yb = _gelu(zu_ref[0, rows, :].astype(_F32)) * s
        ga = _sigmoid(zga_ref[0, rows, :].astype(_F32))
        gb = _sigmoid(zgb_ref[0, rows, :].astype(_F32))
        mix_s[rows, :] = (ga * ya_ref[0, rows, :].astype(_F32) + gb * yb).astype(_BF)
    upd = jnp.dot(mix_s[...], wo_ref[...], preferred_element_type=_F32)
    o_ref[0] = x_ref[0] + g1_ref[0] * upd


def _mix_call(x, z, ya, g1, ln_g, ln_b, ws_bf, bs_full, wo_bf, tm):
    B, S, D = x.shape

    def zspec(j):
        return pl.BlockSpec((1, tm, D), lambda b, i, j=j: (b, i, j))

    return pl.pallas_call(
        _mix_kernel,
        grid=(B, S // tm),
        in_specs=[zspec(0), zspec(3), zspec(4), zspec(5), zspec(6), zspec(0),
                  pl.BlockSpec((1, 1, D), lambda b, i: (b, 0, 0)),
                  _const_spec((1, D)), _const_spec((1, D)),
                  _const_spec((SGU_GROUPS, SGU_CHUNK, SGU_CHUNK)),
                  _const_spec((SGU_CHUNK, D)),
                  _const_spec((D, D))],
        out_specs=pl.BlockSpec((1, tm, D), lambda b, i: (b, i, 0)),
        out_shape=jax.ShapeDtypeStruct((B, S, D), _F32),
        scratch_shapes=[pltpu.VMEM((tm, D), _BF)],
        compiler_params=pltpu.CompilerParams(
            dimension_semantics=("arbitrary", "arbitrary"),
            vmem_limit_bytes=VMEM_LIMIT),
        name="mix",
    )(x, z, z, z, z, ya, g1, ln_g, ln_b, ws_bf, bs_full, wo_bf)


def _ffn_kernel(x_ref, g_ref, sc_ref, sh_ref, g2_ref, w1_ref, w2_ref, nf_ref, o_ref):
    x = x_ref[0]
    ms = jnp.mean(x * x, axis=-1, keepdims=True)
    h = x * lax.rsqrt(ms + EPS) * g_ref[...]
    hb = (h * (1.0 + sc_ref[0]) + sh_ref[0]).astype(_BF)
    ff = jnp.zeros(x.shape, _F32)
    for j in range(FF_DIM // D_MODEL):
        cols = slice(j * D_MODEL, (j + 1) * D_MODEL)
        a = jnp.maximum(jnp.dot(hb, w1_ref[:, cols], preferred_element_type=_F32), 0.0)
        ff = ff + jnp.dot((a * a).astype(_BF), w2_ref[cols, :], preferred_element_type=_F32)
    x2 = x + g2_ref[0] * ff
    ms2 = jnp.mean(x2 * x2, axis=-1, keepdims=True)
    o_ref[0] = x2 * lax.rsqrt(ms2 + EPS) * nf_ref[...]


def _ffn_call(x1, norm_g, sc, sh, g2, w1_bf, w2_bf, normf_g, tm):
    B, S, D = x1.shape
    vec = pl.BlockSpec((1, 1, D), lambda b, i: (b, 0, 0))
    return pl.pallas_call(
        _ffn_kernel,
        grid=(B, S // tm),
        in_specs=[pl.BlockSpec((1, tm, D), lambda b, i: (b, i, 0)),
                  _const_spec((1, D)), vec, vec, vec,
                  _const_spec((D, FF_DIM)), _const_spec((FF_DIM, D)),
                  _const_spec((1, D))],
        out_specs=pl.BlockSpec((1, tm, D), lambda b, i: (b, i, 0)),
        out_shape=jax.ShapeDtypeStruct((B, S, D), _F32),
        compiler_params=pltpu.CompilerParams(
            dimension_semantics=("arbitrary", "arbitrary"),
            vmem_limit_bytes=VMEM_LIMIT),
        name="ffn",
    )(x1, norm_g, sc, sh, g2, w1_bf, w2_bf, normf_g)


def _gate_weights(w_if, b_if):
    D = w_if.shape[0]
    H = HEADS
    i_f, f_f, i_b, f_b = (w_if[:, k * H:(k + 1) * H] for k in range(4))
    bi_f, bf_f, bi_b, bf_b = (b_if[k] for k in range(4))
    zw = jnp.zeros((D, H), w_if.dtype)
    zb = jnp.zeros((H,), b_if.dtype)
    w_f = jnp.concatenate([f_f, f_f, f_b, f_b], axis=1)
    w_i = jnp.concatenate([zw, i_f, zw, i_b], axis=1)
    b_f = jnp.concatenate([bf_f, bf_f, bf_b, bf_b])
    b_i = jnp.concatenate([zb, bi_f, zb, bi_b])
    padw = jnp.zeros((D, 128 - 4 * H), w_if.dtype)
    padb = jnp.zeros((128 - 4 * H,), b_if.dtype)
    w_col = jnp.concatenate([w_f, padw, w_i, padw], axis=1)
    b_col = jnp.concatenate([b_f, padb, b_i, padb]).reshape(1, 256)
    w_row = jnp.concatenate([w_f, w_i], axis=1).T
    b_row = jnp.concatenate([b_f, b_i]).reshape(32, 1)
    return w_col, b_col, w_row, b_row


def kernel(x, c, w_ada, b_ada, norm1_g, norm2_g, w_in, b_if, conv_w, conv_b, mh_g,
           ln_v_g, ln_v_b, w_s, b_s, w_out, w1, w2, normf_g):
    B, S, D = x.shape
    H = HEADS
    assert w_ada.shape[0] == 1, "single layer"

    mod = _mod_call(c, w_ada[0], b_ada[0])
    sh1, sc1, g1, sh2, sc2, g2 = (mod[:, k * D:(k + 1) * D].reshape(B, 1, D) for k in range(6))

    w_in0 = w_in[0]
    w_z = jnp.concatenate([w_in0[:, :2 * D], w_in0[:, 3 * D:8 * D]], axis=1).astype(_BF)
    w_vt = w_in0[:, 2 * D:3 * D].T.astype(_BF)
    w_col, b_col, w_row, b_row = _gate_weights(w_in0[:, 8 * D:], b_if[0])
    z, vt, gcol, grow = _in_proj_call(x, norm1_g, sc1, sh1, w_z, w_vt, w_col.astype(_BF),
                                      w_row.astype(_BF), b_col, b_row, tm=512)

    g8 = grow.reshape(B, 8, H, S)
    grow_h = jnp.stack([g8[:, k] for k in (0, 1, 5, 2, 3, 7, 4, 6)], axis=2)
    g4 = gcol.reshape(B, S, 4, H)
    gcol_h = jnp.stack([g4[:, :, 1], g4[:, :, 3]], axis=-1).transpose(0, 2, 1, 3)

    ya = _mlstm_call(z, vt, grow_h, gcol_h, conv_w[0], conv_b, mh_g)

    bs_full = jnp.repeat(b_s[0].T, SGU_GROUP_DIM, axis=1)
    x1 = _mix_call(x, z, ya, g1, ln_v_g, ln_v_b, w_s[0].astype(_BF), bs_full,
                   w_out[0].astype(_BF), tm=512)

    return _ffn_call(x1, norm2_g, sc2, sh2, g2, w1[0].astype(_BF), w2[0].astype(_BF),
                     normf_g.reshape(1, D), tm=512)
```

```python
import jax
import jax.numpy as jnp
from jax import lax
from jax.experimental import pallas as pl
from jax.experimental.pallas import tpu as pltpu

D_MODEL = 1024
HEADS = 4
HEAD_DIM = 256
N_GATE = 16
SGU_GROUPS = 8
SGU_GROUP_DIM = 128
SGU_CHUNK = 128
FF_DIM = 4096
EPS = 1e-6
Z_COLS = 7 * D_MODEL

MLSTM_CHUNK = 256
STATE_ROWS = HEAD_DIM + 16
HALO = 16
CONV_ROWS = 128

VMEM_LIMIT = 56 * 1024 * 1024

_BF = jnp.bfloat16
_F32 = jnp.float32
_NT = (((1,), (1,)), ((), ()))


def _const_spec(shape):
    nd = len(shape)
    return pl.BlockSpec(shape, lambda *_: (0,) * nd, pipeline_mode=pl.Buffered(1))


def _sigmoid(x):
    return 1.0 / (1.0 + jnp.exp(-x))


def _log_sigmoid(x):
    return jnp.minimum(x, 0.0) - jnp.log(1.0 + jnp.exp(-jnp.abs(x)))


def _gelu(x):
    return 0.5 * x * (1.0 + lax.erf(x * (2.0 ** -0.5)))


def _split3(x):
    x1 = x.astype(_BF)
    r1 = x - x1.astype(_F32)
    x2 = r1.astype(_BF)
    r2 = r1 - x2.astype(_F32)
    return x1, x2, r2.astype(_BF)


def _mod_kernel(c_ref, w_ref, b_ref, o_ref):
    c = c_ref[...]
    ca = c * _sigmoid(c)
    o_ref[...] = jnp.dot(ca, w_ref[...], precision=lax.Precision.HIGHEST,
                         preferred_element_type=_F32) + b_ref[...]


def _mod_call(c, w_ada, b_ada):
    B, D = c.shape
    N = w_ada.shape[1]
    tn = 1536
    return pl.pallas_call(
        _mod_kernel,
        grid=(N // tn,),
        in_specs=[pl.BlockSpec((B, D), lambda j: (0, 0)),
                  pl.BlockSpec((D, tn), lambda j: (0, j)),
                  pl.BlockSpec((1, tn), lambda j: (0, j))],
        out_specs=pl.BlockSpec((B, tn), lambda j: (0, j)),
        out_shape=jax.ShapeDtypeStruct((B, N), _F32),
        name="mod",
    )(c, w_ada, b_ada.reshape(1, N))


def _lane_scan(x, op, ident, reverse):
    L = x.shape[1]
    lane = lax.broadcasted_iota(jnp.int32, x.shape, 1)
    d = 1
    while d < L:
        if reverse:
            shifted = jnp.where(lane < L - d, pltpu.roll(x, L - d, axis=1), ident)
        else:
            shifted = jnp.where(lane >= d, pltpu.roll(x, d, axis=1), ident)
        x = op(x, shifted)
        d *= 2
    return x


def _in_proj_kernel(x_ref, xp_ref, xn_ref, g_ref, sc_ref, sh_ref, w_ref, wvt_ref, wg_ref,
                    wgt_ref, bc_ref, br_ref, cw_ref, cb_ref, z_ref, vt_ref, gcol_ref, grow_ref,
                    zq_s, zk_s):
    tm = x_ref.shape[1]
    L = MLSTM_CHUNK
    i = pl.program_id(1)

    def norm_mod(x):
        ms = jnp.mean(x * x, axis=-1, keepdims=True)
        h = x * lax.rsqrt(ms + EPS) * g_ref[...]
        return (h * (1.0 + sc_ref[0]) + sh_ref[0]).astype(_BF)

    hb = norm_mod(x_ref[0])
    hb_prev = jnp.where(i > 0, norm_mod(xp_ref[0]), jnp.zeros((HALO, D_MODEL), _BF))
    hb_next = jnp.where(i < pl.num_programs(1) - 1, norm_mod(xn_ref[0]),
                        jnp.zeros((HALO, D_MODEL), _BF))
    hb_ext = jnp.concatenate([hb_prev, hb, hb_next], axis=0)

    gc = jnp.dot(hb, wg_ref[...], preferred_element_type=_F32) + bc_ref[...]
    ls = _log_sigmoid(gc[:, :128])
    gi = gc[:, 128:]
    rr = lax.broadcasted_iota(jnp.int32, (L, L), 0)
    ss = lax.broadcasted_iota(jnp.int32, (L, L), 1)
    tri_lo = (ss <= rr).astype(_BF)
    tri_up = (ss >= rr).astype(_BF)
    lane = lax.broadcasted_iota(jnp.int32, (L, 128), 1)
    for c in range(tm // L):
        rows = slice(c * L, (c + 1) * L)
        cf = jnp.zeros((L, 128), _F32)
        cb = jnp.zeros((L, 128), _F32)
        for piece in _split3(ls[rows]):
            cf = cf + jnp.dot(tri_lo, piece, preferred_element_type=_F32)
            cb = cb + jnp.dot(tri_up, piece, preferred_element_type=_F32)
        cum = jnp.where(lane < 8, cf, cb)
        gcol_ref[0, rows, :] = (gi[rows] - cum)[:, :N_GATE]

    gt = lax.dot_general(wgt_ref[...], hb, _NT, preferred_element_type=_F32) + br_ref[...]
    lst = _log_sigmoid(gt[:16])
    git = gt[16:]
    rowi = lax.broadcasted_iota(jnp.int32, (16, L), 0)
    for c in range(tm // L):
        cols = slice(c * L, (c + 1) * L)
        lsc = lst[:, cols]
        cum = jnp.where(rowi < 8, _lane_scan(lsc, jnp.add, 0.0, False),
                        _lane_scan(lsc, jnp.add, 0.0, True))
        out = jnp.where((rowi & 7) < 4, cum, git[:, cols] - cum)
        cmax = jnp.where(rowi < 8, _lane_scan(out, jnp.maximum, -jnp.inf, False),
                         _lane_scan(out, jnp.maximum, -jnp.inf, True))
        grow_ref[0, :16, cols] = out
        grow_ref[0, 16:, cols] = cmax

    for j, ze_s in enumerate((zq_s, zk_s)):
        cols = slice(j * D_MODEL, (j + 1) * D_MODEL)
        ze_s[...] = jnp.dot(hb_ext, w_ref[:, cols], preferred_element_type=_F32)
        for r0 in range(0, tm, CONV_ROWS):
            n = CONV_ROWS + 16
            zb = ze_s[pl.ds(HALO - 8 + r0, n), :]
            mid = slice(8, 8 + CONV_ROWS)
            y = cb_ref[:, cols] + pltpu.roll(zb, 1, axis=0)[mid] * cw_ref[0:1, cols]
            y = y + zb[mid] * cw_ref[1:2, cols]
            y = y + pltpu.roll(zb, n - 1, axis=0)[mid] * cw_ref[2:3, cols]
            y = y * _sigmoid(y)
            if j == 0:
                y = y * (HEAD_DIM ** -0.5)
            z_ref[0, pl.ds(r0, CONV_ROWS), cols] = y.astype(_BF)

    for j in range(2, Z_COLS // D_MODEL):
        cols = slice(j * D_MODEL, (j + 1) * D_MODEL)
        z_ref[0, :, cols] = jnp.dot(hb, w_ref[:, cols],
                                    preferred_element_type=_F32).astype(_BF)
    vt_ref[0] = lax.dot_general(wvt_ref[...], hb, _NT,
                                preferred_element_type=_F32).astype(_BF)


def _in_proj_call(x, norm_g, sc, sh, w_bf, wvt_bf, wg_bf, wgt_bf, bias_col, bias_row,
                  conv_w, conv_b, tm):
    B, S, D = x.shape
    r = tm // HALO
    last = S // HALO - 1
    return pl.pallas_call(
        _in_proj_kernel,
        grid=(B, S // tm),
        in_specs=[pl.BlockSpec((1, tm, D), lambda b, i: (b, i, 0)),
                  pl.BlockSpec((1, HALO, D), lambda b, i: (b, jnp.maximum(i * r - 1, 0), 0)),
                  pl.BlockSpec((1, HALO, D),
                               lambda b, i: (b, jnp.minimum((i + 1) * r, last), 0)),
                  _const_spec((1, D)),
                  pl.BlockSpec((1, 1, D), lambda b, i: (b, 0, 0)),
                  pl.BlockSpec((1, 1, D), lambda b, i: (b, 0, 0)),
                  _const_spec((D, Z_COLS)),
                  _const_spec((D, D)),
                  _const_spec((D, 256)),
                  _const_spec((32, D)),
                  _const_spec((1, 256)),
                  _const_spec((32, 1)),
                  _const_spec((3, 2 * D)),
                  _const_spec((1, 2 * D))],
        out_specs=[pl.BlockSpec((1, tm, Z_COLS), lambda b, i: (b, i, 0)),
                   pl.BlockSpec((1, D, tm), lambda b, i: (b, 0, i)),
                   pl.BlockSpec((1, tm, N_GATE), lambda b, i: (b, i, 0)),
                   pl.BlockSpec((1, 32, tm), lambda b, i: (b, 0, i))],
        out_shape=[jax.ShapeDtypeStruct((B, S, Z_COLS), _BF),
                   jax.ShapeDtypeStruct((B, D, S), _BF),
                   jax.ShapeDtypeStruct((B, S, N_GATE), _F32),
                   jax.ShapeDtypeStruct((B, 32, S), _F32)],
        scratch_shapes=[pltpu.VMEM((tm + 2 * HALO, D), _F32),
                        pltpu.VMEM((tm + 2 * HALO, D), _F32)],
        compiler_params=pltpu.CompilerParams(
            dimension_semantics=("arbitrary", "arbitrary"),
            vmem_limit_bytes=VMEM_LIMIT),
        name="in_proj",
    )(x, x, x, norm_g, sc, sh, w_bf, wvt_bf, wg_bf, wgt_bf, bias_col, bias_row, conv_w, conv_b)


def _mlstm_kernel(q_ref, k_ref, vt_ref, zo_ref, gr_ref, gc_ref, mhg_ref,
                  y_ref,
                  acc_s, ctf_s, ctb_s, mf_s, mb_s):
    S = q_ref.shape[1]
    L = MLSTM_CHUNK
    NC = S // L
    dh = HEAD_DIM

    acc_s[...] = jnp.zeros_like(acc_s)
    ctf_s[...] = jnp.zeros_like(ctf_s)
    ctb_s[...] = jnp.zeros_like(ctb_s)
    mf_s[...] = jnp.zeros_like(mf_s)
    mb_s[...] = jnp.zeros_like(mb_s)

    si = lax.broadcasted_iota(jnp.int32, (L, L), 0)
    ti = lax.broadcasted_iota(jnp.int32, (L, L), 1)
    mask_f = si <= ti
    mask_b = si >= ti
    ones_row = (lax.broadcasted_iota(jnp.int32, (STATE_ROWS - dh, L), 0) == 0).astype(_BF)

    def chunk_step(t0, ct_s, m_s, mask, k0, j, last):
        qc = q_ref[0, pl.ds(t0, L), :]
        kc = k_ref[0, pl.ds(t0, L), :]
        vta = jnp.concatenate([vt_ref[0, :, pl.ds(t0, L)], ones_row], axis=0)
        gr = gr_ref[0, 0, :, pl.ds(t0, L)]
        b_row = gr[k0:k0 + 1]
        r_row = gr[k0 + 1:k0 + 2]
        cmax_row = gr[k0 + 2:k0 + 3]
        r_col = gc_ref[0, 0, pl.ds(t0, L), j:j + 1]
        m = m_s[...]

        a_row = jnp.maximum(m, cmax_row)
        s_t = lax.dot_general(kc, qc, _NT, preferred_element_type=_F32)
        p_t = jnp.exp(jnp.where(mask, r_col - a_row, -jnp.inf)) * s_t
        w_inter = jnp.exp(m - a_row)
        ct_old = ct_s[...]
        num_t = w_inter * lax.dot_general(ct_old.astype(_BF), qc, _NT,
                                          preferred_element_type=_F32)
        num_t = num_t + jnp.dot(vta, p_t.astype(_BF), preferred_element_type=_F32)
        den = num_t[dh:dh + 1, :]
        scale = 1.0 / jnp.maximum(jnp.abs(den), jnp.exp(-(b_row + a_row)))
        acc_s[:, pl.ds(t0, L)] += num_t[:dh, :] * scale

        a_last = a_row[:, last:last + 1]
        w_row = jnp.exp(r_row - a_last)
        decay = jnp.exp(m - a_last)
        vtw = (vta.astype(_F32) * w_row).astype(_BF)
        ct_s[...] = decay * ct_old + jnp.dot(vtw, kc, preferred_element_type=_F32)
        m_s[...] = b_row[:, last:last + 1] + a_last

    def phase_b(i, carry):
        tf = pl.multiple_of(i * L, L)
        tb = pl.multiple_of((NC - 1 - i) * L, L)
        chunk_step(tf, ctf_s, mf_s, mask_f, 0, 0, L - 1)
        chunk_step(tb, ctb_s, mb_s, mask_b, 3, 1, 0)
        return carry

    lax.fori_loop(0, NC, phase_b, 0, unroll=2)

    def phase_c(i, carry):
        t0 = pl.multiple_of(i * L, L)
        h = acc_s[:, pl.ds(t0, L)].T
        hn = h * lax.rsqrt(jnp.mean(h * h, axis=-1, keepdims=True) + EPS) * mhg_ref[...]
        o = zo_ref[0, pl.ds(t0, L), :].astype(_F32)
        y_ref[0, pl.ds(t0, L), :] = (hn * _sigmoid(o)).astype(_BF)
        return carry

    lax.fori_loop(0, NC, phase_c, 0)


def _mlstm_call(z, vt, grow, gcol, mh_g):
    B, S, _ = z.shape
    dh = HEAD_DIM
    H = HEADS

    def zspec(off):
        return pl.BlockSpec((1, S, dh), lambda b, h, off=off: (b, 0, off + h))

    return pl.pallas_call(
        _mlstm_kernel,
        grid=(B, H),
        in_specs=[zspec(0), zspec(H),
                  pl.BlockSpec((1, dh, S), lambda b, h: (b, h, 0)),
                  zspec(2 * H),
                  pl.BlockSpec((1, 1, 8, S), lambda b, h: (b, h, 0, 0)),
                  pl.BlockSpec((1, 1, S, 2), lambda b, h: (b, h, 0, 0)),
                  pl.BlockSpec((1, dh), lambda b, h: (0, h))],
        out_specs=pl.BlockSpec((1, S, dh), lambda b, h: (b, 0, h)),
        out_shape=jax.ShapeDtypeStruct((B, S, H * dh), _BF),
        scratch_shapes=[pltpu.VMEM((dh, S), _F32),
                        pltpu.VMEM((STATE_ROWS, dh), _F32), pltpu.VMEM((STATE_ROWS, dh), _F32),
                        pltpu.VMEM((1, 1), _F32), pltpu.VMEM((1, 1), _F32)],
        compiler_params=pltpu.CompilerParams(
            dimension_semantics=("arbitrary", "arbitrary"),
            vmem_limit_bytes=VMEM_LIMIT),
        name="mlstm",
    )(z, z, vt, z, grow, gcol, mh_g)


def _mix_kernel(x_ref, zu_ref, zv_ref, zga_ref, zgb_ref, ya_ref, g1_ref, lng_ref, lnb_ref,
                ws_ref, bs_ref, wo_ref, o_ref, mix_s):
    tm = x_ref.shape[1]
    P = SGU_CHUNK
    C = SGU_GROUP_DIM
    for j in range(tm // P):
        rows = slice(j * P, (j + 1) * P)
        gv = _gelu(zv_ref[0, rows, :].astype(_F32))
        mu = jnp.mean(gv, axis=-1, keepdims=True)
        dv = gv - mu
        var = jnp.mean(dv * dv, axis=-1, keepdims=True)
        vn = (dv * lax.rsqrt(var + EPS) * lng_ref[...] + lnb_ref[...]).astype(_BF)
        s = jnp.concatenate(
            [jnp.dot(ws_ref[g], vn[:, g * C:(g + 1) * C], preferred_element_type=_F32)
             for g in range(SGU_GROUPS)], axis=1) + bs_ref[...]
        yb = _gelu(zu_ref[0, rows, :].astype(_F32)) * s
        ga = _sigmoid(zga_ref[0, rows, :].astype(_F32))
        gb = _sigmoid(zgb_ref[0, rows, :].astype(_F32))
        mix_s[rows, :] = (ga * ya_ref[0, rows, :].astype(_F32) + gb * yb).astype(_BF)
    upd = jnp.dot(mix_s[...], wo_ref[...], preferred_element_type=_F32)
    o_ref[0] = x_ref[0] + g1_ref[0] * upd


def _mix_call(x, z, ya, g1, ln_g, ln_b, ws_bf, bs_full, wo_bf, tm):
    B, S, D = x.shape

    def zspec(j):
        return pl.BlockSpec((1, tm, D), lambda b, i, j=j: (b, i, j))

    return pl.pallas_call(
        _mix_kernel,
        grid=(B, S // tm),
        in_specs=[zspec(0), zspec(3), zspec(4), zspec(5), zspec(6), zspec(0),
                  pl.BlockSpec((1, 1, D), lambda b, i: (b, 0, 0)),
                  _const_spec((1, D)), _const_spec((1, D)),
                  _const_spec((SGU_GROUPS, SGU_CHUNK, SGU_CHUNK)),
                  _const_spec((SGU_CHUNK, D)),
                  _const_spec((D, D))],
        out_specs=pl.BlockSpec((1, tm, D), lambda b, i: (b, i, 0)),
        out_shape=jax.ShapeDtypeStruct((B, S, D), _F32),
        scratch_shapes=[pltpu.VMEM((tm, D), _BF)],
        compiler_params=pltpu.CompilerParams(
            dimension_semantics=("arbitrary", "arbitrary"),
            vmem_limit_bytes=VMEM_LIMIT),
        name="mix",
    )(x, z, z, z, z, ya, g1, ln_g, ln_b, ws_bf, bs_full, wo_bf)


def _ffn_kernel(x_ref, g_ref, sc_ref, sh_ref, g2_ref, w1_ref, w2_ref, nf_ref, o_ref):
    x = x_ref[0]
    ms = jnp.mean(x * x, axis=-1, keepdims=True)
    h = x * lax.rsqrt(ms + EPS) * g_ref[...]
    hb = (h * (1.0 + sc_ref[0]) + sh_ref[0]).astype(_BF)
    ff = jnp.zeros(x.shape, _F32)
    for j in range(FF_DIM // D_MODEL):
        cols = slice(j * D_MODEL, (j + 1) * D_MODEL)
        a = jnp.maximum(jnp.dot(hb, w1_ref[:, cols], preferred_element_type=_F32), 0.0)
        ff = ff + jnp.dot((a * a).astype(_BF), w2_ref[cols, :], preferred_element_type=_F32)
    x2 = x + g2_ref[0] * ff
    ms2 = jnp.mean(x2 * x2, axis=-1, keepdims=True)
    o_ref[0] = x2 * lax.rsqrt(ms2 + EPS) * nf_ref[...]


def _ffn_call(x1, norm_g, sc, sh, g2, w1_bf, w2_bf, normf_g, tm):
    B, S, D = x1.shape
    vec = pl.BlockSpec((1, 1, D), lambda b, i: (b, 0, 0))
    return pl.pallas_call(
        _ffn_kernel,
        grid=(B, S // tm),
        in_specs=[pl.BlockSpec((1, tm, D), lambda b, i: (b, i, 0)),
                  _const_spec((1, D)), vec, vec, vec,
                  _const_spec((D, FF_DIM)), _const_spec((FF_DIM, D)),
                  _const_spec((1, D))],
        out_specs=pl.BlockSpec((1, tm, D), lambda b, i: (b, i, 0)),
        out_shape=jax.ShapeDtypeStruct((B, S, D), _F32),
        compiler_params=pltpu.CompilerParams(
            dimension_semantics=("arbitrary", "arbitrary"),
            vmem_limit_bytes=VMEM_LIMIT),
        name="ffn",
    )(x1, norm_g, sc, sh, g2, w1_bf, w2_bf, normf_g)


def _gate_weights(w_if, b_if):
    D = w_if.shape[0]
    H = HEADS
    i_f, f_f, i_b, f_b = (w_if[:, k * H:(k + 1) * H] for k in range(4))
    bi_f, bf_f, bi_b, bf_b = (b_if[k] for k in range(4))
    zw = jnp.zeros((D, H), w_if.dtype)
    zb = jnp.zeros((H,), b_if.dtype)
    w_f = jnp.concatenate([f_f, f_f, f_b, f_b], axis=1)
    w_i = jnp.concatenate([zw, i_f, zw, i_b], axis=1)
    b_f = jnp.concatenate([bf_f, bf_f, bf_b, bf_b])
    b_i = jnp.concatenate([zb, bi_f, zb, bi_b])
    padw = jnp.zeros((D, 128 - 4 * H), w_if.dtype)
    padb = jnp.zeros((128 - 4 * H,), b_if.dtype)
    w_col = jnp.concatenate([w_f, padw, w_i, padw], axis=1)
    b_col = jnp.concatenate([b_f, padb, b_i, padb]).reshape(1, 256)
    w_row = jnp.concatenate([w_f, w_i], axis=1).T
    b_row = jnp.concatenate([b_f, b_i]).reshape(32, 1)
    return w_col, b_col, w_row, b_row


def kernel(x, c, w_ada, b_ada, norm1_g, norm2_g, w_in, b_if, conv_w, conv_b, mh_g,
           ln_v_g, ln_v_b, w_s, b_s, w_out, w1, w2, normf_g):
    B, S, D = x.shape
    H = HEADS
    assert w_ada.shape[0] == 1, "single layer"

    mod = _mod_call(c, w_ada[0], b_ada[0])
    sh1, sc1, g1, sh2, sc2, g2 = (mod[:, k * D:(k + 1) * D].reshape(B, 1, D) for k in range(6))

    w_in0 = w_in[0]
    w_z = jnp.concatenate([w_in0[:, :2 * D], w_in0[:, 3 * D:8 * D]], axis=1).astype(_BF)
    w_vt = w_in0[:, 2 * D:3 * D].T.astype(_BF)
    w_col, b_col, w_row, b_row = _gate_weights(w_in0[:, 8 * D:], b_if[0])
    z, vt, gcol, grow = _in_proj_call(x, norm1_g, sc1, sh1, w_z, w_vt, w_col.astype(_BF),
                                      w_row.astype(_BF), b_col, b_row, conv_w[0], conv_b, tm=512)

    g8 = grow.reshape(B, 8, H, S)
    grow_h = jnp.stack([g8[:, k] for k in (0, 1, 5, 2, 3, 7, 4, 6)], axis=2)
    g4 = gcol.reshape(B, S, 4, H)
    gcol_h = jnp.stack([g4[:, :, 1], g4[:, :, 3]], axis=-1).transpose(0, 2, 1, 3)

    ya = _mlstm_call(z, vt, grow_h, gcol_h, mh_g)

    bs_full = jnp.repeat(b_s[0].T, SGU_GROUP_DIM, axis=1)
    x1 = _mix_call(x, z, ya, g1, ln_v_g, ln_v_b, w_s[0].astype(_BF), bs_full,
                   w_out[0].astype(_BF), tm=512)

    return _ffn_call(x1, norm2_g, sc2, sh2, g2, w1[0].astype(_BF), w2[0].astype(_BF),
                     normf_g.reshape(1, D), tm=512)
```

```python
import jax
import jax.numpy as jnp
from jax import lax
from jax.experimental import pallas as pl
from jax.experimental.pallas import tpu as pltpu

D_MODEL = 1024
HEADS = 4
HEAD_DIM = 256
N_GATE = 16
SGU_GROUPS = 8
SGU_GROUP_DIM = 128
SGU_CHUNK = 128
FF_DIM = 4096
EPS = 1e-6
Z_COLS = 7 * D_MODEL

MLSTM_CHUNK = 256
STATE_ROWS = HEAD_DIM + 16
HALO = 16
GATE_SLOTS = 8
CONV_ROWS = 64
CONV_COLS = 128
MXU_COLS = 256

VMEM_LIMIT = 56 * 1024 * 1024

_BF = jnp.bfloat16
_F32 = jnp.float32
_NT = (((1,), (1,)), ((), ()))
_LOG2E = 1.4426950408889634


def _const_spec(shape):
    nd = len(shape)
    return pl.BlockSpec(shape, lambda *_: (0,) * nd, pipeline_mode=pl.Buffered(1))


def _sigmoid(x):
    return 1.0 / (1.0 + jnp.exp(-x))


def _log_sigmoid(x):
    return jnp.minimum(x, 0.0) - jnp.log(1.0 + jnp.exp(-jnp.abs(x)))


def _gelu(x):
    return 0.5 * x * (1.0 + lax.erf(x * (2.0 ** -0.5)))


def _split3(x):
    x1 = x.astype(_BF)
    r1 = x - x1.astype(_F32)
    x2 = r1.astype(_BF)
    r2 = r1 - x2.astype(_F32)
    return x1, x2, r2.astype(_BF)


def _mod_kernel(c_ref, w_ref, b_ref, o_ref):
    c = c_ref[...]
    ca = c * _sigmoid(c)
    o_ref[...] = jnp.dot(ca, w_ref[...], precision=lax.Precision.HIGHEST,
                         preferred_element_type=_F32) + b_ref[...]


def _mod_call(c, w_ada, b_ada):
    B, D = c.shape
    N = w_ada.shape[1]
    tn = 1536
    return pl.pallas_call(
        _mod_kernel,
        grid=(N // tn,),
        in_specs=[pl.BlockSpec((B, D), lambda j: (0, 0)),
                  pl.BlockSpec((D, tn), lambda j: (0, j)),
                  pl.BlockSpec((1, tn), lambda j: (0, j))],
        out_specs=pl.BlockSpec((B, tn), lambda j: (0, j)),
        out_shape=jax.ShapeDtypeStruct((B, N), _F32),
        name="mod",
    )(c, w_ada, b_ada.reshape(1, N))


def _lane_scan(x, op, ident, reverse):
    L = x.shape[1]
    lane = lax.broadcasted_iota(jnp.int32, x.shape, 1)
    d = 1
    while d < L:
        if reverse:
            shifted = jnp.where(lane < L - d, pltpu.roll(x, L - d, axis=1), ident)
        else:
            shifted = jnp.where(lane >= d, pltpu.roll(x, d, axis=1), ident)
        x = op(x, shifted)
        d *= 2
    return x


def _in_proj_kernel(x_ref, xp_ref, xn_ref, g_ref, sc_ref, sh_ref, w_ref, wvt_ref, wg_ref,
                    wgt_ref, bc_ref, br_ref, cw_ref, cb_ref, z_ref, vt_ref, gcol_ref, grow_ref,
                    zq_s, zk_s):
    tm = x_ref.shape[1]
    L = MLSTM_CHUNK
    i = pl.program_id(1)

    def norm_mod(x):
        ms = jnp.mean(x * x, axis=-1, keepdims=True)
        h = x * lax.rsqrt(ms + EPS) * g_ref[...]
        return (h * (1.0 + sc_ref[0]) + sh_ref[0]).astype(_BF)

    hb = norm_mod(x_ref[0])
    hb_prev = jnp.where(i > 0, norm_mod(xp_ref[0]), jnp.zeros((HALO, D_MODEL), _BF))
    hb_next = jnp.where(i < pl.num_programs(1) - 1, norm_mod(xn_ref[0]),
                        jnp.zeros((HALO, D_MODEL), _BF))
    hb_ext = jnp.concatenate([hb_prev, hb, hb_next], axis=0)

    def plain(c0):
        cols = slice(c0, c0 + MXU_COLS)
        z_ref[0, :, cols] = jnp.dot(hb, w_ref[:, cols],
                                    preferred_element_type=_F32).astype(_BF)

    def conv_matmul(j, ze_s):
        for c0 in range(0, D_MODEL, MXU_COLS):
            ze_s[:, c0:c0 + MXU_COLS] = jnp.dot(
                hb_ext, w_ref[:, j * D_MODEL + c0:j * D_MODEL + c0 + MXU_COLS],
                preferred_element_type=_F32)

    def conv_silu(j, ze_s, r0):
        n = CONV_ROWS + 16
        mid = slice(8, 8 + CONV_ROWS)
        for c0 in range(0, D_MODEL, CONV_COLS):
            cols = slice(j * D_MODEL + c0, j * D_MODEL + c0 + CONV_COLS)
            zb = ze_s[pl.ds(HALO - 8 + r0, n), c0:c0 + CONV_COLS]
            y = cb_ref[:, cols] + pltpu.roll(zb, 1, axis=0)[mid] * cw_ref[0:1, cols]
            y = y + zb[mid] * cw_ref[1:2, cols]
            y = y + pltpu.roll(zb, n - 1, axis=0)[mid] * cw_ref[2:3, cols]
            y = y / (1.0 + jnp.exp2(y * -_LOG2E))
            if j == 0:
                y = y * (HEAD_DIM ** -0.5)
            z_ref[0, pl.ds(r0, CONV_ROWS), cols] = y.astype(_BF)

    def gates_pre():
        gc = jnp.dot(hb, wg_ref[...], preferred_element_type=_F32) + bc_ref[...]
        gt = lax.dot_general(wgt_ref[...], hb, _NT, preferred_element_type=_F32) + br_ref[...]
        return gc, gt

    def gates_col(gc):
        ls = _log_sigmoid(gc[:, :128])
        gi = gc[:, 128:]
        rr = lax.broadcasted_iota(jnp.int32, (L, L), 0)
        ss = lax.broadcasted_iota(jnp.int32, (L, L), 1)
        tri_lo = (ss <= rr).astype(_BF)
        tri_up = (ss >= rr).astype(_BF)
        lane = lax.broadcasted_iota(jnp.int32, (L, 128), 1)
        for c in range(tm // L):
            rows = slice(c * L, (c + 1) * L)
            cf = jnp.zeros((L, 128), _F32)
            cb = jnp.zeros((L, 128), _F32)
            for piece in _split3(ls[rows]):
                cf = cf + jnp.dot(tri_lo, piece, preferred_element_type=_F32)
                cb = cb + jnp.dot(tri_up, piece, preferred_element_type=_F32)
            cum = jnp.where(lane < 8, cf, cb)
            gcol_ref[0, rows, :] = (gi[rows] - cum)[:, :N_GATE]

    def gates_row(gt):
        nr = HEADS * GATE_SLOTS
        lst = _log_sigmoid(gt[:nr])
        git = gt[nr:]
        slot = lax.broadcasted_iota(jnp.int32, (nr, L), 0) & (GATE_SLOTS - 1)
        fwd = slot < 3
        for c in range(tm // L):
            cols = slice(c * L, (c + 1) * L)
            lsc = lst[:, cols]
            cum = jnp.where(fwd, _lane_scan(lsc, jnp.add, 0.0, False),
                            _lane_scan(lsc, jnp.add, 0.0, True))
            base = jnp.where((slot == 0) | (slot == 3), cum, git[:, cols] - cum)
            cmax = jnp.where(fwd, _lane_scan(base, jnp.maximum, -jnp.inf, False),
                             _lane_scan(base, jnp.maximum, -jnp.inf, True))
            grow_ref[0, :, cols] = jnp.where((slot == 2) | (slot == 5), cmax, base)

    plain_blocks = iter(range(2 * D_MODEL, Z_COLS, MXU_COLS))
    gc, gt = gates_pre()
    conv_matmul(0, zq_s)
    conv_matmul(1, zk_s)
    for j, ze_s in enumerate((zq_s, zk_s)):
        for r0 in range(0, tm, CONV_ROWS):
            conv_silu(j, ze_s, r0)
            plain(next(plain_blocks))
    gates_row(gt)
    gates_col(gc)
    for c0 in plain_blocks:
        plain(c0)
    vt_ref[0] = lax.dot_general(wvt_ref[...], hb, _NT,
                                preferred_element_type=_F32).astype(_BF)


def _in_proj_call(x, norm_g, sc, sh, w_bf, wvt_bf, wg_bf, wgt_bf, bias_col, bias_row,
                  conv_w, conv_b, tm):
    B, S, D = x.shape
    r = tm // HALO
    last = S // HALO - 1
    return pl.pallas_call(
        _in_proj_kernel,
        grid=(B, S // tm),
        in_specs=[pl.BlockSpec((1, tm, D), lambda b, i: (b, i, 0)),
                  pl.BlockSpec((1, HALO, D), lambda b, i: (b, jnp.maximum(i * r - 1, 0), 0)),
                  pl.BlockSpec((1, HALO, D),
                               lambda b, i: (b, jnp.minimum((i + 1) * r, last), 0)),
                  _const_spec((1, D)),
                  pl.BlockSpec((1, 1, D), lambda b, i: (b, 0, 0)),
                  pl.BlockSpec((1, 1, D), lambda b, i: (b, 0, 0)),
                  _const_spec((D, Z_COLS)),
                  _const_spec((D, D)),
                  _const_spec((D, 256)),
                  _const_spec((2 * HEADS * GATE_SLOTS, D)),
                  _const_spec((1, 256)),
                  _const_spec((2 * HEADS * GATE_SLOTS, 1)),
                  _const_spec((3, 2 * D)),
                  _const_spec((1, 2 * D))],
        out_specs=[pl.BlockSpec((1, tm, Z_COLS), lambda b, i: (b, i, 0)),
                   pl.BlockSpec((1, D, tm), lambda b, i: (b, 0, i)),
                   pl.BlockSpec((1, tm, N_GATE), lambda b, i: (b, i, 0)),
                   pl.BlockSpec((1, HEADS * GATE_SLOTS, tm), lambda b, i: (b, 0, i))],
        out_shape=[jax.ShapeDtypeStruct((B, S, Z_COLS), _BF),
                   jax.ShapeDtypeStruct((B, D, S), _BF),
                   jax.ShapeDtypeStruct((B, S, N_GATE), _F32),
                   jax.ShapeDtypeStruct((B, HEADS * GATE_SLOTS, S), _F32)],
        scratch_shapes=[pltpu.VMEM((tm + 2 * HALO, D), _F32),
                        pltpu.VMEM((tm + 2 * HALO, D), _F32)],
        compiler_params=pltpu.CompilerParams(
            dimension_semantics=("arbitrary", "arbitrary"),
            vmem_limit_bytes=VMEM_LIMIT),
        name="in_proj",
    )(x, x, x, norm_g, sc, sh, w_bf, wvt_bf, wg_bf, wgt_bf, bias_col, bias_row, conv_w, conv_b)


def _mlstm_kernel(q_ref, k_ref, vt_ref, zo_ref, gr_ref, gc_ref, mhg_ref,
                  y_ref,
                  acc_s, ctf_s, ctb_s, mf_s, mb_s):
    S = q_ref.shape[1]
    L = MLSTM_CHUNK
    NC = S // L
    dh = HEAD_DIM

    acc_s[...] = jnp.zeros_like(acc_s)
    ctf_s[...] = jnp.zeros_like(ctf_s)
    ctb_s[...] = jnp.zeros_like(ctb_s)
    mf_s[...] = jnp.zeros_like(mf_s)
    mb_s[...] = jnp.zeros_like(mb_s)

    si = lax.broadcasted_iota(jnp.int32, (L, L), 0)
    ti = lax.broadcasted_iota(jnp.int32, (L, L), 1)
    mask_f = si <= ti
    mask_b = si >= ti
    ones_row = (lax.broadcasted_iota(jnp.int32, (STATE_ROWS - dh, L), 0) == 0).astype(_BF)

    def chunk_step(t0, ct_s, m_s, mask, k0, j, last):
        qc = q_ref[0, pl.ds(t0, L), :]
        kc = k_ref[0, pl.ds(t0, L), :]
        vta = jnp.concatenate([vt_ref[0, :, pl.ds(t0, L)], ones_row], axis=0)
        gr = gr_ref[0, 0, :, pl.ds(t0, L)]
        b_row = gr[k0:k0 + 1]
        r_row = gr[k0 + 1:k0 + 2]
        cmax_row = gr[k0 + 2:k0 + 3]
        gcb = gc_ref[0, pl.ds(t0, L), :]
        pick = lax.broadcasted_iota(jnp.int32, gcb.shape, 1) == j * HEADS + pl.program_id(1)
        r_col = jnp.sum(jnp.where(pick, gcb, 0.0), axis=1, keepdims=True)
        m = m_s[...]

        a_row = jnp.maximum(m, cmax_row)
        s_t = lax.dot_general(kc, qc, _NT, preferred_element_type=_F32)
        p_t = jnp.exp(jnp.where(mask, r_col - a_row, -jnp.inf)) * s_t
        w_inter = jnp.exp(m - a_row)
        ct_old = ct_s[...]
        num_t = w_inter * lax.dot_general(ct_old.astype(_BF), qc, _NT,
                                          preferred_element_type=_F32)
        num_t = num_t + jnp.dot(vta, p_t.astype(_BF), preferred_element_type=_F32)
        den = num_t[dh:dh + 1, :]
        scale = 1.0 / jnp.maximum(jnp.abs(den), jnp.exp(-(b_row + a_row)))
        acc_s[:, pl.ds(t0, L)] += num_t[:dh, :] * scale

        a_last = a_row[:, last:last + 1]
        w_row = jnp.exp(r_row - a_last)
        decay = jnp.exp(m - a_last)
        vtw = (vta.astype(_F32) * w_row).astype(_BF)
        ct_s[...] = decay * ct_old + jnp.dot(vtw, kc, preferred_element_type=_F32)
        m_s[...] = b_row[:, last:last + 1] + a_last

    def phase_b(i, carry):
        tf = pl.multiple_of(i * L, L)
        tb = pl.multiple_of((NC - 1 - i) * L, L)
        chunk_step(tf, ctf_s, mf_s, mask_f, 0, 1, L - 1)
        chunk_step(tb, ctb_s, mb_s, mask_b, 3, 3, 0)
        return carry

    lax.fori_loop(0, NC, phase_b, 0, unroll=2)

    def phase_c(i, carry):
        t0 = pl.multiple_of(i * L, L)
        h = acc_s[:, pl.ds(t0, L)].T
        hn = h * lax.rsqrt(jnp.mean(h * h, axis=-1, keepdims=True) + EPS) * mhg_ref[...]
        o = zo_ref[0, pl.ds(t0, L), :].astype(_F32)
        y_ref[0, pl.ds(t0, L), :] = (hn * _sigmoid(o)).astype(_BF)
        return carry

    lax.fori_loop(0, NC, phase_c, 0)


def _mlstm_call(z, vt, grow, gcol, mh_g):
    B, S, _ = z.shape
    dh = HEAD_DIM
    H = HEADS

    def zspec(off):
        return pl.BlockSpec((1, S, dh), lambda b, h, off=off: (b, 0, off + h))

    return pl.pallas_call(
        _mlstm_kernel,
        grid=(B, H),
        in_specs=[zspec(0), zspec(H),
                  pl.BlockSpec((1, dh, S), lambda b, h: (b, h, 0)),
                  zspec(2 * H),
                  pl.BlockSpec((1, 1, 8, S), lambda b, h: (b, h, 0, 0)),
                  pl.BlockSpec((1, S, N_GATE), lambda b, h: (b, 0, 0)),
                  pl.BlockSpec((1, dh), lambda b, h: (0, h))],
        out_specs=pl.BlockSpec((1, S, dh), lambda b, h: (b, 0, h)),
        out_shape=jax.ShapeDtypeStruct((B, S, H * dh), _BF),
        scratch_shapes=[pltpu.VMEM((dh, S), _F32),
                        pltpu.VMEM((STATE_ROWS, dh), _F32), pltpu.VMEM((STATE_ROWS, dh), _F32),
                        pltpu.VMEM((1, 1), _F32), pltpu.VMEM((1, 1), _F32)],
        compiler_params=pltpu.CompilerParams(
            dimension_semantics=("arbitrary", "arbitrary"),
            vmem_limit_bytes=VMEM_LIMIT),
        name="mlstm",
    )(z, z, vt, z, grow, gcol, mh_g)


def _mix_kernel(x_ref, zu_ref, zv_ref, zga_ref, zgb_ref, ya_ref, g1_ref, lng_ref, lnb_ref,
                ws_ref, bs_ref, wo_ref, o_ref, mix_s):
    tm = x_ref.shape[1]
    P = SGU_CHUNK
    C = SGU_GROUP_DIM
    for j in range(tm // P):
        rows = slice(j * P, (j + 1) * P)
        gv = _gelu(zv_ref[0, rows, :].astype(_F32))
        mu = jnp.mean(gv, axis=-1, keepdims=True)
        dv = gv - mu
        var = jnp.mean(dv * dv, axis=-1, keepdims=True)
        vn = (dv * lax.rsqrt(var + EPS) * lng_ref[...] + lnb_ref[...]).astype(_BF)
        s = jnp.concatenate(
            [jnp.dot(ws_ref[g], vn[:, g * C:(g + 1) * C], preferred_element_type=_F32)
             for g in range(SGU_GROUPS)], axis=1) + bs_ref[...]
        yb = _gelu(zu_ref[0, rows, :].astype(_F32)) * s
        ga = _sigmoid(zga_ref[0, rows, :].astype(_F32))
        gb = _sigmoid(zgb_ref[0, rows, :].astype(_F32))
        mix_s[rows, :] = (ga * ya_ref[0, rows, :].astype(_F32) + gb * yb).astype(_BF)
    upd = jnp.dot(mix_s[...], wo_ref[...], preferred_element_type=_F32)
    o_ref[0] = x_ref[0] + g1_ref[0] * upd


def _mix_call(x, z, ya, g1, ln_g, ln_b, ws_bf, bs_full, wo_bf, tm):
    B, S, D = x.shape

    def zspec(j):
        return pl.BlockSpec((1, tm, D), lambda b, i, j=j: (b, i, j))

    return pl.pallas_call(
        _mix_kernel,
        grid=(B, S // tm),
        in_specs=[zspec(0), zspec(3), zspec(4), zspec(5), zspec(6), zspec(0),
                  pl.BlockSpec((1, 1, D), lambda b, i: (b, 0, 0)),
                  _const_spec((1, D)), _const_spec((1, D)),
                  _const_spec((SGU_GROUPS, SGU_CHUNK, SGU_CHUNK)),
                  _const_spec((SGU_CHUNK, D)),
                  _const_spec((D, D))],
        out_specs=pl.BlockSpec((1, tm, D), lambda b, i: (b, i, 0)),
        out_shape=jax.ShapeDtypeStruct((B, S, D), _F32),
        scratch_shapes=[pltpu.VMEM((tm, D), _BF)],
        compiler_params=pltpu.CompilerParams(
            dimension_semantics=("arbitrary", "arbitrary"),
            vmem_limit_bytes=VMEM_LIMIT),
        name="mix",
    )(x, z, z, z, z, ya, g1, ln_g, ln_b, ws_bf, bs_full, wo_bf)


def _ffn_kernel(x_ref, g_ref, sc_ref, sh_ref, g2_ref, w1_ref, w2_ref, nf_ref, o_ref):
    x = x_ref[0]
    ms = jnp.mean(x * x, axis=-1, keepdims=True)
    h = x * lax.rsqrt(ms + EPS) * g_ref[...]
    hb = (h * (1.0 + sc_ref[0]) + sh_ref[0]).astype(_BF)
    ff = jnp.zeros(x.shape, _F32)
    for j in range(FF_DIM // D_MODEL):
        cols = slice(j * D_MODEL, (j + 1) * D_MODEL)
        a = jnp.maximum(jnp.dot(hb, w1_ref[:, cols], preferred_element_type=_F32), 0.0)
        ff = ff + jnp.dot((a * a).astype(_BF), w2_ref[cols, :], preferred_element_type=_F32)
    x2 = x + g2_ref[0] * ff
    ms2 = jnp.mean(x2 * x2, axis=-1, keepdims=True)
    o_ref[0] = x2 * lax.rsqrt(ms2 + EPS) * nf_ref[...]


def _ffn_call(x1, norm_g, sc, sh, g2, w1_bf, w2_bf, normf_g, tm):
    B, S, D = x1.shape
    vec = pl.BlockSpec((1, 1, D), lambda b, i: (b, 0, 0))
    return pl.pallas_call(
        _ffn_kernel,
        grid=(B, S // tm),
        in_specs=[pl.BlockSpec((1, tm, D), lambda b, i: (b, i, 0)),
                  _const_spec((1, D)), vec, vec, vec,
                  _const_spec((D, FF_DIM)), _const_spec((FF_DIM, D)),
                  _const_spec((1, D))],
        out_specs=pl.BlockSpec((1, tm, D), lambda b, i: (b, i, 0)),
        out_shape=jax.ShapeDtypeStruct((B, S, D), _F32),
        compiler_params=pltpu.CompilerParams(
            dimension_semantics=("arbitrary", "arbitrary"),
            vmem_limit_bytes=VMEM_LIMIT),
        name="ffn",
    )(x1, norm_g, sc, sh, g2, w1_bf, w2_bf, normf_g)


def _gate_weights(w_if, b_if):
    D = w_if.shape[0]
    H = HEADS
    i_f, f_f, i_b, f_b = (w_if[:, k * H:(k + 1) * H] for k in range(4))
    bi_f, bf_f, bi_b, bf_b = (b_if[k] for k in range(4))
    zw = jnp.zeros((D, H), w_if.dtype)
    zb = jnp.zeros((H,), b_if.dtype)
    w_f = jnp.concatenate([f_f, f_f, f_b, f_b], axis=1)
    w_i = jnp.concatenate([zw, i_f, zw, i_b], axis=1)
    b_f = jnp.concatenate([bf_f, bf_f, bf_b, bf_b])
    b_i = jnp.concatenate([zb, bi_f, zb, bi_b])
    padw = jnp.zeros((D, 128 - 4 * H), w_if.dtype)
    padb = jnp.zeros((128 - 4 * H,), b_if.dtype)
    w_col = jnp.concatenate([w_f, padw, w_i, padw], axis=1)
    b_col = jnp.concatenate([b_f, padb, b_i, padb]).reshape(1, 256)
    zc = jnp.zeros((D,), w_if.dtype)
    z0 = jnp.zeros((), b_if.dtype)
    rows_f, rows_i, bias_f, bias_i = [], [], [], []
    for h in range(H):
        rows_f += [f_f[:, h]] * 3 + [f_b[:, h]] * 3 + [zc] * 2
        rows_i += [zc, i_f[:, h], i_f[:, h], zc, i_b[:, h], i_b[:, h], zc, zc]
        bias_f += [bf_f[h]] * 3 + [bf_b[h]] * 3 + [z0] * 2
        bias_i += [z0, bi_f[h], bi_f[h], z0, bi_b[h], bi_b[h], z0, z0]
    w_row = jnp.stack(rows_f + rows_i)
    b_row = jnp.stack(bias_f + bias_i).reshape(2 * H * GATE_SLOTS, 1)
    return w_col, b_col, w_row, b_row


def kernel(x, c, w_ada, b_ada, norm1_g, norm2_g, w_in, b_if, conv_w, conv_b, mh_g,
           ln_v_g, ln_v_b, w_s, b_s, w_out, w1, w2, normf_g):
    B, S, D = x.shape
    H = HEADS
    assert w_ada.shape[0] == 1, "single layer"

    mod = _mod_call(c, w_ada[0], b_ada[0])
    sh1, sc1, g1, sh2, sc2, g2 = (mod[:, k * D:(k + 1) * D].reshape(B, 1, D) for k in range(6))

    w_in0 = w_in[0]
    w_z = jnp.concatenate([w_in0[:, :2 * D], w_in0[:, 3 * D:8 * D]], axis=1).astype(_BF)
    w_vt = w_in0[:, 2 * D:3 * D].T.astype(_BF)
    w_col, b_col, w_row, b_row = _gate_weights(w_in0[:, 8 * D:], b_if[0])
    z, vt, gcol, grow = _in_proj_call(x, norm1_g, sc1, sh1, w_z, w_vt, w_col.astype(_BF),
                                      w_row.astype(_BF), b_col, b_row, conv_w[0], conv_b, tm=512)

    ya = _mlstm_call(z, vt, grow.reshape(B, H, GATE_SLOTS, S), gcol, mh_g)

    bs_full = jnp.repeat(b_s[0].T, SGU_GROUP_DIM, axis=1)
    x1 = _mix_call(x, z, ya, g1, ln_v_g, ln_v_b, w_s[0].astype(_BF), bs_full,
                   w_out[0].astype(_BF), tm=512)

    return _ffn_call(x1, norm2_g, sc2, sh2, g2, w1[0].astype(_BF), w2[0].astype(_BF),
                     normf_g.reshape(1, D), tm=512)
```

```python
import jax
import jax.numpy as jnp
from jax import lax
from jax.experimental import pallas as pl
from jax.experimental.pallas import tpu as pltpu

D_MODEL = 1024
HEADS = 4
HEAD_DIM = 256
N_GATE = 16
SGU_GROUPS = 8
SGU_GROUP_DIM = 128
SGU_CHUNK = 128
FF_DIM = 4096
EPS = 1e-6
Z_COLS = 7 * D_MODEL

MLSTM_CHUNK = 256
STATE_ROWS = HEAD_DIM + 16
HALO = 16
GATE_SLOTS = 8
CONV_ROWS = 64
CONV_COLS = 128
MXU_COLS = 256

VMEM_LIMIT = 56 * 1024 * 1024

_BF = jnp.bfloat16
_F32 = jnp.float32
_NT = (((1,), (1,)), ((), ()))
_LOG2E = 1.4426950408889634


def _const_spec(shape):
    nd = len(shape)
    return pl.BlockSpec(shape, lambda *_: (0,) * nd, pipeline_mode=pl.Buffered(1))


def _sigmoid(x):
    return 1.0 / (1.0 + jnp.exp(-x))


def _log_sigmoid(x):
    return jnp.minimum(x, 0.0) - jnp.log(1.0 + jnp.exp(-jnp.abs(x)))


def _gelu(x):
    return 0.5 * x * (1.0 + lax.erf(x * (2.0 ** -0.5)))


def _split3(x):
    x1 = x.astype(_BF)
    r1 = x - x1.astype(_F32)
    x2 = r1.astype(_BF)
    r2 = r1 - x2.astype(_F32)
    return x1, x2, r2.astype(_BF)


def _mod_kernel(c_ref, w_ref, b_ref, o_ref):
    c = c_ref[...]
    ca = c * _sigmoid(c)
    o_ref[...] = jnp.dot(ca, w_ref[...], precision=lax.Precision.HIGHEST,
                         preferred_element_type=_F32) + b_ref[...]


def _mod_call(c, w_ada, b_ada):
    B, D = c.shape
    N = w_ada.shape[1]
    tn = 1536
    return pl.pallas_call(
        _mod_kernel,
        grid=(N // tn,),
        in_specs=[pl.BlockSpec((B, D), lambda j: (0, 0)),
                  pl.BlockSpec((D, tn), lambda j: (0, j)),
                  pl.BlockSpec((1, tn), lambda j: (0, j))],
        out_specs=pl.BlockSpec((B, tn), lambda j: (0, j)),
        out_shape=jax.ShapeDtypeStruct((B, N), _F32),
        name="mod",
    )(c, w_ada, b_ada.reshape(1, N))


def _lane_scan(x, op, ident, reverse):
    L = x.shape[1]
    lane = lax.broadcasted_iota(jnp.int32, x.shape, 1)
    d = 1
    while d < L:
        if reverse:
            shifted = jnp.where(lane < L - d, pltpu.roll(x, L - d, axis=1), ident)
        else:
            shifted = jnp.where(lane >= d, pltpu.roll(x, d, axis=1), ident)
        x = op(x, shifted)
        d *= 2
    return x


def _in_proj_kernel(x_ref, xp_ref, xn_ref, g_ref, sc_ref, sh_ref, w_ref, wvt_ref, wg_ref,
                    wgt_ref, bc_ref, br_ref, cw_ref, cb_ref, z_ref, vt_ref, gcol_ref, grow_ref,
                    zq_s, zk_s):
    tm = x_ref.shape[1]
    L = MLSTM_CHUNK
    i = pl.program_id(1)

    def norm_mod(x):
        ms = jnp.mean(x * x, axis=-1, keepdims=True)
        h = x * lax.rsqrt(ms + EPS) * g_ref[...]
        return (h * (1.0 + sc_ref[0]) + sh_ref[0]).astype(_BF)

    hb = norm_mod(x_ref[0])
    hb_prev = jnp.where(i > 0, norm_mod(xp_ref[0]), jnp.zeros((HALO, D_MODEL), _BF))
    hb_next = jnp.where(i < pl.num_programs(1) - 1, norm_mod(xn_ref[0]),
                        jnp.zeros((HALO, D_MODEL), _BF))
    hb_ext = jnp.concatenate([hb_prev, hb, hb_next], axis=0)

    def plain(c0):
        cols = slice(c0, c0 + MXU_COLS)
        z_ref[0, :, cols] = jnp.dot(hb, w_ref[:, cols],
                                    preferred_element_type=_F32).astype(_BF)

    def conv_matmul(j, ze_s):
        for c0 in range(0, D_MODEL, MXU_COLS):
            ze_s[:, c0:c0 + MXU_COLS] = jnp.dot(
                hb_ext, w_ref[:, j * D_MODEL + c0:j * D_MODEL + c0 + MXU_COLS],
                preferred_element_type=_F32)

    def conv_silu(j, ze_s, r0):
        n = CONV_ROWS + 16
        mid = slice(8, 8 + CONV_ROWS)
        for c0 in range(0, D_MODEL, CONV_COLS):
            cols = slice(j * D_MODEL + c0, j * D_MODEL + c0 + CONV_COLS)
            zb = ze_s[pl.ds(HALO - 8 + r0, n), c0:c0 + CONV_COLS]
            y = cb_ref[:, cols] + pltpu.roll(zb, 1, axis=0)[mid] * cw_ref[0:1, cols]
            y = y + zb[mid] * cw_ref[1:2, cols]
            y = y + pltpu.roll(zb, n - 1, axis=0)[mid] * cw_ref[2:3, cols]
            y = y / (1.0 + jnp.exp2(y * -_LOG2E))
            if j == 0:
                y = y * (HEAD_DIM ** -0.5)
            z_ref[0, pl.ds(r0, CONV_ROWS), cols] = y.astype(_BF)

    def gates_pre():
        gc = jnp.dot(hb, wg_ref[...], preferred_element_type=_F32) + bc_ref[...]
        gt = lax.dot_general(wgt_ref[...], hb, _NT, preferred_element_type=_F32) + br_ref[...]
        return gc, gt

    def gates_col(gc):
        ls = _log_sigmoid(gc[:, :128])
        gi = gc[:, 128:]
        rr = lax.broadcasted_iota(jnp.int32, (L, L), 0)
        ss = lax.broadcasted_iota(jnp.int32, (L, L), 1)
        tri_lo = (ss <= rr).astype(_BF)
        tri_up = (ss >= rr).astype(_BF)
        lane = lax.broadcasted_iota(jnp.int32, (L, 128), 1)
        for c in range(tm // L):
            rows = slice(c * L, (c + 1) * L)
            cf = jnp.zeros((L, 128), _F32)
            cb = jnp.zeros((L, 128), _F32)
            for piece in _split3(ls[rows]):
                cf = cf + jnp.dot(tri_lo, piece, preferred_element_type=_F32)
                cb = cb + jnp.dot(tri_up, piece, preferred_element_type=_F32)
            cum = jnp.where(lane < 8, cf, cb)
            gcol_ref[0, rows, :] = (gi[rows] - cum)[:, :N_GATE]

    def gates_row(gt):
        nr = HEADS * GATE_SLOTS
        lst = _log_sigmoid(gt[:nr])
        git = gt[nr:]
        slot = lax.broadcasted_iota(jnp.int32, (nr, L), 0) & (GATE_SLOTS - 1)
        fwd = slot < 3
        for c in range(tm // L):
            cols = slice(c * L, (c + 1) * L)
            lsc = lst[:, cols]
            cum = jnp.where(fwd, _lane_scan(lsc, jnp.add, 0.0, False),
                            _lane_scan(lsc, jnp.add, 0.0, True))
            base = jnp.where((slot == 0) | (slot == 3), cum, git[:, cols] - cum)
            cmax = jnp.where(fwd, _lane_scan(base, jnp.maximum, -jnp.inf, False),
                             _lane_scan(base, jnp.maximum, -jnp.inf, True))
            grow_ref[0, :, cols] = jnp.where((slot == 2) | (slot == 5), cmax, base)

    plain_blocks = iter(range(2 * D_MODEL, Z_COLS, MXU_COLS))
    gc, gt = gates_pre()
    conv_matmul(0, zq_s)
    conv_matmul(1, zk_s)
    for j, ze_s in enumerate((zq_s, zk_s)):
        for r0 in range(0, tm, CONV_ROWS):
            conv_silu(j, ze_s, r0)
            plain(next(plain_blocks))
    gates_row(gt)
    gates_col(gc)
    for c0 in plain_blocks:
        plain(c0)
    vt_ref[0] = lax.dot_general(wvt_ref[...], hb, _NT,
                                preferred_element_type=_F32).astype(_BF)


def _in_proj_call(x, norm_g, sc, sh, w_bf, wvt_bf, wg_bf, wgt_bf, bias_col, bias_row,
                  conv_w, conv_b, tm):
    B, S, D = x.shape
    r = tm // HALO
    last = S // HALO - 1
    return pl.pallas_call(
        _in_proj_kernel,
        grid=(B, S // tm),
        in_specs=[pl.BlockSpec((1, tm, D), lambda b, i: (b, i, 0)),
                  pl.BlockSpec((1, HALO, D), lambda b, i: (b, jnp.maximum(i * r - 1, 0), 0)),
                  pl.BlockSpec((1, HALO, D),
                               lambda b, i: (b, jnp.minimum((i + 1) * r, last), 0)),
                  _const_spec((1, D)),
                  pl.BlockSpec((1, 1, D), lambda b, i: (b, 0, 0)),
                  pl.BlockSpec((1, 1, D), lambda b, i: (b, 0, 0)),
                  _const_spec((D, Z_COLS)),
                  _const_spec((D, D)),
                  _const_spec((D, 256)),
                  _const_spec((2 * HEADS * GATE_SLOTS, D)),
                  _const_spec((1, 256)),
                  _const_spec((2 * HEADS * GATE_SLOTS, 1)),
                  _const_spec((3, 2 * D)),
                  _const_spec((1, 2 * D))],
        out_specs=[pl.BlockSpec((1, tm, Z_COLS), lambda b, i: (b, i, 0)),
                   pl.BlockSpec((1, D, tm), lambda b, i: (b, 0, i)),
                   pl.BlockSpec((1, tm, N_GATE), lambda b, i: (b, i, 0)),
                   pl.BlockSpec((1, HEADS * GATE_SLOTS, tm), lambda b, i: (b, 0, i))],
        out_shape=[jax.ShapeDtypeStruct((B, S, Z_COLS), _BF),
                   jax.ShapeDtypeStruct((B, D, S), _BF),
                   jax.ShapeDtypeStruct((B, S, N_GATE), _F32),
                   jax.ShapeDtypeStruct((B, HEADS * GATE_SLOTS, S), _F32)],
        scratch_shapes=[pltpu.VMEM((tm + 2 * HALO, D), _F32),
                        pltpu.VMEM((tm + 2 * HALO, D), _F32)],
        compiler_params=pltpu.CompilerParams(
            dimension_semantics=("arbitrary", "arbitrary"),
            vmem_limit_bytes=VMEM_LIMIT),
        name="in_proj",
    )(x, x, x, norm_g, sc, sh, w_bf, wvt_bf, wg_bf, wgt_bf, bias_col, bias_row, conv_w, conv_b)


def _mlstm_kernel(q_ref, k_ref, vt_ref, zo_ref, gr_ref, gc_ref, mhg_ref,
                  y_ref,
                  pvf_s, pvb_s, u_s, snap_s, ct_s, mh_s, dec_s):
    S = q_ref.shape[1]
    L = MLSTM_CHUNK
    NC = S // L
    dh = HEAD_DIM
    R = STATE_ROWS
    K0 = (0, 3)
    LAST = (L - 1, 0)

    def gate_rows(gr, d):
        k0 = K0[d]
        return gr[k0:k0 + 1], gr[k0 + 1:k0 + 2], gr[k0 + 2:k0 + 3]

    m = [jnp.zeros((1, 1), _F32), jnp.zeros((1, 1), _F32)]
    for i in range(NC):
        for d, c in enumerate((i, NC - 1 - i)):
            b_row, _, cmax_row = gate_rows(gr_ref[0, 0, :, c * L:(c + 1) * L], d)
            a_last = jnp.maximum(m[d], cmax_row[:, LAST[d]:LAST[d] + 1])
            mh_s[d * NC + c:d * NC + c + 1, :] = jnp.broadcast_to(m[d], (1, 128))
            dec_s[d * NC + c:d * NC + c + 1, :] = jnp.broadcast_to(jnp.exp(m[d] - a_last),
                                                                  (1, 128))
            m[d] = b_row[:, LAST[d]:LAST[d] + 1] + a_last

    si = lax.broadcasted_iota(jnp.int32, (L, L), 0)
    ti = lax.broadcasted_iota(jnp.int32, (L, L), 1)
    masks = (si <= ti, si >= ti)
    ones_row = (lax.broadcasted_iota(jnp.int32, (R - dh, L), 0) == 0).astype(_BF)

    def pass0(c, carry):
        t0 = pl.multiple_of(c * L, L)
        qc = q_ref[0, pl.ds(t0, L), :]
        kc = k_ref[0, pl.ds(t0, L), :]
        vta = jnp.concatenate([vt_ref[0, :, pl.ds(t0, L)], ones_row], axis=0)
        vta32 = vta.astype(_F32)
        gr = gr_ref[0, 0, :, pl.ds(t0, L)]
        gcb = gc_ref[0, pl.ds(t0, L), :]
        lane = lax.broadcasted_iota(jnp.int32, gcb.shape, 1)
        s_t = lax.dot_general(kc, qc, _NT, preferred_element_type=_F32)
        p, vtw = [], []
        for d in range(2):
            _, r_row, cmax_row = gate_rows(gr, d)
            pick = lane == (2 * d + 1) * HEADS + pl.program_id(1)
            r_col = jnp.sum(jnp.where(pick, gcb, 0.0), axis=1, keepdims=True)
            a_row = jnp.maximum(mh_s[pl.ds(d * NC + c, 1), 0:1], cmax_row)
            p.append((jnp.exp(jnp.where(masks[d], r_col - a_row, -jnp.inf)) * s_t).astype(_BF))
            w_row = jnp.exp(r_row - a_row[:, LAST[d]:LAST[d] + 1])
            vtw.append((vta32 * w_row).astype(_BF))
        pv = jnp.dot(vta, jnp.concatenate(p, axis=1), preferred_element_type=_F32)
        pvf_s[:, pl.ds(t0, L)] = pv[:, :L]
        pvb_s[:, pl.ds(t0, L)] = pv[:, L:]
        u_s[c] = jnp.dot(jnp.concatenate(vtw, axis=0), kc, preferred_element_type=_F32)
        return carry

    lax.fori_loop(0, NC, pass0, 0, unroll=8)

    ct_s[...] = jnp.zeros_like(ct_s)

    def pass1(i, carry):
        for d, c in enumerate((i, NC - 1 - i)):
            rows = slice(d * R, (d + 1) * R)
            ct = ct_s[rows, :]
            snap_s[c, rows, :] = ct.astype(_BF)
            ct_s[rows, :] = dec_s[pl.ds(d * NC + c, 1), 0:1] * ct + u_s[c, rows, :]
        return carry

    lax.fori_loop(0, NC, pass1, 0, unroll=2)

    def pass2(c, carry):
        t0 = pl.multiple_of(c * L, L)
        qc = q_ref[0, pl.ds(t0, L), :]
        gr = gr_ref[0, 0, :, pl.ds(t0, L)]
        inter = lax.dot_general(snap_s[c], qc, _NT, preferred_element_type=_F32)
        h_t = None
        for d, pv_s in enumerate((pvf_s, pvb_s)):
            b_row, _, cmax_row = gate_rows(gr, d)
            m_c = mh_s[pl.ds(d * NC + c, 1), 0:1]
            a_row = jnp.maximum(m_c, cmax_row)
            num_t = jnp.exp(m_c - a_row) * inter[d * R:(d + 1) * R] + pv_s[:, pl.ds(t0, L)]
            den = num_t[dh:dh + 1, :]
            scale = 1.0 / jnp.maximum(jnp.abs(den), jnp.exp(-(b_row + a_row)))
            h_d = num_t[:dh, :] * scale
            h_t = h_d if h_t is None else h_t + h_d
        h = h_t.T
        hn = h * lax.rsqrt(jnp.mean(h * h, axis=-1, keepdims=True) + EPS) * mhg_ref[...]
        o = zo_ref[0, pl.ds(t0, L), :].astype(_F32)
        y_ref[0, pl.ds(t0, L), :] = (hn * _sigmoid(o)).astype(_BF)
        return carry

    lax.fori_loop(0, NC, pass2, 0, unroll=4)


def _mlstm_call(z, vt, grow, gcol, mh_g):
    B, S, _ = z.shape
    dh = HEAD_DIM
    H = HEADS
    nc = S // MLSTM_CHUNK

    def zspec(off):
        return pl.BlockSpec((1, S, dh), lambda b, h, off=off: (b, 0, off + h))

    return pl.pallas_call(
        _mlstm_kernel,
        grid=(B, H),
        in_specs=[zspec(0), zspec(H),
                  pl.BlockSpec((1, dh, S), lambda b, h: (b, h, 0)),
                  zspec(2 * H),
                  pl.BlockSpec((1, 1, 8, S), lambda b, h: (b, h, 0, 0)),
                  pl.BlockSpec((1, S, N_GATE), lambda b, h: (b, 0, 0)),
                  pl.BlockSpec((1, dh), lambda b, h: (0, h))],
        out_specs=pl.BlockSpec((1, S, dh), lambda b, h: (b, 0, h)),
        out_shape=jax.ShapeDtypeStruct((B, S, H * dh), _BF),
        scratch_shapes=[pltpu.VMEM((STATE_ROWS, S), _F32), pltpu.VMEM((STATE_ROWS, S), _F32),
                        pltpu.VMEM((nc, 2 * STATE_ROWS, dh), _F32),
                        pltpu.VMEM((nc, 2 * STATE_ROWS, dh), _BF),
                        pltpu.VMEM((2 * STATE_ROWS, dh), _F32),
                        pltpu.VMEM((2 * nc, 128), _F32), pltpu.VMEM((2 * nc, 128), _F32)],
        compiler_params=pltpu.CompilerParams(
            dimension_semantics=("arbitrary", "arbitrary"),
            vmem_limit_bytes=VMEM_LIMIT),
        name="mlstm",
    )(z, z, vt, z, grow, gcol, mh_g)


def _mix_kernel(x_ref, zu_ref, zv_ref, zga_ref, zgb_ref, ya_ref, g1_ref, lng_ref, lnb_ref,
                ws_ref, bs_ref, wo_ref, o_ref, mix_s):
    tm = x_ref.shape[1]
    P = SGU_CHUNK
    C = SGU_GROUP_DIM
    for j in range(tm // P):
        rows = slice(j * P, (j + 1) * P)
        gv = _gelu(zv_ref[0, rows, :].astype(_F32))
        mu = jnp.mean(gv, axis=-1, keepdims=True)
        dv = gv - mu
        var = jnp.mean(dv * dv, axis=-1, keepdims=True)
        vn = (dv * lax.rsqrt(var + EPS) * lng_ref[...] + lnb_ref[...]).astype(_BF)
        s = jnp.concatenate(
            [jnp.dot(ws_ref[g], vn[:, g * C:(g + 1) * C], preferred_element_type=_F32)
             for g in range(SGU_GROUPS)], axis=1) + bs_ref[...]
        yb = _gelu(zu_ref[0, rows, :].astype(_F32)) * s
        ga = _sigmoid(zga_ref[0, rows, :].astype(_F32))
        gb = _sigmoid(zgb_ref[0, rows, :].astype(_F32))
        mix_s[rows, :] = (ga * ya_ref[0, rows, :].astype(_F32) + gb * yb).astype(_BF)
    upd = jnp.dot(mix_s[...], wo_ref[...], preferred_element_type=_F32)
    o_ref[0] = x_ref[0] + g1_ref[0] * upd


def _mix_call(x, z, ya, g1, ln_g, ln_b, ws_bf, bs_full, wo_bf, tm):
    B, S, D = x.shape

    def zspec(j):
        return pl.BlockSpec((1, tm, D), lambda b, i, j=j: (b, i, j))

    return pl.pallas_call(
        _mix_kernel,
        grid=(B, S // tm),
        in_specs=[zspec(0), zspec(3), zspec(4), zspec(5), zspec(6), zspec(0),
                  pl.BlockSpec((1, 1, D), lambda b, i: (b, 0, 0)),
                  _const_spec((1, D)), _const_spec((1, D)),
                  _const_spec((SGU_GROUPS, SGU_CHUNK, SGU_CHUNK)),
                  _const_spec((SGU_CHUNK, D)),
                  _const_spec((D, D))],
        out_specs=pl.BlockSpec((1, tm, D), lambda b, i: (b, i, 0)),
        out_shape=jax.ShapeDtypeStruct((B, S, D), _F32),
        scratch_shapes=[pltpu.VMEM((tm, D), _BF)],
        compiler_params=pltpu.CompilerParams(
            dimension_semantics=("arbitrary", "arbitrary"),
            vmem_limit_bytes=VMEM_LIMIT),
        name="mix",
    )(x, z, z, z, z, ya, g1, ln_g, ln_b, ws_bf, bs_full, wo_bf)


def _ffn_kernel(x_ref, g_ref, sc_ref, sh_ref, g2_ref, w1_ref, w2_ref, nf_ref, o_ref):
    x = x_ref[0]
    ms = jnp.mean(x * x, axis=-1, keepdims=True)
    h = x * lax.rsqrt(ms + EPS) * g_ref[...]
    hb = (h * (1.0 + sc_ref[0]) + sh_ref[0]).astype(_BF)
    ff = jnp.zeros(x.shape, _F32)
    for j in range(FF_DIM // D_MODEL):
        cols = slice(j * D_MODEL, (j + 1) * D_MODEL)
        a = jnp.maximum(jnp.dot(hb, w1_ref[:, cols], preferred_element_type=_F32), 0.0)
        ff = ff + jnp.dot((a * a).astype(_BF), w2_ref[cols, :], preferred_element_type=_F32)
    x2 = x + g2_ref[0] * ff
    ms2 = jnp.mean(x2 * x2, axis=-1, keepdims=True)
    o_ref[0] = x2 * lax.rsqrt(ms2 + EPS) * nf_ref[...]


def _ffn_call(x1, norm_g, sc, sh, g2, w1_bf, w2_bf, normf_g, tm):
    B, S, D = x1.shape
    vec = pl.BlockSpec((1, 1, D), lambda b, i: (b, 0, 0))
    return pl.pallas_call(
        _ffn_kernel,
        grid=(B, S // tm),
        in_specs=[pl.BlockSpec((1, tm, D), lambda b, i: (b, i, 0)),
                  _const_spec((1, D)), vec, vec, vec,
                  _const_spec((D, FF_DIM)), _const_spec((FF_DIM, D)),
                  _const_spec((1, D))],
        out_specs=pl.BlockSpec((1, tm, D), lambda b, i: (b, i, 0)),
        out_shape=jax.ShapeDtypeStruct((B, S, D), _F32),
        compiler_params=pltpu.CompilerParams(
            dimension_semantics=("arbitrary", "arbitrary"),
            vmem_limit_bytes=VMEM_LIMIT),
        name="ffn",
    )(x1, norm_g, sc, sh, g2, w1_bf, w2_bf, normf_g)


def _gate_weights(w_if, b_if):
    D = w_if.shape[0]
    H = HEADS
    i_f, f_f, i_b, f_b = (w_if[:, k * H:(k + 1) * H] for k in range(4))
    bi_f, bf_f, bi_b, bf_b = (b_if[k] for k in range(4))
    zw = jnp.zeros((D, H), w_if.dtype)
    zb = jnp.zeros((H,), b_if.dtype)
    w_f = jnp.concatenate([f_f, f_f, f_b, f_b], axis=1)
    w_i = jnp.concatenate([zw, i_f, zw, i_b], axis=1)
    b_f = jnp.concatenate([bf_f, bf_f, bf_b, bf_b])
    b_i = jnp.concatenate([zb, bi_f, zb, bi_b])
    padw = jnp.zeros((D, 128 - 4 * H), w_if.dtype)
    padb = jnp.zeros((128 - 4 * H,), b_if.dtype)
    w_col = jnp.concatenate([w_f, padw, w_i, padw], axis=1)
    b_col = jnp.concatenate([b_f, padb, b_i, padb]).reshape(1, 256)
    zero = 4 * H
    idx_f, idx_i = [], []
    for h in range(H):
        idx_f += [H + h] * 3 + [3 * H + h] * 3 + [zero] * 2
        idx_i += [zero, h, h, zero, 2 * H + h, 2 * H + h, zero, zero]
    idx = jnp.array(idx_f + idx_i, jnp.int32)
    w_ext = jnp.concatenate([w_if.T, jnp.zeros((1, D), w_if.dtype)], axis=0)
    b_ext = jnp.concatenate([b_if.reshape(-1), jnp.zeros((1,), b_if.dtype)])
    w_row = jnp.take(w_ext, idx, axis=0)
    b_row = jnp.take(b_ext, idx).reshape(2 * H * GATE_SLOTS, 1)
    return w_col, b_col, w_row, b_row


def kernel(x, c, w_ada, b_ada, norm1_g, norm2_g, w_in, b_if, conv_w, conv_b, mh_g,
           ln_v_g, ln_v_b, w_s, b_s, w_out, w1, w2, normf_g):
    B, S, D = x.shape
    H = HEADS
    assert w_ada.shape[0] == 1, "single layer"

    mod = _mod_call(c, w_ada[0], b_ada[0])
    sh1, sc1, g1, sh2, sc2, g2 = (mod[:, k * D:(k + 1) * D].reshape(B, 1, D) for k in range(6))

    w_in0 = w_in[0]
    w_z = jnp.concatenate([w_in0[:, :2 * D], w_in0[:, 3 * D:8 * D]], axis=1).astype(_BF)
    w_vt = w_in0[:, 2 * D:3 * D].T.astype(_BF)
    w_col, b_col, w_row, b_row = _gate_weights(w_in0[:, 8 * D:], b_if[0])
    z, vt, gcol, grow = _in_proj_call(x, norm1_g, sc1, sh1, w_z, w_vt, w_col.astype(_BF),
                                      w_row.astype(_BF), b_col, b_row, conv_w[0], conv_b, tm=512)

    ya = _mlstm_call(z, vt, grow.reshape(B, H, GATE_SLOTS, S), gcol, mh_g)

    bs_full = jnp.repeat(b_s[0].T, SGU_GROUP_DIM, axis=1)
    x1 = _mix_call(x, z, ya, g1, ln_v_g, ln_v_b, w_s[0].astype(_BF), bs_full,
                   w_out[0].astype(_BF), tm=512)

    return _ffn_call(x1, norm2_g, sc2, sh2, g2, w1[0].astype(_BF), w2[0].astype(_BF),
                     normf_g.reshape(1, D), tm=512)
```

```python
import jax
import jax.numpy as jnp
from jax import lax
from jax.experimental import pallas as pl
from jax.experimental.pallas import tpu as pltpu

D_MODEL = 1024
HEADS = 4
HEAD_DIM = 256
SGU_GROUPS = 8
SGU_GROUP_DIM = 128
SGU_CHUNK = 128
FF_DIM = 4096
EPS = 1e-6
Z_COLS = 7 * D_MODEL

MLSTM_CHUNK = 256
STATE_ROWS = HEAD_DIM + 16
HALO = 16
GATE_SLOTS = 8
CONV_ROWS = 64
CONV_COLS = 128
MXU_COLS = 256

VMEM_LIMIT = 56 * 1024 * 1024

_BF = jnp.bfloat16
_F32 = jnp.float32
_NT = (((1,), (1,)), ((), ()))
_LOG2E = 1.4426950408889634


def _const_spec(shape):
    nd = len(shape)
    return pl.BlockSpec(shape, lambda *_: (0,) * nd, pipeline_mode=pl.Buffered(1))


def _sigmoid(x):
    return 1.0 / (1.0 + jnp.exp(-x))


def _log_sigmoid(x):
    return jnp.minimum(x, 0.0) - jnp.log(1.0 + jnp.exp(-jnp.abs(x)))


def _gelu(x):
    return 0.5 * x * (1.0 + lax.erf(x * (2.0 ** -0.5)))


def _mod_kernel(c_ref, w_ref, b_ref, o_ref):
    c = c_ref[...]
    ca = c * _sigmoid(c)
    o_ref[...] = jnp.dot(ca, w_ref[...], precision=lax.Precision.HIGHEST,
                         preferred_element_type=_F32) + b_ref[...]


def _mod_call(c, w_ada, b_ada):
    B, D = c.shape
    N = w_ada.shape[1]
    tn = 1536
    return pl.pallas_call(
        _mod_kernel,
        grid=(N // tn,),
        in_specs=[pl.BlockSpec((B, D), lambda j: (0, 0)),
                  pl.BlockSpec((D, tn), lambda j: (0, j)),
                  pl.BlockSpec((1, tn), lambda j: (0, j))],
        out_specs=pl.BlockSpec((B, tn), lambda j: (0, j)),
        out_shape=jax.ShapeDtypeStruct((B, N), _F32),
        name="mod",
    )(c, w_ada, b_ada.reshape(1, N))


def _lane_scan(x, op, ident, reverse):
    L = x.shape[1]
    lane = lax.broadcasted_iota(jnp.int32, x.shape, 1)
    d = 1
    while d < L:
        if reverse:
            shifted = jnp.where(lane < L - d, pltpu.roll(x, L - d, axis=1), ident)
        else:
            shifted = jnp.where(lane >= d, pltpu.roll(x, d, axis=1), ident)
        x = op(x, shifted)
        d *= 2
    return x


def _in_proj_kernel(x_ref, xp_ref, xn_ref, g_ref, sc_ref, sh_ref, w_ref, wvt_ref, wg_ref,
                    bg_ref, cw_ref, cb_ref, z_ref, vt_ref, grow_ref, zq_s, zk_s):
    tm = x_ref.shape[1]
    L = MLSTM_CHUNK
    i = pl.program_id(1)

    def norm_mod(x):
        ms = jnp.mean(x * x, axis=-1, keepdims=True)
        h = x * lax.rsqrt(ms + EPS) * g_ref[...]
        return (h * (1.0 + sc_ref[0]) + sh_ref[0]).astype(_BF)

    hb = norm_mod(x_ref[0])
    hb_prev = jnp.where(i > 0, norm_mod(xp_ref[0]), jnp.zeros((HALO, D_MODEL), _BF))
    hb_next = jnp.where(i < pl.num_programs(1) - 1, norm_mod(xn_ref[0]),
                        jnp.zeros((HALO, D_MODEL), _BF))
    hb_ext = jnp.concatenate([hb_prev, hb, hb_next], axis=0)

    def plain(c0):
        cols = slice(c0, c0 + MXU_COLS)
        z_ref[0, :, cols] = jnp.dot(hb, w_ref[:, cols],
                                    preferred_element_type=_F32).astype(_BF)

    def conv_matmul(j, ze_s):
        for c0 in range(0, D_MODEL, MXU_COLS):
            ze_s[:, c0:c0 + MXU_COLS] = jnp.dot(
                hb_ext, w_ref[:, j * D_MODEL + c0:j * D_MODEL + c0 + MXU_COLS],
                preferred_element_type=_F32)

    def conv_silu(j, ze_s, r0):
        n = CONV_ROWS + 16
        mid = slice(8, 8 + CONV_ROWS)
        for c0 in range(0, D_MODEL, CONV_COLS):
            cols = slice(j * D_MODEL + c0, j * D_MODEL + c0 + CONV_COLS)
            zb = ze_s[pl.ds(HALO - 8 + r0, n), c0:c0 + CONV_COLS]
            y = cb_ref[:, cols] + pltpu.roll(zb, 1, axis=0)[mid] * cw_ref[0:1, cols]
            y = y + zb[mid] * cw_ref[1:2, cols]
            y = y + pltpu.roll(zb, n - 1, axis=0)[mid] * cw_ref[2:3, cols]
            y = y / (1.0 + jnp.exp2(y * -_LOG2E))
            if j == 0:
                y = y * (HEAD_DIM ** -0.5)
            z_ref[0, pl.ds(r0, CONV_ROWS), cols] = y.astype(_BF)

    def gates_pre():
        gc = jnp.dot(hb, wg_ref[...], preferred_element_type=_F32) + bg_ref[...]
        return gc.T

    def gates_row(gt):
        nr = HEADS * GATE_SLOTS
        lst = _log_sigmoid(gt[:nr])
        git = gt[nr:2 * nr]
        slot = lax.broadcasted_iota(jnp.int32, (nr, L), 0) & (GATE_SLOTS - 1)
        fwd = slot < 3
        for c in range(tm // L):
            cols = slice(c * L, (c + 1) * L)
            lsc = lst[:, cols]
            cum = jnp.where(fwd, _lane_scan(lsc, jnp.add, 0.0, False),
                            _lane_scan(lsc, jnp.add, 0.0, True))
            base = jnp.where((slot == 0) | (slot == 3), cum, git[:, cols] - cum)
            cmax = jnp.where(fwd, _lane_scan(base, jnp.maximum, -jnp.inf, False),
                             _lane_scan(base, jnp.maximum, -jnp.inf, True))
            grow_ref[0, :, cols] = jnp.where((slot == 2) | (slot == 5), cmax, base)

    plain_blocks = iter(range(2 * D_MODEL, Z_COLS, MXU_COLS))
    gt = gates_pre()
    conv_matmul(0, zq_s)
    conv_matmul(1, zk_s)
    for j, ze_s in enumerate((zq_s, zk_s)):
        for r0 in range(0, tm, CONV_ROWS):
            conv_silu(j, ze_s, r0)
            plain(next(plain_blocks))
    gates_row(gt)
    for c0 in plain_blocks:
        plain(c0)
    vt_ref[0] = lax.dot_general(wvt_ref[...], hb, _NT,
                                preferred_element_type=_F32).astype(_BF)


def _in_proj_call(x, norm_g, sc, sh, w_bf, wvt_bf, wg_bf, bias_g, conv_w, conv_b, tm):
    B, S, D = x.shape
    r = tm // HALO
    last = S // HALO - 1
    return pl.pallas_call(
        _in_proj_kernel,
        grid=(B, S // tm),
        in_specs=[pl.BlockSpec((1, tm, D), lambda b, i: (b, i, 0)),
                  pl.BlockSpec((1, HALO, D), lambda b, i: (b, jnp.maximum(i * r - 1, 0), 0)),
                  pl.BlockSpec((1, HALO, D),
                               lambda b, i: (b, jnp.minimum((i + 1) * r, last), 0)),
                  _const_spec((1, D)),
                  pl.BlockSpec((1, 1, D), lambda b, i: (b, 0, 0)),
                  pl.BlockSpec((1, 1, D), lambda b, i: (b, 0, 0)),
                  _const_spec((D, Z_COLS)),
                  _const_spec((D, D)),
                  _const_spec((D, 128)),
                  _const_spec((1, 128)),
                  _const_spec((3, 2 * D)),
                  _const_spec((1, 2 * D))],
        out_specs=[pl.BlockSpec((1, tm, Z_COLS), lambda b, i: (b, i, 0)),
                   pl.BlockSpec((1, D, tm), lambda b, i: (b, 0, i)),
                   pl.BlockSpec((1, HEADS * GATE_SLOTS, tm), lambda b, i: (b, 0, i))],
        out_shape=[jax.ShapeDtypeStruct((B, S, Z_COLS), _BF),
                   jax.ShapeDtypeStruct((B, D, S), _BF),
                   jax.ShapeDtypeStruct((B, HEADS * GATE_SLOTS, S), _F32)],
        scratch_shapes=[pltpu.VMEM((tm + 2 * HALO, D), _F32),
                        pltpu.VMEM((tm + 2 * HALO, D), _F32)],
        compiler_params=pltpu.CompilerParams(
            dimension_semantics=("arbitrary", "arbitrary"),
            vmem_limit_bytes=VMEM_LIMIT),
        name="in_proj",
    )(x, x, x, norm_g, sc, sh, w_bf, wvt_bf, wg_bf, bias_g, conv_w, conv_b)


def _mlstm_kernel(q_ref, k_ref, vt_ref, zo_ref, gr_ref, mhg_ref,
                  y_ref,
                  pvf_s, pvb_s, u_s, snap_s, ct_s, mh_s, dec_s):
    S = q_ref.shape[1]
    L = MLSTM_CHUNK
    NC = S // L
    dh = HEAD_DIM
    R = STATE_ROWS
    K0 = (0, 3)
    LAST = (L - 1, 0)

    def gate_rows(gr, d):
        k0 = K0[d]
        return gr[k0:k0 + 1], gr[k0 + 1:k0 + 2], gr[k0 + 2:k0 + 3]

    m = [jnp.zeros((1, 1), _F32), jnp.zeros((1, 1), _F32)]
    for i in range(NC):
        for d, c in enumerate((i, NC - 1 - i)):
            b_row, _, cmax_row = gate_rows(gr_ref[0, 0, :, c * L:(c + 1) * L], d)
            a_last = jnp.maximum(m[d], cmax_row[:, LAST[d]:LAST[d] + 1])
            mh_s[d * NC + c:d * NC + c + 1, :] = jnp.broadcast_to(m[d], (1, 128))
            dec_s[d * NC + c:d * NC + c + 1, :] = jnp.broadcast_to(jnp.exp(m[d] - a_last),
                                                                  (1, 128))
            m[d] = b_row[:, LAST[d]:LAST[d] + 1] + a_last

    si = lax.broadcasted_iota(jnp.int32, (L, L), 0)
    ti = lax.broadcasted_iota(jnp.int32, (L, L), 1)
    masks = (si <= ti, si >= ti)
    ones_row = (lax.broadcasted_iota(jnp.int32, (R - dh, L), 0) == 0).astype(_BF)

    def pass0(c, carry):
        t0 = pl.multiple_of(c * L, L)
        qc = q_ref[0, pl.ds(t0, L), :]
        kc = k_ref[0, pl.ds(t0, L), :]
        vta = jnp.concatenate([vt_ref[0, :, pl.ds(t0, L)], ones_row], axis=0)
        vta32 = vta.astype(_F32)
        gr = gr_ref[0, 0, :, pl.ds(t0, L)]
        s_t = lax.dot_general(kc, qc, _NT, preferred_element_type=_F32)
        p, vtw = [], []
        for d in range(2):
            _, r_row, cmax_row = gate_rows(gr, d)
            r_rep = jnp.broadcast_to(r_row, (128, L)).T
            r_col = jnp.concatenate([r_rep] * (L // 128), axis=1)
            a_row = jnp.maximum(mh_s[pl.ds(d * NC + c, 1), 0:1], cmax_row)
            p.append((jnp.exp(jnp.where(masks[d], r_col - a_row, -jnp.inf)) * s_t).astype(_BF))
            w_row = jnp.exp(r_row - a_row[:, LAST[d]:LAST[d] + 1])
            vtw.append((vta32 * w_row).astype(_BF))
        pv = jnp.dot(vta, jnp.concatenate(p, axis=1), preferred_element_type=_F32)
        pvf_s[:, pl.ds(t0, L)] = pv[:, :L]
        pvb_s[:, pl.ds(t0, L)] = pv[:, L:]
        u_s[c] = jnp.dot(jnp.concatenate(vtw, axis=0), kc, preferred_element_type=_F32)
        return carry

    lax.fori_loop(0, NC, pass0, 0, unroll=8)

    ct_s[...] = jnp.zeros_like(ct_s)

    def pass1(i, carry):
        for d, c in enumerate((i, NC - 1 - i)):
            rows = slice(d * R, (d + 1) * R)
            ct = ct_s[rows, :]
            snap_s[c, rows, :] = ct.astype(_BF)
            ct_s[rows, :] = dec_s[pl.ds(d * NC + c, 1), 0:1] * ct + u_s[c, rows, :]
        return carry

    lax.fori_loop(0, NC, pass1, 0, unroll=2)

    def pass2(c, carry):
        t0 = pl.multiple_of(c * L, L)
        qc = q_ref[0, pl.ds(t0, L), :]
        gr = gr_ref[0, 0, :, pl.ds(t0, L)]
        inter = lax.dot_general(snap_s[c], qc, _NT, preferred_element_type=_F32)
        h_t = None
        for d, pv_s in enumerate((pvf_s, pvb_s)):
            b_row, _, cmax_row = gate_rows(gr, d)
            m_c = mh_s[pl.ds(d * NC + c, 1), 0:1]
            a_row = jnp.maximum(m_c, cmax_row)
            num_t = jnp.exp(m_c - a_row) * inter[d * R:(d + 1) * R] + pv_s[:, pl.ds(t0, L)]
            den = num_t[dh:dh + 1, :]
            scale = 1.0 / jnp.maximum(jnp.abs(den), jnp.exp(-(b_row + a_row)))
            h_d = num_t[:dh, :] * scale
            h_t = h_d if h_t is None else h_t + h_d
        h = h_t.T
        hn = h * lax.rsqrt(jnp.mean(h * h, axis=-1, keepdims=True) + EPS) * mhg_ref[...]
        o = zo_ref[0, pl.ds(t0, L), :].astype(_F32)
        y_ref[0, pl.ds(t0, L), :] = (hn * _sigmoid(o)).astype(_BF)
        return carry

    lax.fori_loop(0, NC, pass2, 0, unroll=4)


def _mlstm_call(z, vt, grow, mh_g):
    B, S, _ = z.shape
    dh = HEAD_DIM
    H = HEADS
    nc = S // MLSTM_CHUNK

    def zspec(off):
        return pl.BlockSpec((1, S, dh), lambda b, h, off=off: (b, 0, off + h))

    return pl.pallas_call(
        _mlstm_kernel,
        grid=(B, H),
        in_specs=[zspec(0), zspec(H),
                  pl.BlockSpec((1, dh, S), lambda b, h: (b, h, 0)),
                  zspec(2 * H),
                  pl.BlockSpec((1, 1, 8, S), lambda b, h: (b, h, 0, 0)),
                  pl.BlockSpec((1, dh), lambda b, h: (0, h))],
        out_specs=pl.BlockSpec((1, S, dh), lambda b, h: (b, 0, h)),
        out_shape=jax.ShapeDtypeStruct((B, S, H * dh), _BF),
        scratch_shapes=[pltpu.VMEM((STATE_ROWS, S), _F32), pltpu.VMEM((STATE_ROWS, S), _F32),
                        pltpu.VMEM((nc, 2 * STATE_ROWS, dh), _F32),
                        pltpu.VMEM((nc, 2 * STATE_ROWS, dh), _BF),
                        pltpu.VMEM((2 * STATE_ROWS, dh), _F32),
                        pltpu.VMEM((2 * nc, 128), _F32), pltpu.VMEM((2 * nc, 128), _F32)],
        compiler_params=pltpu.CompilerParams(
            dimension_semantics=("arbitrary", "arbitrary"),
            vmem_limit_bytes=VMEM_LIMIT),
        name="mlstm",
    )(z, z, vt, z, grow, mh_g)


def _mix_kernel(x_ref, zu_ref, zv_ref, zga_ref, zgb_ref, ya_ref, g1_ref, lng_ref, lnb_ref,
                ws_ref, bs_ref, wo_ref, o_ref, mix_s):
    tm = x_ref.shape[1]
    P = SGU_CHUNK
    C = SGU_GROUP_DIM
    for j in range(tm // P):
        rows = slice(j * P, (j + 1) * P)
        gv = _gelu(zv_ref[0, rows, :].astype(_F32))
        mu = jnp.mean(gv, axis=-1, keepdims=True)
        dv = gv - mu
        var = jnp.mean(dv * dv, axis=-1, keepdims=True)
        vn = (dv * lax.rsqrt(var + EPS) * lng_ref[...] + lnb_ref[...]).astype(_BF)
        s = jnp.concatenate(
            [jnp.dot(ws_ref[g], vn[:, g * C:(g + 1) * C], preferred_element_type=_F32)
             for g in range(SGU_GROUPS)], axis=1) + bs_ref[...]
        yb = _gelu(zu_ref[0, rows, :].astype(_F32)) * s
        ga = _sigmoid(zga_ref[0, rows, :].astype(_F32))
        gb = _sigmoid(zgb_ref[0, rows, :].astype(_F32))
        mix_s[rows, :] = (ga * ya_ref[0, rows, :].astype(_F32) + gb * yb).astype(_BF)
    upd = jnp.dot(mix_s[...], wo_ref[...], preferred_element_type=_F32)
    o_ref[0] = x_ref[0] + g1_ref[0] * upd


def _mix_call(x, z, ya, g1, ln_g, ln_b, ws_bf, bs_full, wo_bf, tm):
    B, S, D = x.shape

    def zspec(j):
        return pl.BlockSpec((1, tm, D), lambda b, i, j=j: (b, i, j))

    return pl.pallas_call(
        _mix_kernel,
        grid=(B, S // tm),
        in_specs=[zspec(0), zspec(3), zspec(4), zspec(5), zspec(6), zspec(0),
                  pl.BlockSpec((1, 1, D), lambda b, i: (b, 0, 0)),
                  _const_spec((1, D)), _const_spec((1, D)),
                  _const_spec((SGU_GROUPS, SGU_CHUNK, SGU_CHUNK)),
                  _const_spec((SGU_CHUNK, D)),
                  _const_spec((D, D))],
        out_specs=pl.BlockSpec((1, tm, D), lambda b, i: (b, i, 0)),
        out_shape=jax.ShapeDtypeStruct((B, S, D), _F32),
        scratch_shapes=[pltpu.VMEM((tm, D), _BF)],
        compiler_params=pltpu.CompilerParams(
            dimension_semantics=("arbitrary", "arbitrary"),
            vmem_limit_bytes=VMEM_LIMIT),
        name="mix",
    )(x, z, z, z, z, ya, g1, ln_g, ln_b, ws_bf, bs_full, wo_bf)


def _ffn_kernel(x_ref, g_ref, sc_ref, sh_ref, g2_ref, w1_ref, w2_ref, nf_ref, o_ref):
    x = x_ref[0]
    ms = jnp.mean(x * x, axis=-1, keepdims=True)
    h = x * lax.rsqrt(ms + EPS) * g_ref[...]
    hb = (h * (1.0 + sc_ref[0]) + sh_ref[0]).astype(_BF)
    ff = jnp.zeros(x.shape, _F32)
    for j in range(FF_DIM // D_MODEL):
        cols = slice(j * D_MODEL, (j + 1) * D_MODEL)
        a = jnp.maximum(jnp.dot(hb, w1_ref[:, cols], preferred_element_type=_F32), 0.0)
        ff = ff + jnp.dot((a * a).astype(_BF), w2_ref[cols, :], preferred_element_type=_F32)
    x2 = x + g2_ref[0] * ff
    ms2 = jnp.mean(x2 * x2, axis=-1, keepdims=True)
    o_ref[0] = x2 * lax.rsqrt(ms2 + EPS) * nf_ref[...]


def _ffn_call(x1, norm_g, sc, sh, g2, w1_bf, w2_bf, normf_g, tm):
    B, S, D = x1.shape
    vec = pl.BlockSpec((1, 1, D), lambda b, i: (b, 0, 0))
    return pl.pallas_call(
        _ffn_kernel,
        grid=(B, S // tm),
        in_specs=[pl.BlockSpec((1, tm, D), lambda b, i: (b, i, 0)),
                  _const_spec((1, D)), vec, vec, vec,
                  _const_spec((D, FF_DIM)), _const_spec((FF_DIM, D)),
                  _const_spec((1, D))],
        out_specs=pl.BlockSpec((1, tm, D), lambda b, i: (b, i, 0)),
        out_shape=jax.ShapeDtypeStruct((B, S, D), _F32),
        compiler_params=pltpu.CompilerParams(
            dimension_semantics=("arbitrary", "arbitrary"),
            vmem_limit_bytes=VMEM_LIMIT),
        name="ffn",
    )(x1, norm_g, sc, sh, g2, w1_bf, w2_bf, normf_g)


def _gate_weights(w_if, b_if):
    D = w_if.shape[0]
    H = HEADS
    zero = 4 * H
    idx_f, idx_i = [], []
    for h in range(H):
        idx_f += [H + h] * 3 + [3 * H + h] * 3 + [zero] * 2
        idx_i += [zero, h, h, zero, 2 * H + h, 2 * H + h, zero, zero]
    idx = jnp.array(idx_f + idx_i + [zero] * (128 - 2 * H * GATE_SLOTS), jnp.int32)
    w_ext = jnp.concatenate([w_if, jnp.zeros((D, 1), w_if.dtype)], axis=1)
    b_ext = jnp.concatenate([b_if.reshape(-1), jnp.zeros((1,), b_if.dtype)])
    return jnp.take(w_ext, idx, axis=1), jnp.take(b_ext, idx).reshape(1, 128)


def kernel(x, c, w_ada, b_ada, norm1_g, norm2_g, w_in, b_if, conv_w, conv_b, mh_g,
           ln_v_g, ln_v_b, w_s, b_s, w_out, w1, w2, normf_g):
    B, S, D = x.shape
    H = HEADS
    assert w_ada.shape[0] == 1, "single layer"

    mod = _mod_call(c, w_ada[0], b_ada[0])
    sh1, sc1, g1, sh2, sc2, g2 = (mod[:, k * D:(k + 1) * D].reshape(B, 1, D) for k in range(6))

    w_in0 = w_in[0]
    w_z = jnp.concatenate([w_in0[:, :2 * D], w_in0[:, 3 * D:8 * D]], axis=1).astype(_BF)
    w_vt = w_in0[:, 2 * D:3 * D].T.astype(_BF)
    w_g, b_g = _gate_weights(w_in0[:, 8 * D:], b_if[0])
    z, vt, grow = _in_proj_call(x, norm1_g, sc1, sh1, w_z, w_vt, w_g.astype(_BF), b_g,
                                conv_w[0], conv_b, tm=512)

    ya = _mlstm_call(z, vt, grow.reshape(B, H, GATE_SLOTS, S), mh_g)

    bs_full = jnp.repeat(b_s[0].T, SGU_GROUP_DIM, axis=1)
    x1 = _mix_call(x, z, ya, g1, ln_v_g, ln_v_b, w_s[0].astype(_BF), bs_full,
                   w_out[0].astype(_BF), tm=512)

    return _ffn_call(x1, norm2_g, sc2, sh2, g2, w1[0].astype(_BF), w2[0].astype(_BF),
                     normf_g.reshape(1, D), tm=512)
```

```python
import jax
import jax.numpy as jnp
from jax import lax
from jax.experimental import pallas as pl
from jax.experimental.pallas import tpu as pltpu

D_MODEL = 1024
HEADS = 4
HEAD_DIM = 256
SGU_GROUPS = 8
SGU_GROUP_DIM = 128
SGU_CHUNK = 128
FF_DIM = 4096
EPS = 1e-6
Z_COLS = 7 * D_MODEL

MLSTM_CHUNK = 256
STATE_ROWS = HEAD_DIM + 16
HALO = 16
GATE_SLOTS = 8
CONV_ROWS = 64
CONV_COLS = 128
MXU_COLS = 256

VMEM_LIMIT = 56 * 1024 * 1024

_BF = jnp.bfloat16
_F32 = jnp.float32
_NT = (((1,), (1,)), ((), ()))
_LOG2E = 1.4426950408889634


def _const_spec(shape):
    nd = len(shape)
    return pl.BlockSpec(shape, lambda *_: (0,) * nd, pipeline_mode=pl.Buffered(1))


def _sigmoid(x):
    return 1.0 / (1.0 + jnp.exp(-x))


def _log_sigmoid(x):
    return jnp.minimum(x, 0.0) - jnp.log(1.0 + jnp.exp(-jnp.abs(x)))


def _gelu(x):
    return 0.5 * x * (1.0 + lax.erf(x * (2.0 ** -0.5)))


def _mod_kernel(c_ref, w_ref, b_ref, o_ref):
    c = c_ref[...]
    ca = c * _sigmoid(c)
    o_ref[...] = jnp.dot(ca, w_ref[...], precision=lax.Precision.HIGHEST,
                         preferred_element_type=_F32) + b_ref[...]


def _mod_call(c, w_ada, b_ada):
    B, D = c.shape
    N = w_ada.shape[1]
    tn = 1536
    return pl.pallas_call(
        _mod_kernel,
        grid=(N // tn,),
        in_specs=[pl.BlockSpec((B, D), lambda j: (0, 0)),
                  pl.BlockSpec((D, tn), lambda j: (0, j)),
                  pl.BlockSpec((1, tn), lambda j: (0, j))],
        out_specs=pl.BlockSpec((B, tn), lambda j: (0, j)),
        out_shape=jax.ShapeDtypeStruct((B, N), _F32),
        name="mod",
    )(c, w_ada, b_ada.reshape(1, N))


def _lane_scan(x, op, ident, reverse):
    L = x.shape[1]
    lane = lax.broadcasted_iota(jnp.int32, x.shape, 1)
    d = 1
    while d < L:
        if reverse:
            shifted = jnp.where(lane < L - d, pltpu.roll(x, L - d, axis=1), ident)
        else:
            shifted = jnp.where(lane >= d, pltpu.roll(x, d, axis=1), ident)
        x = op(x, shifted)
        d *= 2
    return x


def _in_proj_kernel(x_ref, xp_ref, xn_ref, g_ref, sc_ref, sh_ref, w_ref, wg_ref,
                    bg_ref, cw_ref, cb_ref, z_ref, vt_ref, grow_ref, zq_s, zk_s, wvt_s):
    tm = x_ref.shape[1]
    L = MLSTM_CHUNK
    i = pl.program_id(1)

    def norm_mod(x):
        ms = jnp.mean(x * x, axis=-1, keepdims=True)
        h = x * lax.rsqrt(ms + EPS) * g_ref[...]
        return (h * (1.0 + sc_ref[0]) + sh_ref[0]).astype(_BF)

    hb = norm_mod(x_ref[0])
    hb_prev = jnp.where(i > 0, norm_mod(xp_ref[0]), jnp.zeros((HALO, D_MODEL), _BF))
    hb_next = jnp.where(i < pl.num_programs(1) - 1, norm_mod(xn_ref[0]),
                        jnp.zeros((HALO, D_MODEL), _BF))
    hb_ext = jnp.concatenate([hb_prev, hb, hb_next], axis=0)

    def plain(c0):
        cols = slice(c0, c0 + MXU_COLS)
        wcols = slice(c0 + D_MODEL, c0 + D_MODEL + MXU_COLS)
        r = jnp.dot(hb, w_ref[:, wcols], preferred_element_type=_F32)
        if 3 * D_MODEL <= c0 < 5 * D_MODEL:
            r = _gelu(r)
        else:
            r = 1.0 / (1.0 + jnp.exp2(r * -_LOG2E))
        z_ref[0, :, cols] = r.astype(_BF)

    def conv_matmul(j, ze_s):
        for c0 in range(0, D_MODEL, MXU_COLS):
            ze_s[:, c0:c0 + MXU_COLS] = jnp.dot(
                hb_ext, w_ref[:, j * D_MODEL + c0:j * D_MODEL + c0 + MXU_COLS],
                preferred_element_type=_F32)

    def conv_silu(j, ze_s, r0):
        n = CONV_ROWS + 16
        mid = slice(8, 8 + CONV_ROWS)
        for c0 in range(0, D_MODEL, CONV_COLS):
            cols = slice(j * D_MODEL + c0, j * D_MODEL + c0 + CONV_COLS)
            zb = ze_s[pl.ds(HALO - 8 + r0, n), c0:c0 + CONV_COLS]
            y = cb_ref[:, cols] + pltpu.roll(zb, 1, axis=0)[mid] * cw_ref[0:1, cols]
            y = y + zb[mid] * cw_ref[1:2, cols]
            y = y + pltpu.roll(zb, n - 1, axis=0)[mid] * cw_ref[2:3, cols]
            y = y / (1.0 + jnp.exp2(y * -_LOG2E))
            if j == 0:
                y = y * (HEAD_DIM ** -0.5)
            z_ref[0, pl.ds(r0, CONV_ROWS), cols] = y.astype(_BF)

    def gates_pre():
        gc = jnp.dot(hb, wg_ref[...], preferred_element_type=_F32) + bg_ref[...]
        return gc.T

    def gates_row(gt):
        nr = HEADS * GATE_SLOTS
        lst = _log_sigmoid(gt[:nr])
        git = gt[nr:2 * nr]
        slot = lax.broadcasted_iota(jnp.int32, (nr, L), 0) & (GATE_SLOTS - 1)
        fwd = slot < 3
        for c in range(tm // L):
            cols = slice(c * L, (c + 1) * L)
            lsc = lst[:, cols]
            cum = jnp.where(fwd, _lane_scan(lsc, jnp.add, 0.0, False),
                            _lane_scan(lsc, jnp.add, 0.0, True))
            base = jnp.where((slot == 0) | (slot == 3), cum, git[:, cols] - cum)
            cmax = jnp.where(fwd, _lane_scan(base, jnp.maximum, -jnp.inf, False),
                             _lane_scan(base, jnp.maximum, -jnp.inf, True))
            grow_ref[0, :, cols] = jnp.where((slot == 2) | (slot == 5), cmax, base)

    @pl.when((pl.program_id(0) == 0) & (i == 0))
    def _():
        wvt_s[...] = w_ref[:, 2 * D_MODEL:3 * D_MODEL].astype(_F32).T.astype(_BF)

    plain_blocks = iter(range(2 * D_MODEL, Z_COLS, MXU_COLS))
    gt = gates_pre()
    conv_matmul(0, zq_s)
    conv_matmul(1, zk_s)
    for j, ze_s in enumerate((zq_s, zk_s)):
        for r0 in range(0, tm, CONV_ROWS):
            conv_silu(j, ze_s, r0)
            plain(next(plain_blocks))
    gates_row(gt)
    for c0 in plain_blocks:
        plain(c0)
    vt_ref[0] = lax.dot_general(wvt_s[...], hb, _NT,
                                preferred_element_type=_F32).astype(_BF)


def _in_proj_call(x, norm_g, sc, sh, w_bf, wg_bf, bias_g, conv_w, conv_b, tm):
    B, S, D = x.shape
    r = tm // HALO
    last = S // HALO - 1
    return pl.pallas_call(
        _in_proj_kernel,
        grid=(B, S // tm),
        in_specs=[pl.BlockSpec((1, tm, D), lambda b, i: (b, i, 0)),
                  pl.BlockSpec((1, HALO, D), lambda b, i: (b, jnp.maximum(i * r - 1, 0), 0)),
                  pl.BlockSpec((1, HALO, D),
                               lambda b, i: (b, jnp.minimum((i + 1) * r, last), 0)),
                  _const_spec((1, D)),
                  pl.BlockSpec((1, 1, D), lambda b, i: (b, 0, 0)),
                  pl.BlockSpec((1, 1, D), lambda b, i: (b, 0, 0)),
                  _const_spec((D, Z_COLS + D)),
                  _const_spec((D, 128)),
                  _const_spec((1, 128)),
                  _const_spec((3, 2 * D)),
                  _const_spec((1, 2 * D))],
        out_specs=[pl.BlockSpec((1, tm, Z_COLS), lambda b, i: (b, i, 0)),
                   pl.BlockSpec((1, D, tm), lambda b, i: (b, 0, i)),
                   pl.BlockSpec((1, HEADS * GATE_SLOTS, tm), lambda b, i: (b, 0, i))],
        out_shape=[jax.ShapeDtypeStruct((B, S, Z_COLS), _BF),
                   jax.ShapeDtypeStruct((B, D, S), _BF),
                   jax.ShapeDtypeStruct((B, HEADS * GATE_SLOTS, S), _F32)],
        scratch_shapes=[pltpu.VMEM((tm + 2 * HALO, D), _F32),
                        pltpu.VMEM((tm + 2 * HALO, D), _F32),
                        pltpu.VMEM((D, D), _BF)],
        compiler_params=pltpu.CompilerParams(
            dimension_semantics=("arbitrary", "arbitrary"),
            vmem_limit_bytes=VMEM_LIMIT),
        name="in_proj",
    )(x, x, x, norm_g, sc, sh, w_bf, wg_bf, bias_g, conv_w, conv_b)


def _mlstm_kernel(q_ref, k_ref, vt_ref, zo_ref, gr_ref, mhg_ref,
                  y_ref,
                  pvf_s, pvb_s, u_s, snap_s, ct_s, mh_s, dec_s):
    S = q_ref.shape[1]
    L = MLSTM_CHUNK
    NC = S // L
    dh = HEAD_DIM
    R = STATE_ROWS
    K0 = (0, 3)
    LAST = (L - 1, 0)

    def gate_rows(gr, d):
        k0 = K0[d]
        return gr[k0:k0 + 1], gr[k0 + 1:k0 + 2], gr[k0 + 2:k0 + 3]

    m = [jnp.zeros((1, 1), _F32), jnp.zeros((1, 1), _F32)]
    for i in range(NC):
        for d, c in enumerate((i, NC - 1 - i)):
            b_row, _, cmax_row = gate_rows(gr_ref[0, 0, :, c * L:(c + 1) * L], d)
            a_last = jnp.maximum(m[d], cmax_row[:, LAST[d]:LAST[d] + 1])
            mh_s[d * NC + c:d * NC + c + 1, :] = jnp.broadcast_to(m[d], (1, 128))
            dec_s[d * NC + c:d * NC + c + 1, :] = jnp.broadcast_to(jnp.exp(m[d] - a_last),
                                                                  (1, 128))
            m[d] = b_row[:, LAST[d]:LAST[d] + 1] + a_last

    si = lax.broadcasted_iota(jnp.int32, (L, L), 0)
    ti = lax.broadcasted_iota(jnp.int32, (L, L), 1)
    masks = (si <= ti, si >= ti)
    ones_row = (lax.broadcasted_iota(jnp.int32, (R - dh, L), 0) == 0).astype(_BF)

    def pass0(c, carry):
        t0 = pl.multiple_of(c * L, L)
        qc = q_ref[0, pl.ds(t0, L), :]
        kc = k_ref[0, pl.ds(t0, L), :]
        vta = jnp.concatenate([vt_ref[0, :, pl.ds(t0, L)], ones_row], axis=0)
        vta32 = vta.astype(_F32)
        gr = gr_ref[0, 0, :, pl.ds(t0, L)]
        s_t = lax.dot_general(kc, qc, _NT, preferred_element_type=_F32)
        p, vtw = [], []
        for d in range(2):
            _, r_row, cmax_row = gate_rows(gr, d)
            r_rep = jnp.broadcast_to(r_row, (128, L)).T
            r_col = jnp.concatenate([r_rep] * (L // 128), axis=1)
            a_row = jnp.maximum(mh_s[pl.ds(d * NC + c, 1), 0:1], cmax_row)
            p.append((jnp.exp(jnp.where(masks[d], r_col - a_row, -jnp.inf)) * s_t).astype(_BF))
            w_row = jnp.exp(r_row - a_row[:, LAST[d]:LAST[d] + 1])
            vtw.append((vta32 * w_row).astype(_BF))
        pv = jnp.dot(vta, jnp.concatenate(p, axis=1), preferred_element_type=_F32)
        pvf_s[:, pl.ds(t0, L)] = pv[:, :L]
        pvb_s[:, pl.ds(t0, L)] = pv[:, L:]
        u_s[c] = jnp.dot(jnp.concatenate(vtw, axis=0), kc, preferred_element_type=_F32)
        return carry

    lax.fori_loop(0, NC, pass0, 0, unroll=8)

    ct_s[...] = jnp.zeros_like(ct_s)

    def pass1(i, carry):
        for d, c in enumerate((i, NC - 1 - i)):
            rows = slice(d * R, (d + 1) * R)
            ct = ct_s[rows, :]
            snap_s[c, rows, :] = ct.astype(_BF)
            ct_s[rows, :] = dec_s[pl.ds(d * NC + c, 1), 0:1] * ct + u_s[c, rows, :]
        return carry

    lax.fori_loop(0, NC, pass1, 0, unroll=2)

    def pass2(c, carry):
        t0 = pl.multiple_of(c * L, L)
        qc = q_ref[0, pl.ds(t0, L), :]
        gr = gr_ref[0, 0, :, pl.ds(t0, L)]
        inter = lax.dot_general(snap_s[c], qc, _NT, preferred_element_type=_F32)
        h_t = None
        for d, pv_s in enumerate((pvf_s, pvb_s)):
            b_row, _, cmax_row = gate_rows(gr, d)
            m_c = mh_s[pl.ds(d * NC + c, 1), 0:1]
            a_row = jnp.maximum(m_c, cmax_row)
            num_t = jnp.exp(m_c - a_row) * inter[d * R:(d + 1) * R] + pv_s[:, pl.ds(t0, L)]
            den = num_t[dh:dh + 1, :]
            scale = 1.0 / jnp.maximum(jnp.abs(den), jnp.exp(-(b_row + a_row)))
            h_d = num_t[:dh, :] * scale
            h_t = h_d if h_t is None else h_t + h_d
        h = h_t.T
        hn = h * lax.rsqrt(jnp.mean(h * h, axis=-1, keepdims=True) + EPS) * mhg_ref[...]
        o = zo_ref[0, pl.ds(t0, L), :].astype(_F32)
        y_ref[0, pl.ds(t0, L), :] = (hn * o).astype(_BF)
        return carry

    lax.fori_loop(0, NC, pass2, 0, unroll=4)


def _mlstm_call(z, vt, grow, mh_g):
    B, S, _ = z.shape
    dh = HEAD_DIM
    H = HEADS
    nc = S // MLSTM_CHUNK

    def zspec(off):
        return pl.BlockSpec((1, S, dh), lambda b, h, off=off: (b, 0, off + h))

    return pl.pallas_call(
        _mlstm_kernel,
        grid=(B, H),
        in_specs=[zspec(0), zspec(H),
                  pl.BlockSpec((1, dh, S), lambda b, h: (b, h, 0)),
                  zspec(2 * H),
                  pl.BlockSpec((1, 1, 8, S), lambda b, h: (b, h, 0, 0)),
                  pl.BlockSpec((1, dh), lambda b, h: (0, h))],
        out_specs=pl.BlockSpec((1, S, dh), lambda b, h: (b, 0, h)),
        out_shape=jax.ShapeDtypeStruct((B, S, H * dh), _BF),
        scratch_shapes=[pltpu.VMEM((STATE_ROWS, S), _F32), pltpu.VMEM((STATE_ROWS, S), _F32),
                        pltpu.VMEM((nc, 2 * STATE_ROWS, dh), _F32),
                        pltpu.VMEM((nc, 2 * STATE_ROWS, dh), _BF),
                        pltpu.VMEM((2 * STATE_ROWS, dh), _F32),
                        pltpu.VMEM((2 * nc, 128), _F32), pltpu.VMEM((2 * nc, 128), _F32)],
        compiler_params=pltpu.CompilerParams(
            dimension_semantics=("arbitrary", "arbitrary"),
            vmem_limit_bytes=VMEM_LIMIT),
        name="mlstm",
    )(z, z, vt, z, grow, mh_g)


def _mix_kernel(x_ref, zu_ref, zv_ref, zga_ref, zgb_ref, ya_ref, g1_ref, lng_ref, lnb_ref,
                ws_ref, bs_ref, wo_ref, o_ref, mix_s):
    tm = x_ref.shape[1]
    P = SGU_CHUNK
    C = SGU_GROUP_DIM
    for j in range(tm // P):
        rows = slice(j * P, (j + 1) * P)
        gv = zv_ref[0, rows, :].astype(_F32)
        mu = jnp.mean(gv, axis=-1, keepdims=True)
        dv = gv - mu
        var = jnp.mean(dv * dv, axis=-1, keepdims=True)
        vn = (dv * lax.rsqrt(var + EPS) * lng_ref[...] + lnb_ref[...]).astype(_BF)
        s = jnp.concatenate(
            [jnp.dot(ws_ref[g], vn[:, g * C:(g + 1) * C], preferred_element_type=_F32)
             for g in range(SGU_GROUPS)], axis=1) + bs_ref[...]
        yb = zu_ref[0, rows, :].astype(_F32) * s
        ga = zga_ref[0, rows, :].astype(_F32)
        gb = zgb_ref[0, rows, :].astype(_F32)
        mix_s[rows, :] = (ga * ya_ref[0, rows, :].astype(_F32) + gb * yb).astype(_BF)
    upd = jnp.dot(mix_s[...], wo_ref[...], preferred_element_type=_F32)
    o_ref[0] = x_ref[0] + g1_ref[0] * upd


def _mix_call(x, z, ya, g1, ln_g, ln_b, ws_bf, bs_full, wo_bf, tm):
    B, S, D = x.shape

    def zspec(j):
        return pl.BlockSpec((1, tm, D), lambda b, i, j=j: (b, i, j))

    return pl.pallas_call(
        _mix_kernel,
        grid=(B, S // tm),
        in_specs=[zspec(0), zspec(3), zspec(4), zspec(5), zspec(6), zspec(0),
                  pl.BlockSpec((1, 1, D), lambda b, i: (b, 0, 0)),
                  _const_spec((1, D)), _const_spec((1, D)),
                  _const_spec((SGU_GROUPS, SGU_CHUNK, SGU_CHUNK)),
                  _const_spec((SGU_CHUNK, D)),
                  _const_spec((D, D))],
        out_specs=pl.BlockSpec((1, tm, D), lambda b, i: (b, i, 0)),
        out_shape=jax.ShapeDtypeStruct((B, S, D), _F32),
        scratch_shapes=[pltpu.VMEM((tm, D), _BF)],
        compiler_params=pltpu.CompilerParams(
            dimension_semantics=("arbitrary", "arbitrary"),
            vmem_limit_bytes=VMEM_LIMIT),
        name="mix",
    )(x, z, z, z, z, ya, g1, ln_g, ln_b, ws_bf, bs_full, wo_bf)


def _ffn_kernel(x_ref, g_ref, sc_ref, sh_ref, g2_ref, w1_ref, w2_ref, nf_ref, o_ref):
    x = x_ref[0]
    ms = jnp.mean(x * x, axis=-1, keepdims=True)
    h = x * lax.rsqrt(ms + EPS) * g_ref[...]
    hb = (h * (1.0 + sc_ref[0]) + sh_ref[0]).astype(_BF)
    ff = jnp.zeros(x.shape, _F32)
    for j in range(FF_DIM // D_MODEL):
        cols = slice(j * D_MODEL, (j + 1) * D_MODEL)
        a = jnp.maximum(jnp.dot(hb, w1_ref[:, cols], preferred_element_type=_F32), 0.0)
        ff = ff + jnp.dot((a * a).astype(_BF), w2_ref[cols, :], preferred_element_type=_F32)
    x2 = x + g2_ref[0] * ff
    ms2 = jnp.mean(x2 * x2, axis=-1, keepdims=True)
    o_ref[0] = x2 * lax.rsqrt(ms2 + EPS) * nf_ref[...]


def _ffn_call(x1, norm_g, sc, sh, g2, w1_bf, w2_bf, normf_g, tm):
    B, S, D = x1.shape
    vec = pl.BlockSpec((1, 1, D), lambda b, i: (b, 0, 0))
    return pl.pallas_call(
        _ffn_kernel,
        grid=(B, S // tm),
        in_specs=[pl.BlockSpec((1, tm, D), lambda b, i: (b, i, 0)),
                  _const_spec((1, D)), vec, vec, vec,
                  _const_spec((D, FF_DIM)), _const_spec((FF_DIM, D)),
                  _const_spec((1, D))],
        out_specs=pl.BlockSpec((1, tm, D), lambda b, i: (b, i, 0)),
        out_shape=jax.ShapeDtypeStruct((B, S, D), _F32),
        compiler_params=pltpu.CompilerParams(
            dimension_semantics=("arbitrary", "arbitrary"),
            vmem_limit_bytes=VMEM_LIMIT),
        name="ffn",
    )(x1, norm_g, sc, sh, g2, w1_bf, w2_bf, normf_g)


def _gate_weights(w_if, b_if):
    D = w_if.shape[0]
    H = HEADS
    zero = 4 * H
    idx_f, idx_i = [], []
    for h in range(H):
        idx_f += [H + h] * 3 + [3 * H + h] * 3 + [zero] * 2
        idx_i += [zero, h, h, zero, 2 * H + h, 2 * H + h, zero, zero]
    idx = jnp.array(idx_f + idx_i + [zero] * (128 - 2 * H * GATE_SLOTS), jnp.int32)
    w_ext = jnp.concatenate([w_if, jnp.zeros((D, 1), w_if.dtype)], axis=1)
    b_ext = jnp.concatenate([b_if.reshape(-1), jnp.zeros((1,), b_if.dtype)])
    return jnp.take(w_ext, idx, axis=1), jnp.take(b_ext, idx).reshape(1, 128)


def kernel(x, c, w_ada, b_ada, norm1_g, norm2_g, w_in, b_if, conv_w, conv_b, mh_g,
           ln_v_g, ln_v_b, w_s, b_s, w_out, w1, w2, normf_g):
    B, S, D = x.shape
    H = HEADS
    assert w_ada.shape[0] == 1, "single layer"

    mod = _mod_call(c, w_ada[0], b_ada[0])
    sh1, sc1, g1, sh2, sc2, g2 = (mod[:, k * D:(k + 1) * D].reshape(B, 1, D) for k in range(6))

    w_in0 = w_in[0]
    w_z = w_in0[:, :8 * D].astype(_BF)
    w_g, b_g = _gate_weights(w_in0[:, 8 * D:], b_if[0])
    z, vt, grow = _in_proj_call(x, norm1_g, sc1, sh1, w_z, w_g.astype(_BF), b_g,
                                conv_w[0], conv_b, tm=512)

    ya = _mlstm_call(z, vt, grow.reshape(B, H, GATE_SLOTS, S), mh_g)

    bs_full = jnp.repeat(b_s[0].T, SGU_GROUP_DIM, axis=1)
    x1 = _mix_call(x, z, ya, g1, ln_v_g, ln_v_b, w_s[0].astype(_BF), bs_full,
                   w_out[0].astype(_BF), tm=512)

    return _ffn_call(x1, norm2_g, sc2, sh2, g2, w1[0].astype(_BF), w2[0].astype(_BF),
                     normf_g.reshape(1, D), tm=512)
```

```python
import jax
import jax.numpy as jnp
from jax import lax
from jax.experimental import pallas as pl
from jax.experimental.pallas import tpu as pltpu

D_MODEL = 1024
HEADS = 4
HEAD_DIM = 256
SGU_GROUPS = 8
SGU_GROUP_DIM = 128
SGU_CHUNK = 128
FF_DIM = 4096
EPS = 1e-6
Z_COLS = 7 * D_MODEL

MLSTM_CHUNK = 256
STATE_ROWS = HEAD_DIM + 16
PASS0_GROUP = 8
PASS2_GROUP = 4
HALO = 16
GATE_SLOTS = 8
CONV_ROWS = 64
CONV_COLS = 128
MXU_COLS = 256

VMEM_LIMIT = 56 * 1024 * 1024

_BF = jnp.bfloat16
_F32 = jnp.float32
_NT = (((1,), (1,)), ((), ()))
_LOG2E = 1.4426950408889634


def _const_spec(shape):
    nd = len(shape)
    return pl.BlockSpec(shape, lambda *_: (0,) * nd, pipeline_mode=pl.Buffered(1))


def _sigmoid(x):
    return 1.0 / (1.0 + jnp.exp(-x))


def _log_sigmoid(x):
    return jnp.minimum(x, 0.0) - jnp.log(1.0 + jnp.exp(-jnp.abs(x)))


def _gelu(x):
    return 0.5 * x * (1.0 + lax.erf(x * (2.0 ** -0.5)))


def _mod_kernel(c_ref, w_ref, b_ref, o_ref):
    c = c_ref[...]
    ca = c * _sigmoid(c)
    o_ref[...] = jnp.dot(ca, w_ref[...], precision=lax.Precision.HIGHEST,
                         preferred_element_type=_F32) + b_ref[...]


def _mod_call(c, w_ada, b_ada):
    B, D = c.shape
    N = w_ada.shape[1]
    tn = 1536
    return pl.pallas_call(
        _mod_kernel,
        grid=(N // tn,),
        in_specs=[pl.BlockSpec((B, D), lambda j: (0, 0)),
                  pl.BlockSpec((D, tn), lambda j: (0, j)),
                  pl.BlockSpec((1, tn), lambda j: (0, j))],
        out_specs=pl.BlockSpec((B, tn), lambda j: (0, j)),
        out_shape=jax.ShapeDtypeStruct((B, N), _F32),
        name="mod",
    )(c, w_ada, b_ada.reshape(1, N))


def _lane_scan(x, op, ident, reverse):
    L = x.shape[1]
    lane = lax.broadcasted_iota(jnp.int32, x.shape, 1)
    d = 1
    while d < L:
        if reverse:
            shifted = jnp.where(lane < L - d, pltpu.roll(x, L - d, axis=1), ident)
        else:
            shifted = jnp.where(lane >= d, pltpu.roll(x, d, axis=1), ident)
        x = op(x, shifted)
        d *= 2
    return x


def _in_proj_kernel(x_ref, xp_ref, xn_ref, g_ref, sc_ref, sh_ref, w_ref, wg_ref,
                    bg_ref, cw_ref, cb_ref, z_ref, vt_ref, grow_ref, zq_s, zk_s, wvt_s, hb_s):
    tm = x_ref.shape[1]
    L = MLSTM_CHUNK
    i = pl.program_id(1)

    def norm_mod(x):
        ms = jnp.mean(x * x, axis=-1, keepdims=True)
        h = x * lax.rsqrt(ms + EPS) * g_ref[...]
        return (h * (1.0 + sc_ref[0]) + sh_ref[0]).astype(_BF)

    hb_s[HALO:HALO + tm, :] = norm_mod(x_ref[0])
    hb_s[:HALO, :] = jnp.where(i > 0, norm_mod(xp_ref[0]), jnp.zeros((HALO, D_MODEL), _BF))
    hb_s[HALO + tm:, :] = jnp.where(i < pl.num_programs(1) - 1, norm_mod(xn_ref[0]),
                                    jnp.zeros((HALO, D_MODEL), _BF))
    hb = hb_s.at[HALO:HALO + tm, :]

    def plain(c0):
        cols = slice(c0, c0 + MXU_COLS)
        wcols = slice(c0 + D_MODEL, c0 + D_MODEL + MXU_COLS)
        r = jnp.dot(hb[...], w_ref[:, wcols], preferred_element_type=_F32)
        if 3 * D_MODEL <= c0 < 5 * D_MODEL:
            r = _gelu(r)
        else:
            r = 1.0 / (1.0 + jnp.exp2(r * -_LOG2E))
        z_ref[0, :, cols] = r.astype(_BF)

    def conv_matmul(j, ze_s):
        for c0 in range(0, D_MODEL, MXU_COLS):
            ze_s[:, c0:c0 + MXU_COLS] = jnp.dot(
                hb_s[...], w_ref[:, j * D_MODEL + c0:j * D_MODEL + c0 + MXU_COLS],
                preferred_element_type=_F32)

    def conv_silu(j, ze_s, r0):
        n = CONV_ROWS + 16
        mid = slice(8, 8 + CONV_ROWS)
        for c0 in range(0, D_MODEL, CONV_COLS):
            cols = slice(j * D_MODEL + c0, j * D_MODEL + c0 + CONV_COLS)
            zb = ze_s[pl.ds(HALO - 8 + r0, n), c0:c0 + CONV_COLS]
            y = cb_ref[:, cols] + pltpu.roll(zb, 1, axis=0)[mid] * cw_ref[0:1, cols]
            y = y + zb[mid] * cw_ref[1:2, cols]
            y = y + pltpu.roll(zb, n - 1, axis=0)[mid] * cw_ref[2:3, cols]
            y = y / (1.0 + jnp.exp2(y * -_LOG2E))
            if j == 0:
                y = y * (HEAD_DIM ** -0.5)
            z_ref[0, pl.ds(r0, CONV_ROWS), cols] = y.astype(_BF)

    def gates_pre():
        gc = jnp.dot(hb[...], wg_ref[...], preferred_element_type=_F32) + bg_ref[...]
        return gc.T

    def gates_row(gt):
        nr = HEADS * GATE_SLOTS
        lst = _log_sigmoid(gt[:nr])
        git = gt[nr:2 * nr]
        slot = lax.broadcasted_iota(jnp.int32, (nr, L), 0) & (GATE_SLOTS - 1)
        fwd = slot < 3
        for c in range(tm // L):
            cols = slice(c * L, (c + 1) * L)
            lsc = lst[:, cols]
            cum = jnp.where(fwd, _lane_scan(lsc, jnp.add, 0.0, False),
                            _lane_scan(lsc, jnp.add, 0.0, True))
            base = jnp.where((slot == 0) | (slot == 3), cum, git[:, cols] - cum)
            cmax = jnp.where(fwd, _lane_scan(base, jnp.maximum, -jnp.inf, False),
                             _lane_scan(base, jnp.maximum, -jnp.inf, True))
            grow_ref[0, :, cols] = jnp.where((slot == 2) | (slot == 5), cmax, base)

    @pl.when((pl.program_id(0) == 0) & (i == 0))
    def _():
        wvt_s[...] = w_ref[:, 2 * D_MODEL:3 * D_MODEL].astype(_F32).T.astype(_BF)

    plain_blocks = iter(range(2 * D_MODEL, Z_COLS, MXU_COLS))
    gt = gates_pre()
    conv_matmul(0, zq_s)
    conv_matmul(1, zk_s)
    for j, ze_s in enumerate((zq_s, zk_s)):
        for r0 in range(0, tm, CONV_ROWS):
            conv_silu(j, ze_s, r0)
            plain(next(plain_blocks))
    gates_row(gt)
    for c0 in plain_blocks:
        plain(c0)
    vt_ref[0] = lax.dot_general(wvt_s[...], hb[...], _NT,
                                preferred_element_type=_F32).astype(_BF)


def _in_proj_call(x, norm_g, sc, sh, w_bf, wg_bf, bias_g, conv_w, conv_b, tm):
    B, S, D = x.shape
    r = tm // HALO
    last = S // HALO - 1
    return pl.pallas_call(
        _in_proj_kernel,
        grid=(B, S // tm),
        in_specs=[pl.BlockSpec((1, tm, D), lambda b, i: (b, i, 0)),
                  pl.BlockSpec((1, HALO, D), lambda b, i: (b, jnp.maximum(i * r - 1, 0), 0)),
                  pl.BlockSpec((1, HALO, D),
                               lambda b, i: (b, jnp.minimum((i + 1) * r, last), 0)),
                  _const_spec((1, D)),
                  pl.BlockSpec((1, 1, D), lambda b, i: (b, 0, 0)),
                  pl.BlockSpec((1, 1, D), lambda b, i: (b, 0, 0)),
                  _const_spec((D, Z_COLS + D)),
                  _const_spec((D, 128)),
                  _const_spec((1, 128)),
                  _const_spec((3, 2 * D)),
                  _const_spec((1, 2 * D))],
        out_specs=[pl.BlockSpec((1, tm, Z_COLS), lambda b, i: (b, i, 0)),
                   pl.BlockSpec((1, D, tm), lambda b, i: (b, 0, i)),
                   pl.BlockSpec((1, HEADS * GATE_SLOTS, tm), lambda b, i: (b, 0, i))],
        out_shape=[jax.ShapeDtypeStruct((B, S, Z_COLS), _BF),
                   jax.ShapeDtypeStruct((B, D, S), _BF),
                   jax.ShapeDtypeStruct((B, HEADS * GATE_SLOTS, S), _F32)],
        scratch_shapes=[pltpu.VMEM((tm + 2 * HALO, D), _F32),
                        pltpu.VMEM((tm + 2 * HALO, D), _F32),
                        pltpu.VMEM((D, D), _BF),
                        pltpu.VMEM((tm + 2 * HALO, D), _BF)],
        compiler_params=pltpu.CompilerParams(
            dimension_semantics=("arbitrary", "arbitrary"),
            vmem_limit_bytes=VMEM_LIMIT),
        name="in_proj",
    )(x, x, x, norm_g, sc, sh, w_bf, wg_bf, bias_g, conv_w, conv_b)


def _mlstm_kernel(q_ref, k_ref, vt_ref, zo_ref, gr_ref, mhg_ref,
                  y_ref,
                  pvf_s, pvb_s, u_s, snap_s, ct_s, mh_s, dec_s):
    S = q_ref.shape[1]
    L = MLSTM_CHUNK
    NC = S // L
    dh = HEAD_DIM
    R = STATE_ROWS
    K0 = (0, 3)
    LAST = (L - 1, 0)

    def gate_rows(gr, d):
        k0 = K0[d]
        return gr[k0:k0 + 1], gr[k0 + 1:k0 + 2], gr[k0 + 2:k0 + 3]

    m = [jnp.zeros((1, 1), _F32), jnp.zeros((1, 1), _F32)]
    for i in range(NC):
        for d, c in enumerate((i, NC - 1 - i)):
            b_row, _, cmax_row = gate_rows(gr_ref[0, 0, :, c * L:(c + 1) * L], d)
            a_last = jnp.maximum(m[d], cmax_row[:, LAST[d]:LAST[d] + 1])
            mh_s[d * NC + c:d * NC + c + 1, :] = jnp.broadcast_to(m[d], (1, 128))
            dec_s[d * NC + c:d * NC + c + 1, :] = jnp.broadcast_to(jnp.exp(m[d] - a_last),
                                                                  (1, 128))
            m[d] = b_row[:, LAST[d]:LAST[d] + 1] + a_last

    si = lax.broadcasted_iota(jnp.int32, (L, L), 0)
    ti = lax.broadcasted_iota(jnp.int32, (L, L), 1)
    masks = (si <= ti, si >= ti)
    ones_row = (lax.broadcasted_iota(jnp.int32, (R - dh, L), 0) == 0).astype(_BF)

    def scores(c):
        t0 = pl.multiple_of(c * L, L)
        return lax.dot_general(k_ref[0, pl.ds(t0, L), :], q_ref[0, pl.ds(t0, L), :], _NT,
                               preferred_element_type=_F32)

    def pass0(g, carry):
        s_next = scores(g * PASS0_GROUP)
        for k in range(PASS0_GROUP):
            c = g * PASS0_GROUP + k
            t0 = pl.multiple_of(c * L, L)
            s_t = s_next
            vta = jnp.concatenate([vt_ref[0, :, pl.ds(t0, L)], ones_row], axis=0)
            vta32 = vta.astype(_F32)
            gr = gr_ref[0, 0, :, pl.ds(t0, L)]
            a_rows, vtw = [], []
            for d in range(2):
                _, r_row, cmax_row = gate_rows(gr, d)
                a_rows.append(jnp.maximum(mh_s[pl.ds(d * NC + c, 1), 0:1], cmax_row))
                w_row = jnp.exp(r_row - a_rows[d][:, LAST[d]:LAST[d] + 1])
                vtw.append((vta32 * w_row).astype(_BF))
            u_s[c] = jnp.dot(jnp.concatenate(vtw, axis=0), k_ref[0, pl.ds(t0, L), :],
                             preferred_element_type=_F32)
            if k + 1 < PASS0_GROUP:
                s_next = scores(c + 1)
            p = []
            for d in range(2):
                r_row = gate_rows(gr, d)[1]
                r_rep = jnp.broadcast_to(r_row, (128, L)).T
                r_col = jnp.concatenate([r_rep] * (L // 128), axis=1)
                p.append((jnp.exp(jnp.where(masks[d], r_col - a_rows[d], -jnp.inf))
                          * s_t).astype(_BF))
            pv = jnp.dot(vta, jnp.concatenate(p, axis=1), preferred_element_type=_F32)
            pvf_s[:, pl.ds(t0, L)] = pv[:, :L]
            pvb_s[:, pl.ds(t0, L)] = pv[:, L:]
        return carry

    lax.fori_loop(0, NC // PASS0_GROUP, pass0, 0)

    ct_s[...] = jnp.zeros_like(ct_s)

    def pass1(i, carry):
        for d, c in enumerate((i, NC - 1 - i)):
            rows = slice(d * R, (d + 1) * R)
            ct = ct_s[rows, :]
            snap_s[c, rows, :] = ct.astype(_BF)
            ct_s[rows, :] = dec_s[pl.ds(d * NC + c, 1), 0:1] * ct + u_s[c, rows, :]
        return carry

    lax.fori_loop(0, NC, pass1, 0, unroll=2)

    def inter_term(c):
        t0 = pl.multiple_of(c * L, L)
        return lax.dot_general(snap_s[c], q_ref[0, pl.ds(t0, L), :], _NT,
                               preferred_element_type=_F32)

    def pass2(g, carry):
        inter_next = inter_term(g * PASS2_GROUP)
        for k in range(PASS2_GROUP):
            c = g * PASS2_GROUP + k
            t0 = pl.multiple_of(c * L, L)
            inter = inter_next
            if k + 1 < PASS2_GROUP:
                inter_next = inter_term(c + 1)
            gr = gr_ref[0, 0, :, pl.ds(t0, L)]
            h_t = None
            for d, pv_s in enumerate((pvf_s, pvb_s)):
                b_row, _, cmax_row = gate_rows(gr, d)
                m_c = mh_s[pl.ds(d * NC + c, 1), 0:1]
                a_row = jnp.maximum(m_c, cmax_row)
                num_t = (jnp.exp(m_c - a_row) * inter[d * R:(d + 1) * R]
                         + pv_s[:, pl.ds(t0, L)])
                den = num_t[dh:dh + 1, :]
                scale = 1.0 / jnp.maximum(jnp.abs(den), jnp.exp(-(b_row + a_row)))
                h_d = num_t[:dh, :] * scale
                h_t = h_d if h_t is None else h_t + h_d
            h = h_t.T
            hn = h * lax.rsqrt(jnp.mean(h * h, axis=-1, keepdims=True) + EPS) * mhg_ref[...]
            o = zo_ref[0, pl.ds(t0, L), :].astype(_F32)
            y_ref[0, pl.ds(t0, L), :] = (hn * o).astype(_BF)
        return carry

    lax.fori_loop(0, NC // PASS2_GROUP, pass2, 0)


def _mlstm_call(z, vt, grow, mh_g):
    B, S, _ = z.shape
    dh = HEAD_DIM
    H = HEADS
    nc = S // MLSTM_CHUNK

    def zspec(off):
        return pl.BlockSpec((1, S, dh), lambda b, h, off=off: (b, 0, off + h))

    return pl.pallas_call(
        _mlstm_kernel,
        grid=(B, H),
        in_specs=[zspec(0), zspec(H),
                  pl.BlockSpec((1, dh, S), lambda b, h: (b, h, 0)),
                  zspec(2 * H),
                  pl.BlockSpec((1, 1, 8, S), lambda b, h: (b, h, 0, 0)),
                  pl.BlockSpec((1, dh), lambda b, h: (0, h))],
        out_specs=pl.BlockSpec((1, S, dh), lambda b, h: (b, 0, h)),
        out_shape=jax.ShapeDtypeStruct((B, S, H * dh), _BF),
        scratch_shapes=[pltpu.VMEM((STATE_ROWS, S), _F32), pltpu.VMEM((STATE_ROWS, S), _F32),
                        pltpu.VMEM((nc, 2 * STATE_ROWS, dh), _F32),
                        pltpu.VMEM((nc, 2 * STATE_ROWS, dh), _BF),
                        pltpu.VMEM((2 * STATE_ROWS, dh), _F32),
                        pltpu.VMEM((2 * nc, 128), _F32), pltpu.VMEM((2 * nc, 128), _F32)],
        compiler_params=pltpu.CompilerParams(
            dimension_semantics=("arbitrary", "arbitrary"),
            vmem_limit_bytes=VMEM_LIMIT),
        name="mlstm",
    )(z, z, vt, z, grow, mh_g)


def _mix_kernel(x_ref, zu_ref, zv_ref, zga_ref, zgb_ref, ya_ref, g1_ref, lng_ref, lnb_ref,
                ws_ref, bs_ref, wo_ref, o_ref, mix_s):
    tm = x_ref.shape[1]
    P = SGU_CHUNK
    C = SGU_GROUP_DIM
    for j in range(tm // P):
        rows = slice(j * P, (j + 1) * P)
        gv = zv_ref[0, rows, :].astype(_F32)
        mu = jnp.mean(gv, axis=-1, keepdims=True)
        dv = gv - mu
        var = jnp.mean(dv * dv, axis=-1, keepdims=True)
        vn = (dv * lax.rsqrt(var + EPS) * lng_ref[...] + lnb_ref[...]).astype(_BF)
        s = jnp.concatenate(
            [jnp.dot(ws_ref[g], vn[:, g * C:(g + 1) * C], preferred_element_type=_F32)
             for g in range(SGU_GROUPS)], axis=1) + bs_ref[...]
        yb = zu_ref[0, rows, :].astype(_F32) * s
        ga = zga_ref[0, rows, :].astype(_F32)
        gb = zgb_ref[0, rows, :].astype(_F32)
        mix_s[rows, :] = (ga * ya_ref[0, rows, :].astype(_F32) + gb * yb).astype(_BF)
    upd = jnp.dot(mix_s[...], wo_ref[...], preferred_element_type=_F32)
    o_ref[0] = x_ref[0] + g1_ref[0] * upd


def _mix_call(x, z, ya, g1, ln_g, ln_b, ws_bf, bs_full, wo_bf, tm):
    B, S, D = x.shape

    def zspec(j):
        return pl.BlockSpec((1, tm, D), lambda b, i, j=j: (b, i, j))

    return pl.pallas_call(
        _mix_kernel,
        grid=(B, S // tm),
        in_specs=[zspec(0), zspec(3), zspec(4), zspec(5), zspec(6), zspec(0),
                  pl.BlockSpec((1, 1, D), lambda b, i: (b, 0, 0)),
                  _const_spec((1, D)), _const_spec((1, D)),
                  _const_spec((SGU_GROUPS, SGU_CHUNK, SGU_CHUNK)),
                  _const_spec((SGU_CHUNK, D)),
                  _const_spec((D, D))],
        out_specs=pl.BlockSpec((1, tm, D), lambda b, i: (b, i, 0)),
        out_shape=jax.ShapeDtypeStruct((B, S, D), _F32),
        scratch_shapes=[pltpu.VMEM((tm, D), _BF)],
        compiler_params=pltpu.CompilerParams(
            dimension_semantics=("arbitrary", "arbitrary"),
            vmem_limit_bytes=VMEM_LIMIT),
        name="mix",
    )(x, z, z, z, z, ya, g1, ln_g, ln_b, ws_bf, bs_full, wo_bf)


def _ffn_kernel(x_ref, g_ref, sc_ref, sh_ref, g2_ref, w1_ref, w2_ref, nf_ref, o_ref):
    x = x_ref[0]
    ms = jnp.mean(x * x, axis=-1, keepdims=True)
    h = x * lax.rsqrt(ms + EPS) * g_ref[...]
    hb = (h * (1.0 + sc_ref[0]) + sh_ref[0]).astype(_BF)
    ff = jnp.zeros(x.shape, _F32)
    for j in range(FF_DIM // D_MODEL):
        cols = slice(j * D_MODEL, (j + 1) * D_MODEL)
        a = jnp.maximum(jnp.dot(hb, w1_ref[:, cols], preferred_element_type=_F32), 0.0)
        ff = ff + jnp.dot((a * a).astype(_BF), w2_ref[cols, :], preferred_element_type=_F32)
    x2 = x + g2_ref[0] * ff
    ms2 = jnp.mean(x2 * x2, axis=-1, keepdims=True)
    o_ref[0] = x2 * lax.rsqrt(ms2 + EPS) * nf_ref[...]


def _ffn_call(x1, norm_g, sc, sh, g2, w1_bf, w2_bf, normf_g, tm):
    B, S, D = x1.shape
    vec = pl.BlockSpec((1, 1, D), lambda b, i: (b, 0, 0))
    return pl.pallas_call(
        _ffn_kernel,
        grid=(B, S // tm),
        in_specs=[pl.BlockSpec((1, tm, D), lambda b, i: (b, i, 0)),
                  _const_spec((1, D)), vec, vec, vec,
                  _const_spec((D, FF_DIM)), _const_spec((FF_DIM, D)),
                  _const_spec((1, D))],
        out_specs=pl.BlockSpec((1, tm, D), lambda b, i: (b, i, 0)),
        out_shape=jax.ShapeDtypeStruct((B, S, D), _F32),
        compiler_params=pltpu.CompilerParams(
            dimension_semantics=("arbitrary", "arbitrary"),
            vmem_limit_bytes=VMEM_LIMIT),
        name="ffn",
    )(x1, norm_g, sc, sh, g2, w1_bf, w2_bf, normf_g)


def _gate_weights(w_if, b_if):
    D = w_if.shape[0]
    H = HEADS
    zero = 4 * H
    idx_f, idx_i = [], []
    for h in range(H):
        idx_f += [H + h] * 3 + [3 * H + h] * 3 + [zero] * 2
        idx_i += [zero, h, h, zero, 2 * H + h, 2 * H + h, zero, zero]
    idx = jnp.array(idx_f + idx_i + [zero] * (128 - 2 * H * GATE_SLOTS), jnp.int32)
    w_ext = jnp.concatenate([w_if, jnp.zeros((D, 1), w_if.dtype)], axis=1)
    b_ext = jnp.concatenate([b_if.reshape(-1), jnp.zeros((1,), b_if.dtype)])
    return jnp.take(w_ext, idx, axis=1), jnp.take(b_ext, idx).reshape(1, 128)


def kernel(x, c, w_ada, b_ada, norm1_g, norm2_g, w_in, b_if, conv_w, conv_b, mh_g,
           ln_v_g, ln_v_b, w_s, b_s, w_out, w1, w2, normf_g):
    B, S, D = x.shape
    H = HEADS
    assert w_ada.shape[0] == 1, "single layer"

    mod = _mod_call(c, w_ada[0], b_ada[0])
    sh1, sc1, g1, sh2, sc2, g2 = (mod[:, k * D:(k + 1) * D].reshape(B, 1, D) for k in range(6))

    w_in0 = w_in[0]
    w_z = w_in0[:, :8 * D].astype(_BF)
    w_g, b_g = _gate_weights(w_in0[:, 8 * D:], b_if[0])
    z, vt, grow = _in_proj_call(x, norm1_g, sc1, sh1, w_z, w_g.astype(_BF), b_g,
                                conv_w[0], conv_b, tm=512)

    ya = _mlstm_call(z, vt, grow.reshape(B, H, GATE_SLOTS, S), mh_g)

    bs_full = jnp.repeat(b_s[0].T, SGU_GROUP_DIM, axis=1)
    x1 = _mix_call(x, z, ya, g1, ln_v_g, ln_v_b, w_s[0].astype(_BF), bs_full,
                   w_out[0].astype(_BF), tm=512)

    return _ffn_call(x1, norm2_g, sc2, sh2, g2, w1[0].astype(_BF), w2[0].astype(_BF),
                     normf_g.reshape(1, D), tm=1024)
```

```python
import jax
import jax.numpy as jnp
import numpy as np
from jax import lax
from jax.experimental import pallas as pl
from jax.experimental.pallas import tpu as pltpu

D_MODEL = 1024
HEADS = 4
HEAD_DIM = 256
SGU_GROUPS = 8
SGU_GROUP_DIM = 128
SGU_CHUNK = 128
FF_DIM = 4096
EPS = 1e-6
Z_COLS = 7 * D_MODEL

MLSTM_CHUNK = 256
STATE_ROWS = HEAD_DIM + 16
STATE_BLOCK = 64
PASS0_GROUP = 8
PASS2_GROUP = 4
HALO = 16
GATE_SLOTS = 8
CONV_ROWS = 64
CONV_COLS = 128
MXU_COLS = 256

VMEM_LIMIT = 56 * 1024 * 1024

_BF = jnp.bfloat16
_F32 = jnp.float32
_NT = (((1,), (1,)), ((), ()))
_LOG2E = 1.4426950408889634


def _const_spec(shape):
    nd = len(shape)
    return pl.BlockSpec(shape, lambda *_: (0,) * nd, pipeline_mode=pl.Buffered(1))


def _sigmoid(x):
    return 1.0 / (1.0 + jnp.exp(-x))


def _log_sigmoid(x):
    return jnp.minimum(x, 0.0) - jnp.log(1.0 + jnp.exp(-jnp.abs(x)))


def _gelu(x):
    return 0.5 * x * (1.0 + lax.erf(x * (2.0 ** -0.5)))


def _mod_kernel(c_ref, w_ref, b_ref, o_ref):
    c = c_ref[...]
    ca = c * _sigmoid(c)
    o_ref[...] = jnp.dot(ca, w_ref[...], precision=lax.Precision.HIGHEST,
                         preferred_element_type=_F32) + b_ref[...]


def _mod_call(c, w_ada, b_ada):
    B, D = c.shape
    N = w_ada.shape[1]
    tn = 1536
    return pl.pallas_call(
        _mod_kernel,
        grid=(N // tn,),
        in_specs=[pl.BlockSpec((B, D), lambda j: (0, 0)),
                  pl.BlockSpec((D, tn), lambda j: (0, j)),
                  pl.BlockSpec((1, tn), lambda j: (0, j))],
        out_specs=pl.BlockSpec((B, tn), lambda j: (0, j)),
        out_shape=jax.ShapeDtypeStruct((B, N), _F32),
        name="mod",
    )(c, w_ada, b_ada.reshape(1, N))


def _lane_scan(x, op, ident, reverse):
    L = x.shape[1]
    lane = lax.broadcasted_iota(jnp.int32, x.shape, 1)
    d = 1
    while d < L:
        if reverse:
            shifted = jnp.where(lane < L - d, pltpu.roll(x, L - d, axis=1), ident)
        else:
            shifted = jnp.where(lane >= d, pltpu.roll(x, d, axis=1), ident)
        x = op(x, shifted)
        d *= 2
    return x


def _in_proj_kernel(x_ref, xp_ref, xn_ref, g_ref, sc_ref, sh_ref, w_ref, wg_ref,
                    bg_ref, cw_ref, cb_ref, z_ref, vt_ref, grow_ref, zq_s, zk_s, wvt_s, hb_s):
    tm = x_ref.shape[1]
    L = MLSTM_CHUNK
    i = pl.program_id(1)

    gain = g_ref[...] * (1.0 + sc_ref[0])

    def norm_mod(x):
        ms = jnp.mean(x * x, axis=-1, keepdims=True)
        return (x * lax.rsqrt(ms + EPS) * gain + sh_ref[0]).astype(_BF)

    hb_s[HALO:HALO + tm, :] = norm_mod(x_ref[0])
    hb_s[:HALO, :] = jnp.where(i > 0, norm_mod(xp_ref[0]), jnp.zeros((HALO, D_MODEL), _BF))
    hb_s[HALO + tm:, :] = jnp.where(i < pl.num_programs(1) - 1, norm_mod(xn_ref[0]),
                                    jnp.zeros((HALO, D_MODEL), _BF))
    hb = hb_s.at[HALO:HALO + tm, :]

    def plain(c0):
        cols = slice(c0, c0 + MXU_COLS)
        wcols = slice(c0 + D_MODEL, c0 + D_MODEL + MXU_COLS)
        r = jnp.dot(hb[...], w_ref[:, wcols], preferred_element_type=_F32)
        if 3 * D_MODEL <= c0 < 5 * D_MODEL:
            r = _gelu(r)
        else:
            r = 1.0 / (1.0 + jnp.exp2(r * -_LOG2E))
        z_ref[0, :, cols] = r.astype(_BF)

    def conv_matmul(j, ze_s):
        for c0 in range(0, D_MODEL, MXU_COLS):
            ze_s[:, c0:c0 + MXU_COLS] = jnp.dot(
                hb_s[...], w_ref[:, j * D_MODEL + c0:j * D_MODEL + c0 + MXU_COLS],
                preferred_element_type=_F32)

    def conv_silu(j, ze_s, r0):
        n = CONV_ROWS + 16
        mid = slice(8, 8 + CONV_ROWS)
        for c0 in range(0, D_MODEL, CONV_COLS):
            cols = slice(j * D_MODEL + c0, j * D_MODEL + c0 + CONV_COLS)
            zb = ze_s[pl.ds(HALO - 8 + r0, n), c0:c0 + CONV_COLS]
            y = cb_ref[:, cols] + pltpu.roll(zb, 1, axis=0)[mid] * cw_ref[0:1, cols]
            y = y + zb[mid] * cw_ref[1:2, cols]
            y = y + pltpu.roll(zb, n - 1, axis=0)[mid] * cw_ref[2:3, cols]
            y = y / (1.0 + jnp.exp2(y * -_LOG2E))
            if j == 0:
                y = y * (HEAD_DIM ** -0.5)
            z_ref[0, pl.ds(r0, CONV_ROWS), cols] = y.astype(_BF)

    def gates_pre():
        gc = jnp.dot(hb[...], wg_ref[...], preferred_element_type=_F32) + bg_ref[...]
        return gc.T

    def gates_row(gt):
        nr = HEADS * GATE_SLOTS
        lst = _log_sigmoid(gt[:nr])
        git = gt[nr:2 * nr]
        slot = lax.broadcasted_iota(jnp.int32, (nr, L), 0) & (GATE_SLOTS - 1)
        fwd = slot < 3
        for c in range(tm // L):
            cols = slice(c * L, (c + 1) * L)
            lsc = lst[:, cols]
            cum = jnp.where(fwd, _lane_scan(lsc, jnp.add, 0.0, False),
                            _lane_scan(lsc, jnp.add, 0.0, True))
            base = jnp.where((slot == 0) | (slot == 3), cum, git[:, cols] - cum)
            cmax = jnp.where(fwd, _lane_scan(base, jnp.maximum, -jnp.inf, False),
                             _lane_scan(base, jnp.maximum, -jnp.inf, True))
            grow_ref[0, :, cols] = jnp.where((slot == 2) | (slot == 5), cmax, base)

    @pl.when((pl.program_id(0) == 0) & (i == 0))
    def _():
        wvt_s[...] = w_ref[:, 2 * D_MODEL:3 * D_MODEL].astype(_F32).T.astype(_BF)

    plain_blocks = iter(range(2 * D_MODEL, Z_COLS, MXU_COLS))
    gt = gates_pre()
    conv_matmul(0, zq_s)
    conv_matmul(1, zk_s)
    for j, ze_s in enumerate((zq_s, zk_s)):
        for r0 in range(0, tm, CONV_ROWS):
            conv_silu(j, ze_s, r0)
            plain(next(plain_blocks))
    gates_row(gt)
    for c0 in plain_blocks:
        plain(c0)
    vt_ref[0] = lax.dot_general(wvt_s[...], hb[...], _NT,
                                preferred_element_type=_F32).astype(_BF)


def _in_proj_call(x, norm_g, sc, sh, w_bf, wg_bf, bias_g, conv_w, conv_b, tm):
    B, S, D = x.shape
    r = tm // HALO
    last = S // HALO - 1
    return pl.pallas_call(
        _in_proj_kernel,
        grid=(B, S // tm),
        in_specs=[pl.BlockSpec((1, tm, D), lambda b, i: (b, i, 0)),
                  pl.BlockSpec((1, HALO, D), lambda b, i: (b, jnp.maximum(i * r - 1, 0), 0)),
                  pl.BlockSpec((1, HALO, D),
                               lambda b, i: (b, jnp.minimum((i + 1) * r, last), 0)),
                  _const_spec((1, D)),
                  pl.BlockSpec((1, 1, D), lambda b, i: (b, 0, 0)),
                  pl.BlockSpec((1, 1, D), lambda b, i: (b, 0, 0)),
                  _const_spec(w_bf.shape),
                  _const_spec((D, 128)),
                  _const_spec((1, 128)),
                  _const_spec((3, 2 * D)),
                  _const_spec((1, 2 * D))],
        out_specs=[pl.BlockSpec((1, tm, Z_COLS), lambda b, i: (b, i, 0)),
                   pl.BlockSpec((1, D, tm), lambda b, i: (b, 0, i)),
                   pl.BlockSpec((1, HEADS * GATE_SLOTS, tm), lambda b, i: (b, 0, i))],
        out_shape=[jax.ShapeDtypeStruct((B, S, Z_COLS), _BF),
                   jax.ShapeDtypeStruct((B, D, S), _BF),
                   jax.ShapeDtypeStruct((B, HEADS * GATE_SLOTS, S), _F32)],
        scratch_shapes=[pltpu.VMEM((tm + 2 * HALO, D), _F32),
                        pltpu.VMEM((tm + 2 * HALO, D), _F32),
                        pltpu.VMEM((D, D), _BF),
                        pltpu.VMEM((tm + 2 * HALO, D), _BF)],
        compiler_params=pltpu.CompilerParams(
            dimension_semantics=("arbitrary", "arbitrary"),
            vmem_limit_bytes=VMEM_LIMIT),
        name="in_proj",
    )(x, x, x, norm_g, sc, sh, w_bf, wg_bf, bias_g, conv_w, conv_b)


def _mlstm_kernel(q_ref, k_ref, vt_ref, zo_ref, gr_ref, mhg_ref,
                  y_ref,
                  pvf_s, pvb_s, u_s, snap_s, ct_s, mh_s, dec_s, ht_s):
    S = q_ref.shape[1]
    L = MLSTM_CHUNK
    NC = S // L
    dh = HEAD_DIM
    R = STATE_ROWS
    K0 = (0, 3)
    LAST = (L - 1, 0)

    def gate_rows(gr, d):
        k0 = K0[d]
        return gr[k0:k0 + 1], gr[k0 + 1:k0 + 2], gr[k0 + 2:k0 + 3]

    m = [jnp.zeros((1, 1), _F32), jnp.zeros((1, 1), _F32)]
    for i in range(NC):
        for d, c in enumerate((i, NC - 1 - i)):
            b_row, _, cmax_row = gate_rows(gr_ref[0, 0, :, c * L:(c + 1) * L], d)
            a_last = jnp.maximum(m[d], cmax_row[:, LAST[d]:LAST[d] + 1])
            mh_s[d * NC + c:d * NC + c + 1, :] = jnp.broadcast_to(m[d], (1, 128))
            dec_s[d * NC + c:d * NC + c + 1, :] = jnp.broadcast_to(jnp.exp(m[d] - a_last),
                                                                  (1, 128))
            m[d] = b_row[:, LAST[d]:LAST[d] + 1] + a_last

    si = lax.broadcasted_iota(jnp.int32, (L, L), 0)
    ti = lax.broadcasted_iota(jnp.int32, (L, L), 1)
    masks = (si <= ti, si >= ti)
    ones_row = (lax.broadcasted_iota(jnp.int32, (R - dh, L), 0) == 0).astype(_BF)

    def scores(c):
        t0 = pl.multiple_of(c * L, L)
        return lax.dot_general(k_ref[0, pl.ds(t0, L), :], q_ref[0, pl.ds(t0, L), :], _NT,
                               preferred_element_type=_F32)

    def pass0(g, carry):
        s_next = scores(g * PASS0_GROUP)
        for k in range(PASS0_GROUP):
            c = g * PASS0_GROUP + k
            t0 = pl.multiple_of(c * L, L)
            s_t = s_next
            vta = jnp.concatenate([vt_ref[0, :, pl.ds(t0, L)], ones_row], axis=0)
            vta32 = vta.astype(_F32)
            gr = gr_ref[0, 0, :, pl.ds(t0, L)]
            a_rows, vtw = [], []
            for d in range(2):
                _, r_row, cmax_row = gate_rows(gr, d)
                a_rows.append(jnp.maximum(mh_s[pl.ds(d * NC + c, 1), 0:1], cmax_row))
                w_row = jnp.exp(r_row - a_rows[d][:, LAST[d]:LAST[d] + 1])
                vtw.append((vta32 * w_row).astype(_BF))
            u_s[c] = jnp.dot(jnp.concatenate(vtw, axis=0), k_ref[0, pl.ds(t0, L), :],
                             preferred_element_type=_F32).astype(_BF)
            if k + 1 < PASS0_GROUP:
                s_next = scores(c + 1)
            p = []
            for d in range(2):
                r_row = gate_rows(gr, d)[1]
                r_rep = jnp.broadcast_to(r_row, (128, L)).T
                r_col = jnp.concatenate([r_rep] * (L // 128), axis=1)
                p.append((jnp.exp(jnp.where(masks[d], r_col - a_rows[d], -jnp.inf))
                          * s_t).astype(_BF))
            pv = jnp.dot(vta, jnp.concatenate(p, axis=1), preferred_element_type=_F32)
            pvf_s[:, pl.ds(t0, L)] = pv[:, :L]
            pvb_s[:, pl.ds(t0, L)] = pv[:, L:]
        return carry

    lax.fori_loop(0, NC // PASS0_GROUP, pass0, 0)

    ct_s[...] = jnp.zeros_like(ct_s)

    def pass1(i, carry):
        for d, c in enumerate((i, NC - 1 - i)):
            decay = dec_s[pl.ds(d * NC + c, 1), 0:1]
            for r0 in range(d * R, (d + 1) * R, STATE_BLOCK):
                rows = slice(r0, min(r0 + STATE_BLOCK, (d + 1) * R))
                ct = ct_s[rows, :]
                snap_s[c, rows, :] = ct.astype(_BF)
                ct_s[rows, :] = decay * ct + u_s[c, rows, :].astype(_F32)
        return carry

    lax.fori_loop(0, NC, pass1, 0, unroll=2)

    def inter_term(c):
        t0 = pl.multiple_of(c * L, L)
        return lax.dot_general(snap_s[c], q_ref[0, pl.ds(t0, L), :], _NT,
                               preferred_element_type=_F32)

    def pass2(g, carry):
        inter_next = inter_term(g * PASS2_GROUP)
        for k in range(PASS2_GROUP):
            c = g * PASS2_GROUP + k
            t0 = pl.multiple_of(c * L, L)
            inter = inter_next
            if k + 1 < PASS2_GROUP:
                inter_next = inter_term(c + 1)
            gr = gr_ref[0, 0, :, pl.ds(t0, L)]
            w_inter, scale = [], []
            for d, pv_s in enumerate((pvf_s, pvb_s)):
                b_row, _, cmax_row = gate_rows(gr, d)
                m_c = mh_s[pl.ds(d * NC + c, 1), 0:1]
                a_row = jnp.maximum(m_c, cmax_row)
                w_inter.append(jnp.exp(m_c - a_row))
                den = (w_inter[d] * inter[d * R + dh:d * R + dh + 1]
                       + pv_s[dh:dh + 1, pl.ds(t0, L)])
                scale.append(1.0 / jnp.maximum(jnp.abs(den), jnp.exp(-(b_row + a_row))))
            ss = jnp.zeros((1, L), _F32)
            for e0 in range(0, dh, STATE_BLOCK):
                blk = None
                for d, pv_s in enumerate((pvf_s, pvb_s)):
                    num = (w_inter[d] * inter[d * R + e0:d * R + e0 + STATE_BLOCK]
                           + pv_s[e0:e0 + STATE_BLOCK, pl.ds(t0, L)])
                    blk = num * scale[d] if blk is None else blk + num * scale[d]
                ss = ss + jnp.sum(blk * blk, axis=0, keepdims=True)
                ht_s[e0:e0 + STATE_BLOCK, :] = blk
            rs = lax.rsqrt(ss * (1.0 / dh) + EPS)
            for t1 in range(0, L, 128):
                h = (ht_s[:, t1:t1 + 128] * rs[:, t1:t1 + 128]).T
                tq = pl.multiple_of(t0 + t1, 128)
                o = zo_ref[0, pl.ds(tq, 128), :].astype(_F32)
                y_ref[0, pl.ds(tq, 128), :] = (h * mhg_ref[...] * o).astype(_BF)
        return carry

    lax.fori_loop(0, NC // PASS2_GROUP, pass2, 0)


def _mlstm_call(z, vt, grow, mh_g):
    B, S, _ = z.shape
    dh = HEAD_DIM
    H = HEADS
    nc = S // MLSTM_CHUNK

    def zspec(off):
        return pl.BlockSpec((1, S, dh), lambda b, h, off=off: (b, 0, off + h))

    return pl.pallas_call(
        _mlstm_kernel,
        grid=(B, H),
        in_specs=[zspec(0), zspec(H),
                  pl.BlockSpec((1, dh, S), lambda b, h: (b, h, 0)),
                  zspec(2 * H),
                  pl.BlockSpec((1, 1, 8, S), lambda b, h: (b, h, 0, 0)),
                  pl.BlockSpec((1, dh), lambda b, h: (0, h))],
        out_specs=pl.BlockSpec((1, S, dh), lambda b, h: (b, 0, h)),
        out_shape=jax.ShapeDtypeStruct((B, S, H * dh), _BF),
        scratch_shapes=[pltpu.VMEM((STATE_ROWS, S), _F32), pltpu.VMEM((STATE_ROWS, S), _F32),
                        pltpu.VMEM((nc, 2 * STATE_ROWS, dh), _BF),
                        pltpu.VMEM((nc, 2 * STATE_ROWS, dh), _BF),
                        pltpu.VMEM((2 * STATE_ROWS, dh), _F32),
                        pltpu.VMEM((2 * nc, 128), _F32), pltpu.VMEM((2 * nc, 128), _F32),
                        pltpu.VMEM((dh, MLSTM_CHUNK), _F32)],
        compiler_params=pltpu.CompilerParams(
            dimension_semantics=("arbitrary", "arbitrary"),
            vmem_limit_bytes=VMEM_LIMIT),
        name="mlstm",
    )(z, z, vt, z, grow, mh_g)


def _mix_kernel(x_ref, zu_ref, zv_ref, zga_ref, zgb_ref, ya_ref, g1_ref, lng_ref, lnb_ref,
                ws_ref, bs_ref, wo_ref, o_ref, mix_s):
    tm = x_ref.shape[1]
    P = SGU_CHUNK
    C = SGU_GROUP_DIM
    for j in range(tm // P):
        rows = slice(j * P, (j + 1) * P)
        gv = zv_ref[0, rows, :].astype(_F32)
        mu = jnp.mean(gv, axis=-1, keepdims=True)
        dv = gv - mu
        var = jnp.mean(dv * dv, axis=-1, keepdims=True)
        vn = (dv * lax.rsqrt(var + EPS) * lng_ref[...] + lnb_ref[...]).astype(_BF)
        s = jnp.concatenate(
            [jnp.dot(ws_ref[g], vn[:, g * C:(g + 1) * C], preferred_element_type=_F32)
             for g in range(SGU_GROUPS)], axis=1) + bs_ref[...]
        yb = zu_ref[0, rows, :].astype(_F32) * s
        ga = zga_ref[0, rows, :].astype(_F32)
        gb = zgb_ref[0, rows, :].astype(_F32)
        mix_s[rows, :] = (ga * ya_ref[0, rows, :].astype(_F32) + gb * yb).astype(_BF)
    upd = jnp.dot(mix_s[...], wo_ref[...], preferred_element_type=_F32)
    o_ref[0] = x_ref[0] + g1_ref[0] * upd


def _mix_call(x, z, ya, g1, ln_g, ln_b, ws_bf, bs_full, wo_bf, tm):
    B, S, D = x.shape

    def zspec(j):
        return pl.BlockSpec((1, tm, D), lambda b, i, j=j: (b, i, j))

    return pl.pallas_call(
        _mix_kernel,
        grid=(B, S // tm),
        in_specs=[zspec(0), zspec(3), zspec(4), zspec(5), zspec(6), zspec(0),
                  pl.BlockSpec((1, 1, D), lambda b, i: (b, 0, 0)),
                  _const_spec((1, D)), _const_spec((1, D)),
                  _const_spec((SGU_GROUPS, SGU_CHUNK, SGU_CHUNK)),
                  _const_spec((SGU_CHUNK, D)),
                  _const_spec((D, D))],
        out_specs=pl.BlockSpec((1, tm, D), lambda b, i: (b, i, 0)),
        out_shape=jax.ShapeDtypeStruct((B, S, D), _F32),
        scratch_shapes=[pltpu.VMEM((tm, D), _BF)],
        compiler_params=pltpu.CompilerParams(
            dimension_semantics=("arbitrary", "arbitrary"),
            vmem_limit_bytes=VMEM_LIMIT),
        name="mix",
    )(x, z, z, z, z, ya, g1, ln_g, ln_b, ws_bf, bs_full, wo_bf)


def _ffn_kernel(x_ref, g_ref, sc_ref, sh_ref, g2_ref, w1_ref, w2_ref, nf_ref, o_ref):
    x = x_ref[0]
    ms = jnp.mean(x * x, axis=-1, keepdims=True)
    gain = g_ref[...] * (1.0 + sc_ref[0])
    hb = (x * lax.rsqrt(ms + EPS) * gain + sh_ref[0]).astype(_BF)
    ff = jnp.zeros(x.shape, _F32)
    for j in range(FF_DIM // D_MODEL):
        cols = slice(j * D_MODEL, (j + 1) * D_MODEL)
        a = jnp.maximum(jnp.dot(hb, w1_ref[:, cols], preferred_element_type=_F32), 0.0)
        ff = ff + jnp.dot((a * a).astype(_BF), w2_ref[cols, :], preferred_element_type=_F32)
    x2 = x + g2_ref[0] * ff
    ms2 = jnp.mean(x2 * x2, axis=-1, keepdims=True)
    o_ref[0] = x2 * lax.rsqrt(ms2 + EPS) * nf_ref[...]


def _ffn_call(x1, norm_g, sc, sh, g2, w1_bf, w2_bf, normf_g, tm):
    B, S, D = x1.shape
    vec = pl.BlockSpec((1, 1, D), lambda b, i: (b, 0, 0))
    return pl.pallas_call(
        _ffn_kernel,
        grid=(B, S // tm),
        in_specs=[pl.BlockSpec((1, tm, D), lambda b, i: (b, i, 0)),
                  _const_spec((1, D)), vec, vec, vec,
                  _const_spec((D, FF_DIM)), _const_spec((FF_DIM, D)),
                  _const_spec((1, D))],
        out_specs=pl.BlockSpec((1, tm, D), lambda b, i: (b, i, 0)),
        out_shape=jax.ShapeDtypeStruct((B, S, D), _F32),
        compiler_params=pltpu.CompilerParams(
            dimension_semantics=("arbitrary", "arbitrary"),
            vmem_limit_bytes=VMEM_LIMIT),
        name="ffn",
    )(x1, norm_g, sc, sh, g2, w1_bf, w2_bf, normf_g)


def _gate_weights(w_if, b_if):
    H = HEADS
    idx_f, idx_i = [], []
    for h in range(H):
        idx_f += [H + h] * 3 + [3 * H + h] * 3 + [None] * 2
        idx_i += [None, h, h, None, 2 * H + h, 2 * H + h, None, None]
    sel = np.zeros((4 * H, 128), np.float32)
    for col, src in enumerate(idx_f + idx_i):
        if src is not None:
            sel[src, col] = 1.0
    w_g = (w_if[:, :, None] * sel[None]).sum(axis=1)
    b_g = (b_if.reshape(-1, 1) * sel).sum(axis=0, keepdims=True)
    return w_g, b_g


def kernel(x, c, w_ada, b_ada, norm1_g, norm2_g, w_in, b_if, conv_w, conv_b, mh_g,
           ln_v_g, ln_v_b, w_s, b_s, w_out, w1, w2, normf_g):
    B, S, D = x.shape
    H = HEADS
    assert w_ada.shape[0] == 1, "single layer"

    mod = _mod_call(c, w_ada[0], b_ada[0])
    sh1, sc1, g1, sh2, sc2, g2 = (mod[:, k * D:(k + 1) * D].reshape(B, 1, D) for k in range(6))

    w_in0 = w_in[0]
    w_z = w_in0.astype(_BF)
    w_g, b_g = _gate_weights(w_in0[:, 8 * D:], b_if[0])
    z, vt, grow = _in_proj_call(x, norm1_g, sc1, sh1, w_z, w_g.astype(_BF), b_g,
                                conv_w[0], conv_b, tm=512)

    ya = _mlstm_call(z, vt, grow.reshape(B, H, GATE_SLOTS, S), mh_g)

    bs_full = jnp.repeat(b_s[0].T, SGU_GROUP_DIM, axis=1)
    x1 = _mix_call(x, z, ya, g1, ln_v_g, ln_v_b, w_s[0].astype(_BF), bs_full,
                   w_out[0].astype(_BF), tm=512)

    return _ffn_call(x1, norm2_g, sc2, sh2, g2, w1[0].astype(_BF), w2[0].astype(_BF),
                     normf_g.reshape(1, D), tm=1024)
```

```python
import jax
import jax.numpy as jnp
import numpy as np
from jax import lax
from jax.experimental import pallas as pl
from jax.experimental.pallas import tpu as pltpu

D_MODEL = 1024
HEADS = 4
HEAD_DIM = 256
SGU_GROUPS = 8
SGU_GROUP_DIM = 128
SGU_CHUNK = 128
FF_DIM = 4096
EPS = 1e-6
Z_COLS = 7 * D_MODEL

MLSTM_CHUNK = 256
STATE_ROWS = HEAD_DIM + 16
STATE_BLOCK = 64
PASS0_GROUP = 8
PASS2_GROUP = 4
HALO = 16
GATE_SLOTS = 8
CONV_ROWS = 64
CONV_COLS = 128
MXU_COLS = 256

VMEM_LIMIT = 56 * 1024 * 1024

_BF = jnp.bfloat16
_F32 = jnp.float32
_NT = (((1,), (1,)), ((), ()))
_LOG2E = 1.4426950408889634


def _const_spec(shape):
    nd = len(shape)
    return pl.BlockSpec(shape, lambda *_: (0,) * nd, pipeline_mode=pl.Buffered(1))


def _sigmoid(x):
    return 1.0 / (1.0 + jnp.exp(-x))


def _log_sigmoid(x):
    return jnp.minimum(x, 0.0) - jnp.log(1.0 + jnp.exp(-jnp.abs(x)))


def _gelu(x):
    return 0.5 * x * (1.0 + lax.erf(x * (2.0 ** -0.5)))


def _mod_kernel(c_ref, w_ref, b_ref, o_ref):
    c = c_ref[...]
    ca = c * _sigmoid(c)
    o_ref[...] = jnp.dot(ca, w_ref[...], precision=lax.Precision.HIGHEST,
                         preferred_element_type=_F32) + b_ref[...]


def _cast_kernel(w_ref, o_ref):
    o_ref[...] = w_ref[0].astype(_BF)


def _cast_call(w, n_cols):
    rows = w.shape[1]
    return pl.pallas_call(
        _cast_kernel,
        grid=(n_cols // D_MODEL,),
        in_specs=[pl.BlockSpec((1, rows, D_MODEL), lambda j: (0, 0, j))],
        out_specs=pl.BlockSpec((rows, D_MODEL), lambda j: (0, j)),
        out_shape=jax.ShapeDtypeStruct((rows, n_cols), _BF),
        name="cast",
    )(w)


def _mod_call(c, w_ada, b_ada):
    B, D = c.shape
    N = w_ada.shape[1]
    tn = 1536
    return pl.pallas_call(
        _mod_kernel,
        grid=(N // tn,),
        in_specs=[pl.BlockSpec((B, D), lambda j: (0, 0)),
                  pl.BlockSpec((D, tn), lambda j: (0, j)),
                  pl.BlockSpec((1, tn), lambda j: (0, j))],
        out_specs=pl.BlockSpec((B, tn), lambda j: (0, j)),
        out_shape=jax.ShapeDtypeStruct((B, N), _F32),
        name="mod",
    )(c, w_ada, b_ada.reshape(1, N))


def _lane_scan(x, op, ident, reverse):
    L = x.shape[1]
    lane = lax.broadcasted_iota(jnp.int32, x.shape, 1)
    d = 1
    while d < L:
        if reverse:
            shifted = jnp.where(lane < L - d, pltpu.roll(x, L - d, axis=1), ident)
        else:
            shifted = jnp.where(lane >= d, pltpu.roll(x, d, axis=1), ident)
        x = op(x, shifted)
        d *= 2
    return x


def _in_proj_kernel(x_ref, xp_ref, xn_ref, g_ref, sc_ref, sh_ref, w_ref, wg_ref,
                    bg_ref, cw_ref, cb_ref, z_ref, vt_ref, grow_ref, zq_s, zk_s, wvt_s, hb_s):
    tm = x_ref.shape[1]
    L = MLSTM_CHUNK
    i = pl.program_id(1)

    gain = g_ref[...] * (1.0 + sc_ref[0])

    def norm_mod(x):
        ms = jnp.mean(x * x, axis=-1, keepdims=True)
        return (x * lax.rsqrt(ms + EPS) * gain + sh_ref[0]).astype(_BF)

    hb_s[HALO:HALO + tm, :] = norm_mod(x_ref[0])
    hb_s[:HALO, :] = jnp.where(i > 0, norm_mod(xp_ref[0]), jnp.zeros((HALO, D_MODEL), _BF))
    hb_s[HALO + tm:, :] = jnp.where(i < pl.num_programs(1) - 1, norm_mod(xn_ref[0]),
                                    jnp.zeros((HALO, D_MODEL), _BF))
    hb = hb_s.at[HALO:HALO + tm, :]

    def plain(c0):
        cols = slice(c0, c0 + MXU_COLS)
        wcols = slice(c0 + D_MODEL, c0 + D_MODEL + MXU_COLS)
        r = jnp.dot(hb[...], w_ref[:, wcols], preferred_element_type=_F32)
        if 3 * D_MODEL <= c0 < 5 * D_MODEL:
            r = _gelu(r)
        else:
            r = 1.0 / (1.0 + jnp.exp2(r * -_LOG2E))
        z_ref[0, :, cols] = r.astype(_BF)

    def conv_matmul(j, ze_s):
        for c0 in range(0, D_MODEL, MXU_COLS):
            ze_s[:, c0:c0 + MXU_COLS] = jnp.dot(
                hb_s[...], w_ref[:, j * D_MODEL + c0:j * D_MODEL + c0 + MXU_COLS],
                preferred_element_type=_F32)

    def conv_silu(j, ze_s, r0):
        n = CONV_ROWS + 16
        mid = slice(8, 8 + CONV_ROWS)
        for c0 in range(0, D_MODEL, CONV_COLS):
            cols = slice(j * D_MODEL + c0, j * D_MODEL + c0 + CONV_COLS)
            zb = ze_s[pl.ds(HALO - 8 + r0, n), c0:c0 + CONV_COLS]
            y = cb_ref[:, cols] + pltpu.roll(zb, 1, axis=0)[mid] * cw_ref[0:1, cols]
            y = y + zb[mid] * cw_ref[1:2, cols]
            y = y + pltpu.roll(zb, n - 1, axis=0)[mid] * cw_ref[2:3, cols]
            y = y / (1.0 + jnp.exp2(y * -_LOG2E))
            if j == 0:
                y = y * (HEAD_DIM ** -0.5)
            z_ref[0, pl.ds(r0, CONV_ROWS), cols] = y.astype(_BF)

    def gates_pre():
        gc = jnp.dot(hb[...], wg_ref[...], preferred_element_type=_F32) + bg_ref[...]
        return gc.T

    def gates_row(gt):
        nr = HEADS * GATE_SLOTS
        lst = _log_sigmoid(gt[:nr])
        git = gt[nr:2 * nr]
        slot = lax.broadcasted_iota(jnp.int32, (nr, L), 0) & (GATE_SLOTS - 1)
        fwd = slot < 3
        for c in range(tm // L):
            cols = slice(c * L, (c + 1) * L)
            lsc = lst[:, cols]
            cum = jnp.where(fwd, _lane_scan(lsc, jnp.add, 0.0, False),
                            _lane_scan(lsc, jnp.add, 0.0, True))
            base = jnp.where((slot == 0) | (slot == 3), cum, git[:, cols] - cum)
            cmax = jnp.where(fwd, _lane_scan(base, jnp.maximum, -jnp.inf, False),
                             _lane_scan(base, jnp.maximum, -jnp.inf, True))
            grow_ref[0, :, cols] = jnp.where((slot == 2) | (slot == 5), cmax, base)

    @pl.when((pl.program_id(0) == 0) & (i == 0))
    def _():
        wvt_s[...] = w_ref[:, 2 * D_MODEL:3 * D_MODEL].astype(_F32).T.astype(_BF)

    plain_blocks = iter(range(2 * D_MODEL, Z_COLS, MXU_COLS))
    gt = gates_pre()
    conv_matmul(0, zq_s)
    conv_matmul(1, zk_s)
    for j, ze_s in enumerate((zq_s, zk_s)):
        for r0 in range(0, tm, CONV_ROWS):
            conv_silu(j, ze_s, r0)
            plain(next(plain_blocks))
    gates_row(gt)
    for c0 in plain_blocks:
        plain(c0)
    vt_ref[0] = lax.dot_general(wvt_s[...], hb[...], _NT,
                                preferred_element_type=_F32).astype(_BF)


def _in_proj_call(x, norm_g, sc, sh, w_bf, wg_bf, bias_g, conv_w, conv_b, tm):
    B, S, D = x.shape
    r = tm // HALO
    last = S // HALO - 1
    return pl.pallas_call(
        _in_proj_kernel,
        grid=(B, S // tm),
        in_specs=[pl.BlockSpec((1, tm, D), lambda b, i: (b, i, 0)),
                  pl.BlockSpec((1, HALO, D), lambda b, i: (b, jnp.maximum(i * r - 1, 0), 0)),
                  pl.BlockSpec((1, HALO, D),
                               lambda b, i: (b, jnp.minimum((i + 1) * r, last), 0)),
                  _const_spec((1, D)),
                  pl.BlockSpec((1, 1, D), lambda b, i: (b, 0, 0)),
                  pl.BlockSpec((1, 1, D), lambda b, i: (b, 0, 0)),
                  _const_spec(w_bf.shape),
                  _const_spec((D, 128)),
                  _const_spec((1, 128)),
                  _const_spec((3, 2 * D)),
                  _const_spec((1, 2 * D))],
        out_specs=[pl.BlockSpec((1, tm, Z_COLS), lambda b, i: (b, i, 0)),
                   pl.BlockSpec((1, D, tm), lambda b, i: (b, 0, i)),
                   pl.BlockSpec((1, HEADS * GATE_SLOTS, tm), lambda b, i: (b, 0, i))],
        out_shape=[jax.ShapeDtypeStruct((B, S, Z_COLS), _BF),
                   jax.ShapeDtypeStruct((B, D, S), _BF),
                   jax.ShapeDtypeStruct((B, HEADS * GATE_SLOTS, S), _F32)],
        scratch_shapes=[pltpu.VMEM((tm + 2 * HALO, D), _F32),
                        pltpu.VMEM((tm + 2 * HALO, D), _F32),
                        pltpu.VMEM((D, D), _BF),
                        pltpu.VMEM((tm + 2 * HALO, D), _BF)],
        compiler_params=pltpu.CompilerParams(
            dimension_semantics=("arbitrary", "arbitrary"),
            vmem_limit_bytes=VMEM_LIMIT),
        name="in_proj",
    )(x, x, x, norm_g, sc, sh, w_bf, wg_bf, bias_g, conv_w, conv_b)


def _mlstm_kernel(q_ref, k_ref, vt_ref, zo_ref, gr_ref, mhg_ref,
                  y_ref,
                  pvf_s, pvb_s, u_s, snap_s, ct_s, mh_s, dec_s, ht_s):
    S = q_ref.shape[1]
    L = MLSTM_CHUNK
    NC = S // L
    dh = HEAD_DIM
    R = STATE_ROWS
    K0 = (0, 3)
    LAST = (L - 1, 0)

    def gate_rows(gr, d):
        k0 = K0[d]
        return gr[k0:k0 + 1], gr[k0 + 1:k0 + 2], gr[k0 + 2:k0 + 3]

    m = [jnp.zeros((1, 1), _F32), jnp.zeros((1, 1), _F32)]
    for i in range(NC):
        for d, c in enumerate((i, NC - 1 - i)):
            b_row, _, cmax_row = gate_rows(gr_ref[0, 0, :, c * L:(c + 1) * L], d)
            a_last = jnp.maximum(m[d], cmax_row[:, LAST[d]:LAST[d] + 1])
            mh_s[d * NC + c:d * NC + c + 1, :] = jnp.broadcast_to(m[d], (1, 128))
            dec_s[d * NC + c:d * NC + c + 1, :] = jnp.broadcast_to(jnp.exp(m[d] - a_last),
                                                                  (1, 128))
            m[d] = b_row[:, LAST[d]:LAST[d] + 1] + a_last

    si = lax.broadcasted_iota(jnp.int32, (L, L), 0)
    ti = lax.broadcasted_iota(jnp.int32, (L, L), 1)
    masks = (si <= ti, si >= ti)
    ones_row = (lax.broadcasted_iota(jnp.int32, (R - dh, L), 0) == 0).astype(_BF)

    def scores(c):
        t0 = pl.multiple_of(c * L, L)
        return lax.dot_general(k_ref[0, pl.ds(t0, L), :], q_ref[0, pl.ds(t0, L), :], _NT,
                               preferred_element_type=_F32)

    def pass0(g, carry):
        s_next = scores(g * PASS0_GROUP)
        for k in range(PASS0_GROUP):
            c = g * PASS0_GROUP + k
            t0 = pl.multiple_of(c * L, L)
            s_t = s_next
            vta = jnp.concatenate([vt_ref[0, :, pl.ds(t0, L)], ones_row], axis=0)
            vta32 = vta.astype(_F32)
            gr = gr_ref[0, 0, :, pl.ds(t0, L)]
            a_rows, vtw = [], []
            for d in range(2):
                _, r_row, cmax_row = gate_rows(gr, d)
                a_rows.append(jnp.maximum(mh_s[pl.ds(d * NC + c, 1), 0:1], cmax_row))
                w_row = jnp.exp(r_row - a_rows[d][:, LAST[d]:LAST[d] + 1])
                vtw.append((vta32 * w_row).astype(_BF))
            u_s[c] = jnp.dot(jnp.concatenate(vtw, axis=0), k_ref[0, pl.ds(t0, L), :],
                             preferred_element_type=_F32).astype(_BF)
            if k + 1 < PASS0_GROUP:
                s_next = scores(c + 1)
            p = []
            for d in range(2):
                r_row = gate_rows(gr, d)[1]
                r_rep = jnp.broadcast_to(r_row, (128, L)).T
                r_col = jnp.concatenate([r_rep] * (L // 128), axis=1)
                p.append((jnp.exp(jnp.where(masks[d], r_col - a_rows[d], -jnp.inf))
                          * s_t).astype(_BF))
            pv = jnp.dot(vta, jnp.concatenate(p, axis=1), preferred_element_type=_F32)
            pvf_s[:, pl.ds(t0, L)] = pv[:, :L]
            pvb_s[:, pl.ds(t0, L)] = pv[:, L:]
        return carry

    lax.fori_loop(0, NC // PASS0_GROUP, pass0, 0)

    ct_s[...] = jnp.zeros_like(ct_s)

    def pass1(i, carry):
        for d, c in enumerate((i, NC - 1 - i)):
            decay = dec_s[pl.ds(d * NC + c, 1), 0:1]
            for r0 in range(d * R, (d + 1) * R, STATE_BLOCK):
                rows = slice(r0, min(r0 + STATE_BLOCK, (d + 1) * R))
                ct = ct_s[rows, :]
                snap_s[c, rows, :] = ct.astype(_BF)
                ct_s[rows, :] = decay * ct + u_s[c, rows, :].astype(_F32)
        return carry

    lax.fori_loop(0, NC, pass1, 0, unroll=2)

    def inter_term(c):
        t0 = pl.multiple_of(c * L, L)
        return lax.dot_general(snap_s[c], q_ref[0, pl.ds(t0, L), :], _NT,
                               preferred_element_type=_F32)

    def pass2(g, carry):
        inter_next = inter_term(g * PASS2_GROUP)
        for k in range(PASS2_GROUP):
            c = g * PASS2_GROUP + k
            t0 = pl.multiple_of(c * L, L)
            inter = inter_next
            if k + 1 < PASS2_GROUP:
                inter_next = inter_term(c + 1)
            gr = gr_ref[0, 0, :, pl.ds(t0, L)]
            w_inter, scale = [], []
            for d, pv_s in enumerate((pvf_s, pvb_s)):
                b_row, _, cmax_row = gate_rows(gr, d)
                m_c = mh_s[pl.ds(d * NC + c, 1), 0:1]
                a_row = jnp.maximum(m_c, cmax_row)
                w_inter.append(jnp.exp(m_c - a_row))
                den = (w_inter[d] * inter[d * R + dh:d * R + dh + 1]
                       + pv_s[dh:dh + 1, pl.ds(t0, L)])
                scale.append(1.0 / jnp.maximum(jnp.abs(den), jnp.exp(-(b_row + a_row))))
            ss = jnp.zeros((1, L), _F32)
            for e0 in range(0, dh, STATE_BLOCK):
                blk = None
                for d, pv_s in enumerate((pvf_s, pvb_s)):
                    num = (w_inter[d] * inter[d * R + e0:d * R + e0 + STATE_BLOCK]
                           + pv_s[e0:e0 + STATE_BLOCK, pl.ds(t0, L)])
                    blk = num * scale[d] if blk is None else blk + num * scale[d]
                ss = ss + jnp.sum(blk * blk, axis=0, keepdims=True)
                ht_s[e0:e0 + STATE_BLOCK, :] = blk
            rs = lax.rsqrt(ss * (1.0 / dh) + EPS)
            for t1 in range(0, L, 128):
                h = (ht_s[:, t1:t1 + 128] * rs[:, t1:t1 + 128]).T
                tq = pl.multiple_of(t0 + t1, 128)
                o = zo_ref[0, pl.ds(tq, 128), :].astype(_F32)
                y_ref[0, pl.ds(tq, 128), :] = (h * mhg_ref[...] * o).astype(_BF)
        return carry

    lax.fori_loop(0, NC // PASS2_GROUP, pass2, 0)


def _mlstm_call(z, vt, grow, mh_g):
    B, S, _ = z.shape
    dh = HEAD_DIM
    H = HEADS
    nc = S // MLSTM_CHUNK

    def zspec(off):
        return pl.BlockSpec((1, S, dh), lambda b, h, off=off: (b, 0, off + h))

    return pl.pallas_call(
        _mlstm_kernel,
        grid=(B, H),
        in_specs=[zspec(0), zspec(H),
                  pl.BlockSpec((1, dh, S), lambda b, h: (b, h, 0)),
                  zspec(2 * H),
                  pl.BlockSpec((1, 1, 8, S), lambda b, h: (b, h, 0, 0)),
                  pl.BlockSpec((1, dh), lambda b, h: (0, h))],
        out_specs=pl.BlockSpec((1, S, dh), lambda b, h: (b, 0, h)),
        out_shape=jax.ShapeDtypeStruct((B, S, H * dh), _BF),
        scratch_shapes=[pltpu.VMEM((STATE_ROWS, S), _F32), pltpu.VMEM((STATE_ROWS, S), _F32),
                        pltpu.VMEM((nc, 2 * STATE_ROWS, dh), _BF),
                        pltpu.VMEM((nc, 2 * STATE_ROWS, dh), _BF),
                        pltpu.VMEM((2 * STATE_ROWS, dh), _F32),
                        pltpu.VMEM((2 * nc, 128), _F32), pltpu.VMEM((2 * nc, 128), _F32),
                        pltpu.VMEM((dh, MLSTM_CHUNK), _F32)],
        compiler_params=pltpu.CompilerParams(
            dimension_semantics=("arbitrary", "arbitrary"),
            vmem_limit_bytes=VMEM_LIMIT),
        name="mlstm",
    )(z, z, vt, z, grow, mh_g)


def _mix_kernel(x_ref, zu_ref, zv_ref, zga_ref, zgb_ref, ya_ref, g1_ref, lng_ref, lnb_ref,
                ws_ref, bs_ref, wo_ref, o_ref, mix_s):
    tm = x_ref.shape[1]
    P = SGU_CHUNK
    C = SGU_GROUP_DIM
    for j in range(tm // P):
        rows = slice(j * P, (j + 1) * P)
        gv = zv_ref[0, rows, :].astype(_F32)
        mu = jnp.mean(gv, axis=-1, keepdims=True)
        dv = gv - mu
        var = jnp.mean(dv * dv, axis=-1, keepdims=True)
        vn = (dv * lax.rsqrt(var + EPS) * lng_ref[...] + lnb_ref[...]).astype(_BF)
        s = jnp.concatenate(
            [jnp.dot(ws_ref[g], vn[:, g * C:(g + 1) * C], preferred_element_type=_F32)
             for g in range(SGU_GROUPS)], axis=1) + bs_ref[...]
        yb = zu_ref[0, rows, :].astype(_F32) * s
        ga = zga_ref[0, rows, :].astype(_F32)
        gb = zgb_ref[0, rows, :].astype(_F32)
        mix_s[rows, :] = (ga * ya_ref[0, rows, :].astype(_F32) + gb * yb).astype(_BF)
    upd = jnp.dot(mix_s[...], wo_ref[...], preferred_element_type=_F32)
    o_ref[0] = x_ref[0] + g1_ref[0] * upd


def _mix_call(x, z, ya, g1, ln_g, ln_b, ws_bf, bs_full, wo_bf, tm):
    B, S, D = x.shape

    def zspec(j):
        return pl.BlockSpec((1, tm, D), lambda b, i, j=j: (b, i, j))

    return pl.pallas_call(
        _mix_kernel,
        grid=(B, S // tm),
        in_specs=[zspec(0), zspec(3), zspec(4), zspec(5), zspec(6), zspec(0),
                  pl.BlockSpec((1, 1, D), lambda b, i: (b, 0, 0)),
                  _const_spec((1, D)), _const_spec((1, D)),
                  _const_spec((SGU_GROUPS, SGU_CHUNK, SGU_CHUNK)),
                  _const_spec((SGU_CHUNK, D)),
                  _const_spec((D, D))],
        out_specs=pl.BlockSpec((1, tm, D), lambda b, i: (b, i, 0)),
        out_shape=jax.ShapeDtypeStruct((B, S, D), _F32),
        scratch_shapes=[pltpu.VMEM((tm, D), _BF)],
        compiler_params=pltpu.CompilerParams(
            dimension_semantics=("arbitrary", "arbitrary"),
            vmem_limit_bytes=VMEM_LIMIT),
        name="mix",
    )(x, z, z, z, z, ya, g1, ln_g, ln_b, ws_bf, bs_full, wo_bf)


def _ffn_kernel(x_ref, g_ref, sc_ref, sh_ref, g2_ref, w1_ref, w2_ref, nf_ref, o_ref):
    x = x_ref[0]
    ms = jnp.mean(x * x, axis=-1, keepdims=True)
    gain = g_ref[...] * (1.0 + sc_ref[0])
    hb = (x * lax.rsqrt(ms + EPS) * gain + sh_ref[0]).astype(_BF)
    ff = jnp.zeros(x.shape, _F32)
    for j in range(FF_DIM // D_MODEL):
        cols = slice(j * D_MODEL, (j + 1) * D_MODEL)
        a = jnp.maximum(jnp.dot(hb, w1_ref[:, cols], preferred_element_type=_F32), 0.0)
        ff = ff + jnp.dot((a * a).astype(_BF), w2_ref[cols, :], preferred_element_type=_F32)
    x2 = x + g2_ref[0] * ff
    ms2 = jnp.mean(x2 * x2, axis=-1, keepdims=True)
    o_ref[0] = x2 * lax.rsqrt(ms2 + EPS) * nf_ref[...]


def _ffn_call(x1, norm_g, sc, sh, g2, w1_bf, w2_bf, normf_g, tm):
    B, S, D = x1.shape
    vec = pl.BlockSpec((1, 1, D), lambda b, i: (b, 0, 0))
    return pl.pallas_call(
        _ffn_kernel,
        grid=(B, S // tm),
        in_specs=[pl.BlockSpec((1, tm, D), lambda b, i: (b, i, 0)),
                  _const_spec((1, D)), vec, vec, vec,
                  _const_spec((D, FF_DIM)), _const_spec((FF_DIM, D)),
                  _const_spec((1, D))],
        out_specs=pl.BlockSpec((1, tm, D), lambda b, i: (b, i, 0)),
        out_shape=jax.ShapeDtypeStruct((B, S, D), _F32),
        compiler_params=pltpu.CompilerParams(
            dimension_semantics=("arbitrary", "arbitrary"),
            vmem_limit_bytes=VMEM_LIMIT),
        name="ffn",
    )(x1, norm_g, sc, sh, g2, w1_bf, w2_bf, normf_g)


def _gate_weights(w_if, b_if):
    H = HEADS
    idx_f, idx_i = [], []
    for h in range(H):
        idx_f += [H + h] * 3 + [3 * H + h] * 3 + [None] * 2
        idx_i += [None, h, h, None, 2 * H + h, 2 * H + h, None, None]
    sel = np.zeros((4 * H, 128), np.float32)
    for col, src in enumerate(idx_f + idx_i):
        if src is not None:
            sel[src, col] = 1.0
    w_g = (w_if[:, :, None] * sel[None]).sum(axis=1)
    b_g = (b_if.reshape(-1, 1) * sel).sum(axis=0, keepdims=True)
    return w_g, b_g


def kernel(x, c, w_ada, b_ada, norm1_g, norm2_g, w_in, b_if, conv_w, conv_b, mh_g,
           ln_v_g, ln_v_b, w_s, b_s, w_out, w1, w2, normf_g):
    B, S, D = x.shape
    H = HEADS
    assert w_ada.shape[0] == 1, "single layer"

    mod = _mod_call(c, w_ada[0], b_ada[0])
    sh1, sc1, g1, sh2, sc2, g2 = (mod[:, k * D:(k + 1) * D].reshape(B, 1, D) for k in range(6))

    w_in0 = w_in[0]
    w_z = _cast_call(w_in, 8 * D)
    w_g, b_g = _gate_weights(w_in0[:, 8 * D:], b_if[0])
    z, vt, grow = _in_proj_call(x, norm1_g, sc1, sh1, w_z, w_g.astype(_BF), b_g,
                                conv_w[0], conv_b, tm=512)

    ya = _mlstm_call(z, vt, grow.reshape(B, H, GATE_SLOTS, S), mh_g)

    bs_full = jnp.repeat(b_s[0].T, SGU_GROUP_DIM, axis=1)
    x1 = _mix_call(x, z, ya, g1, ln_v_g, ln_v_b, w_s[0].astype(_BF), bs_full,
                   w_out[0].astype(_BF), tm=512)

    return _ffn_call(x1, norm2_g, sc2, sh2, g2, w1[0].astype(_BF), w2[0].astype(_BF),
                     normf_g.reshape(1, D), tm=1024)
```

```python
import jax
import jax.numpy as jnp
import numpy as np
from jax import lax
from jax.experimental import pallas as pl
from jax.experimental.pallas import tpu as pltpu

D_MODEL = 1024
HEADS = 4
HEAD_DIM = 256
SGU_GROUPS = 8
SGU_GROUP_DIM = 128
SGU_CHUNK = 128
FF_DIM = 4096
EPS = 1e-6
Z_COLS = 7 * D_MODEL

MLSTM_CHUNK = 256
STATE_ROWS = HEAD_DIM + 16
STATE_BLOCK = 64
PASS0_GROUP = 8
PASS2_GROUP = 4
HALO = 16
GATE_SLOTS = 8
CONV_ROWS = 64
CONV_COLS = 128
MXU_COLS = 256

VMEM_LIMIT = 56 * 1024 * 1024

_BF = jnp.bfloat16
_F32 = jnp.float32
_NT = (((1,), (1,)), ((), ()))
_LOG2E = 1.4426950408889634


def _const_spec(shape):
    nd = len(shape)
    return pl.BlockSpec(shape, lambda *_: (0,) * nd, pipeline_mode=pl.Buffered(1))


def _sigmoid(x):
    return 1.0 / (1.0 + jnp.exp(-x))


def _log_sigmoid(x):
    return jnp.minimum(x, 0.0) - jnp.log(1.0 + jnp.exp(-jnp.abs(x)))


def _gelu(x):
    return 0.5 * x * (1.0 + lax.erf(x * (2.0 ** -0.5)))


def _mod_kernel(c_ref, w_ref, b_ref, o_ref):
    c = c_ref[...]
    ca = c * _sigmoid(c)
    o_ref[...] = jnp.dot(ca, w_ref[...], precision=lax.Precision.HIGHEST,
                         preferred_element_type=_F32) + b_ref[...]


def _mod_call(c, w_ada, b_ada):
    B, D = c.shape
    N = w_ada.shape[1]
    tn = 1536
    return pl.pallas_call(
        _mod_kernel,
        grid=(N // tn,),
        in_specs=[pl.BlockSpec((B, D), lambda j: (0, 0)),
                  pl.BlockSpec((D, tn), lambda j: (0, j)),
                  pl.BlockSpec((1, tn), lambda j: (0, j))],
        out_specs=pl.BlockSpec((B, tn), lambda j: (0, j)),
        out_shape=jax.ShapeDtypeStruct((B, N), _F32),
        name="mod",
    )(c, w_ada, b_ada.reshape(1, N))


def _lane_scan(x, op, ident, reverse):
    L = x.shape[1]
    lane = lax.broadcasted_iota(jnp.int32, x.shape, 1)
    d = 1
    while d < L:
        if reverse:
            shifted = jnp.where(lane < L - d, pltpu.roll(x, L - d, axis=1), ident)
        else:
            shifted = jnp.where(lane >= d, pltpu.roll(x, d, axis=1), ident)
        x = op(x, shifted)
        d *= 2
    return x


def _in_proj_kernel(x_ref, xp_ref, xn_ref, g_ref, sc_ref, sh_ref, w_ref, wg_ref,
                    bg_ref, cw_ref, cb_ref, z_ref, vt_ref, grow_ref, zq_s, zk_s, hb_s):
    tm = x_ref.shape[1]
    L = MLSTM_CHUNK
    i = pl.program_id(1)

    gain = g_ref[...] * (1.0 + sc_ref[0])

    def norm_mod(x):
        ms = jnp.mean(x * x, axis=-1, keepdims=True)
        return (x * lax.rsqrt(ms + EPS) * gain + sh_ref[0]).astype(_BF)

    hb_s[HALO:HALO + tm, :] = norm_mod(x_ref[0])
    hb_s[:HALO, :] = jnp.where(i > 0, norm_mod(xp_ref[0]), jnp.zeros((HALO, D_MODEL), _BF))
    hb_s[HALO + tm:, :] = jnp.where(i < pl.num_programs(1) - 1, norm_mod(xn_ref[0]),
                                    jnp.zeros((HALO, D_MODEL), _BF))
    hb = hb_s.at[HALO:HALO + tm, :]

    def plain(c0):
        cols = slice(c0, c0 + MXU_COLS)
        wrows = slice(c0 + D_MODEL, c0 + D_MODEL + MXU_COLS)
        r = lax.dot_general(hb[...], w_ref[wrows, :], _NT, preferred_element_type=_F32)
        if 3 * D_MODEL <= c0 < 5 * D_MODEL:
            r = _gelu(r)
        else:
            r = 1.0 / (1.0 + jnp.exp2(r * -_LOG2E))
        z_ref[0, :, cols] = r.astype(_BF)

    def conv_matmul(j, ze_s):
        for c0 in range(0, D_MODEL, MXU_COLS):
            ze_s[:, c0:c0 + MXU_COLS] = lax.dot_general(
                hb_s[...], w_ref[j * D_MODEL + c0:j * D_MODEL + c0 + MXU_COLS, :], _NT,
                preferred_element_type=_F32)

    def conv_silu(j, ze_s, r0):
        n = CONV_ROWS + 16
        mid = slice(8, 8 + CONV_ROWS)
        for c0 in range(0, D_MODEL, CONV_COLS):
            cols = slice(j * D_MODEL + c0, j * D_MODEL + c0 + CONV_COLS)
            zb = ze_s[pl.ds(HALO - 8 + r0, n), c0:c0 + CONV_COLS]
            y = cb_ref[:, cols] + pltpu.roll(zb, 1, axis=0)[mid] * cw_ref[0:1, cols]
            y = y + zb[mid] * cw_ref[1:2, cols]
            y = y + pltpu.roll(zb, n - 1, axis=0)[mid] * cw_ref[2:3, cols]
            y = y / (1.0 + jnp.exp2(y * -_LOG2E))
            if j == 0:
                y = y * (HEAD_DIM ** -0.5)
            z_ref[0, pl.ds(r0, CONV_ROWS), cols] = y.astype(_BF)

    def gates_pre():
        gc = lax.dot_general(hb[...], wg_ref[...], _NT,
                             preferred_element_type=_F32) + bg_ref[...]
        return gc.T

    def gates_row(gt):
        nr = HEADS * GATE_SLOTS
        lst = _log_sigmoid(gt[:nr])
        git = gt[nr:2 * nr]
        slot = lax.broadcasted_iota(jnp.int32, (nr, L), 0) & (GATE_SLOTS - 1)
        fwd = slot < 3
        for c in range(tm // L):
            cols = slice(c * L, (c + 1) * L)
            lsc = lst[:, cols]
            cum = jnp.where(fwd, _lane_scan(lsc, jnp.add, 0.0, False),
                            _lane_scan(lsc, jnp.add, 0.0, True))
            base = jnp.where((slot == 0) | (slot == 3), cum, git[:, cols] - cum)
            cmax = jnp.where(fwd, _lane_scan(base, jnp.maximum, -jnp.inf, False),
                             _lane_scan(base, jnp.maximum, -jnp.inf, True))
            grow_ref[0, :, cols] = jnp.where((slot == 2) | (slot == 5), cmax, base)

    plain_blocks = iter(range(2 * D_MODEL, Z_COLS, MXU_COLS))
    gt = gates_pre()
    conv_matmul(0, zq_s)
    conv_matmul(1, zk_s)
    for j, ze_s in enumerate((zq_s, zk_s)):
        for r0 in range(0, tm, CONV_ROWS):
            conv_silu(j, ze_s, r0)
            plain(next(plain_blocks))
    gates_row(gt)
    for c0 in plain_blocks:
        plain(c0)
    vt_ref[0] = lax.dot_general(w_ref[2 * D_MODEL:3 * D_MODEL, :], hb[...], _NT,
                                preferred_element_type=_F32).astype(_BF)


def _in_proj_call(x, norm_g, sc, sh, w_bf, wg_bf, bias_g, conv_w, conv_b, tm):
    B, S, D = x.shape
    r = tm // HALO
    last = S // HALO - 1
    return pl.pallas_call(
        _in_proj_kernel,
        grid=(B, S // tm),
        in_specs=[pl.BlockSpec((1, tm, D), lambda b, i: (b, i, 0)),
                  pl.BlockSpec((1, HALO, D), lambda b, i: (b, jnp.maximum(i * r - 1, 0), 0)),
                  pl.BlockSpec((1, HALO, D),
                               lambda b, i: (b, jnp.minimum((i + 1) * r, last), 0)),
                  _const_spec((1, D)),
                  pl.BlockSpec((1, 1, D), lambda b, i: (b, 0, 0)),
                  pl.BlockSpec((1, 1, D), lambda b, i: (b, 0, 0)),
                  _const_spec(w_bf.shape),
                  _const_spec((128, D)),
                  _const_spec((1, 128)),
                  _const_spec((3, 2 * D)),
                  _const_spec((1, 2 * D))],
        out_specs=[pl.BlockSpec((1, tm, Z_COLS), lambda b, i: (b, i, 0)),
                   pl.BlockSpec((1, D, tm), lambda b, i: (b, 0, i)),
                   pl.BlockSpec((1, HEADS * GATE_SLOTS, tm), lambda b, i: (b, 0, i))],
        out_shape=[jax.ShapeDtypeStruct((B, S, Z_COLS), _BF),
                   jax.ShapeDtypeStruct((B, D, S), _BF),
                   jax.ShapeDtypeStruct((B, HEADS * GATE_SLOTS, S), _F32)],
        scratch_shapes=[pltpu.VMEM((tm + 2 * HALO, D), _F32),
                        pltpu.VMEM((tm + 2 * HALO, D), _F32),
                        pltpu.VMEM((tm + 2 * HALO, D), _BF)],
        compiler_params=pltpu.CompilerParams(
            dimension_semantics=("arbitrary", "arbitrary"),
            vmem_limit_bytes=VMEM_LIMIT),
        name="in_proj",
    )(x, x, x, norm_g, sc, sh, w_bf, wg_bf, bias_g, conv_w, conv_b)


def _mlstm_kernel(q_ref, k_ref, vt_ref, zo_ref, gr_ref, mhg_ref,
                  y_ref,
                  pvf_s, pvb_s, u_s, snap_s, ct_s, mh_s, dec_s, ht_s):
    S = q_ref.shape[1]
    L = MLSTM_CHUNK
    NC = S // L
    dh = HEAD_DIM
    R = STATE_ROWS
    K0 = (0, 3)
    LAST = (L - 1, 0)

    def gate_rows(gr, d):
        k0 = K0[d]
        return gr[k0:k0 + 1], gr[k0 + 1:k0 + 2], gr[k0 + 2:k0 + 3]

    m = [jnp.zeros((1, 1), _F32), jnp.zeros((1, 1), _F32)]
    for i in range(NC):
        for d, c in enumerate((i, NC - 1 - i)):
            b_row, _, cmax_row = gate_rows(gr_ref[0, 0, :, c * L:(c + 1) * L], d)
            a_last = jnp.maximum(m[d], cmax_row[:, LAST[d]:LAST[d] + 1])
            mh_s[d * NC + c:d * NC + c + 1, :] = jnp.broadcast_to(m[d], (1, 128))
            dec_s[d * NC + c:d * NC + c + 1, :] = jnp.broadcast_to(jnp.exp(m[d] - a_last),
                                                                  (1, 128))
            m[d] = b_row[:, LAST[d]:LAST[d] + 1] + a_last

    si = lax.broadcasted_iota(jnp.int32, (L, L), 0)
    ti = lax.broadcasted_iota(jnp.int32, (L, L), 1)
    masks = (si <= ti, si >= ti)
    ones_row = (lax.broadcasted_iota(jnp.int32, (R - dh, L), 0) == 0).astype(_BF)

    def scores(c):
        t0 = pl.multiple_of(c * L, L)
        return lax.dot_general(k_ref[0, pl.ds(t0, L), :], q_ref[0, pl.ds(t0, L), :], _NT,
                               preferred_element_type=_F32)

    def pass0(g, carry):
        s_next = scores(g * PASS0_GROUP)
        for k in range(PASS0_GROUP):
            c = g * PASS0_GROUP + k
            t0 = pl.multiple_of(c * L, L)
            s_t = s_next
            vta = jnp.concatenate([vt_ref[0, :, pl.ds(t0, L)], ones_row], axis=0)
            vta32 = vta.astype(_F32)
            gr = gr_ref[0, 0, :, pl.ds(t0, L)]
            a_rows, vtw = [], []
            for d in range(2):
                _, r_row, cmax_row = gate_rows(gr, d)
                a_rows.append(jnp.maximum(mh_s[pl.ds(d * NC + c, 1), 0:1], cmax_row))
                w_row = jnp.exp(r_row - a_rows[d][:, LAST[d]:LAST[d] + 1])
                vtw.append((vta32 * w_row).astype(_BF))
            u_s[c] = jnp.dot(jnp.concatenate(vtw, axis=0), k_ref[0, pl.ds(t0, L), :],
                             preferred_element_type=_F32).astype(_BF)
            if k + 1 < PASS0_GROUP:
                s_next = scores(c + 1)
            p = []
            for d in range(2):
                r_row = gate_rows(gr, d)[1]
                r_rep = jnp.broadcast_to(r_row, (128, L)).T
                r_col = jnp.concatenate([r_rep] * (L // 128), axis=1)
                p.append((jnp.exp(jnp.where(masks[d], r_col - a_rows[d], -jnp.inf))
                          * s_t).astype(_BF))
            pv = jnp.dot(vta, jnp.concatenate(p, axis=1), preferred_element_type=_F32)
            pvf_s[:, pl.ds(t0, L)] = pv[:, :L]
            pvb_s[:, pl.ds(t0, L)] = pv[:, L:]
        return carry

    lax.fori_loop(0, NC // PASS0_GROUP, pass0, 0)

    ct_s[...] = jnp.zeros_like(ct_s)

    def pass1(i, carry):
        for d, c in enumerate((i, NC - 1 - i)):
            decay = dec_s[pl.ds(d * NC + c, 1), 0:1]
            for r0 in range(d * R, (d + 1) * R, STATE_BLOCK):
                rows = slice(r0, min(r0 + STATE_BLOCK, (d + 1) * R))
                ct = ct_s[rows, :]
                snap_s[c, rows, :] = ct.astype(_BF)
                ct_s[rows, :] = decay * ct + u_s[c, rows, :].astype(_F32)
        return carry

    lax.fori_loop(0, NC, pass1, 0, unroll=2)

    def inter_term(c):
        t0 = pl.multiple_of(c * L, L)
        return lax.dot_general(snap_s[c], q_ref[0, pl.ds(t0, L), :], _NT,
                               preferred_element_type=_F32)

    def pass2(g, carry):
        inter_next = inter_term(g * PASS2_GROUP)
        for k in range(PASS2_GROUP):
            c = g * PASS2_GROUP + k
            t0 = pl.multiple_of(c * L, L)
            inter = inter_next
            if k + 1 < PASS2_GROUP:
                inter_next = inter_term(c + 1)
            gr = gr_ref[0, 0, :, pl.ds(t0, L)]
            w_inter, scale = [], []
            for d, pv_s in enumerate((pvf_s, pvb_s)):
                b_row, _, cmax_row = gate_rows(gr, d)
                m_c = mh_s[pl.ds(d * NC + c, 1), 0:1]
                a_row = jnp.maximum(m_c, cmax_row)
                w_inter.append(jnp.exp(m_c - a_row))
                den = (w_inter[d] * inter[d * R + dh:d * R + dh + 1]
                       + pv_s[dh:dh + 1, pl.ds(t0, L)])
                scale.append(1.0 / jnp.maximum(jnp.abs(den), jnp.exp(-(b_row + a_row))))
            ss = jnp.zeros((1, L), _F32)
            for e0 in range(0, dh, STATE_BLOCK):
                blk = None
                for d, pv_s in enumerate((pvf_s, pvb_s)):
                    num = (w_inter[d] * inter[d * R + e0:d * R + e0 + STATE_BLOCK]
                           + pv_s[e0:e0 + STATE_BLOCK, pl.ds(t0, L)])
                    blk = num * scale[d] if blk is None else blk + num * scale[d]
                ss = ss + jnp.sum(blk * blk, axis=0, keepdims=True)
                ht_s[e0:e0 + STATE_BLOCK, :] = blk
            rs = lax.rsqrt(ss * (1.0 / dh) + EPS)
            for t1 in range(0, L, 128):
                h = (ht_s[:, t1:t1 + 128] * rs[:, t1:t1 + 128]).T
                tq = pl.multiple_of(t0 + t1, 128)
                o = zo_ref[0, pl.ds(tq, 128), :].astype(_F32)
                y_ref[0, pl.ds(tq, 128), :] = (h * mhg_ref[...] * o).astype(_BF)
        return carry

    lax.fori_loop(0, NC // PASS2_GROUP, pass2, 0)


def _mlstm_call(z, vt, grow, mh_g):
    B, S, _ = z.shape
    dh = HEAD_DIM
    H = HEADS
    nc = S // MLSTM_CHUNK

    def zspec(off):
        return pl.BlockSpec((1, S, dh), lambda b, h, off=off: (b, 0, off + h))

    return pl.pallas_call(
        _mlstm_kernel,
        grid=(B, H),
        in_specs=[zspec(0), zspec(H),
                  pl.BlockSpec((1, dh, S), lambda b, h: (b, h, 0)),
                  zspec(2 * H),
                  pl.BlockSpec((1, 1, 8, S), lambda b, h: (b, h, 0, 0)),
                  pl.BlockSpec((1, dh), lambda b, h: (0, h))],
        out_specs=pl.BlockSpec((1, S, dh), lambda b, h: (b, 0, h)),
        out_shape=jax.ShapeDtypeStruct((B, S, H * dh), _BF),
        scratch_shapes=[pltpu.VMEM((STATE_ROWS, S), _F32), pltpu.VMEM((STATE_ROWS, S), _F32),
                        pltpu.VMEM((nc, 2 * STATE_ROWS, dh), _BF),
                        pltpu.VMEM((nc, 2 * STATE_ROWS, dh), _BF),
                        pltpu.VMEM((2 * STATE_ROWS, dh), _F32),
                        pltpu.VMEM((2 * nc, 128), _F32), pltpu.VMEM((2 * nc, 128), _F32),
                        pltpu.VMEM((dh, MLSTM_CHUNK), _F32)],
        compiler_params=pltpu.CompilerParams(
            dimension_semantics=("arbitrary", "arbitrary"),
            vmem_limit_bytes=VMEM_LIMIT),
        name="mlstm",
    )(z, z, vt, z, grow, mh_g)


def _mix_kernel(x_ref, zu_ref, zv_ref, zga_ref, zgb_ref, ya_ref, g1_ref, lng_ref, lnb_ref,
                ws_ref, bs_ref, wo_ref, o_ref, mix_s):
    tm = x_ref.shape[1]
    P = SGU_CHUNK
    C = SGU_GROUP_DIM
    for j in range(tm // P):
        rows = slice(j * P, (j + 1) * P)
        gv = zv_ref[0, rows, :].astype(_F32)
        mu = jnp.mean(gv, axis=-1, keepdims=True)
        dv = gv - mu
        var = jnp.mean(dv * dv, axis=-1, keepdims=True)
        vn = (dv * lax.rsqrt(var + EPS) * lng_ref[...] + lnb_ref[...]).astype(_BF)
        s = jnp.concatenate(
            [jnp.dot(ws_ref[g], vn[:, g * C:(g + 1) * C], preferred_element_type=_F32)
             for g in range(SGU_GROUPS)], axis=1) + bs_ref[...]
        yb = zu_ref[0, rows, :].astype(_F32) * s
        ga = zga_ref[0, rows, :].astype(_F32)
        gb = zgb_ref[0, rows, :].astype(_F32)
        mix_s[rows, :] = (ga * ya_ref[0, rows, :].astype(_F32) + gb * yb).astype(_BF)
    upd = jnp.dot(mix_s[...], wo_ref[...], preferred_element_type=_F32)
    o_ref[0] = x_ref[0] + g1_ref[0] * upd


def _mix_call(x, z, ya, g1, ln_g, ln_b, ws_bf, bs_full, wo_bf, tm):
    B, S, D = x.shape

    def zspec(j):
        return pl.BlockSpec((1, tm, D), lambda b, i, j=j: (b, i, j))

    return pl.pallas_call(
        _mix_kernel,
        grid=(B, S // tm),
        in_specs=[zspec(0), zspec(3), zspec(4), zspec(5), zspec(6), zspec(0),
                  pl.BlockSpec((1, 1, D), lambda b, i: (b, 0, 0)),
                  _const_spec((1, D)), _const_spec((1, D)),
                  _const_spec((SGU_GROUPS, SGU_CHUNK, SGU_CHUNK)),
                  _const_spec((SGU_CHUNK, D)),
                  _const_spec((D, D))],
        out_specs=pl.BlockSpec((1, tm, D), lambda b, i: (b, i, 0)),
        out_shape=jax.ShapeDtypeStruct((B, S, D), _F32),
        scratch_shapes=[pltpu.VMEM((tm, D), _BF)],
        compiler_params=pltpu.CompilerParams(
            dimension_semantics=("arbitrary", "arbitrary"),
            vmem_limit_bytes=VMEM_LIMIT),
        name="mix",
    )(x, z, z, z, z, ya, g1, ln_g, ln_b, ws_bf, bs_full, wo_bf)


def _ffn_kernel(x_ref, g_ref, sc_ref, sh_ref, g2_ref, w1_ref, w2_ref, nf_ref, o_ref):
    x = x_ref[0]
    ms = jnp.mean(x * x, axis=-1, keepdims=True)
    gain = g_ref[...] * (1.0 + sc_ref[0])
    hb = (x * lax.rsqrt(ms + EPS) * gain + sh_ref[0]).astype(_BF)
    ff = jnp.zeros(x.shape, _F32)
    for j in range(FF_DIM // D_MODEL):
        cols = slice(j * D_MODEL, (j + 1) * D_MODEL)
        a = jnp.maximum(jnp.dot(hb, w1_ref[:, cols], preferred_element_type=_F32), 0.0)
        ff = ff + jnp.dot((a * a).astype(_BF), w2_ref[cols, :], preferred_element_type=_F32)
    x2 = x + g2_ref[0] * ff
    ms2 = jnp.mean(x2 * x2, axis=-1, keepdims=True)
    o_ref[0] = x2 * lax.rsqrt(ms2 + EPS) * nf_ref[...]


def _ffn_call(x1, norm_g, sc, sh, g2, w1_bf, w2_bf, normf_g, tm):
    B, S, D = x1.shape
    vec = pl.BlockSpec((1, 1, D), lambda b, i: (b, 0, 0))
    return pl.pallas_call(
        _ffn_kernel,
        grid=(B, S // tm),
        in_specs=[pl.BlockSpec((1, tm, D), lambda b, i: (b, i, 0)),
                  _const_spec((1, D)), vec, vec, vec,
                  _const_spec((D, FF_DIM)), _const_spec((FF_DIM, D)),
                  _const_spec((1, D))],
        out_specs=pl.BlockSpec((1, tm, D), lambda b, i: (b, i, 0)),
        out_shape=jax.ShapeDtypeStruct((B, S, D), _F32),
        compiler_params=pltpu.CompilerParams(
            dimension_semantics=("arbitrary", "arbitrary"),
            vmem_limit_bytes=VMEM_LIMIT),
        name="ffn",
    )(x1, norm_g, sc, sh, g2, w1_bf, w2_bf, normf_g)


def _gate_weights(w_if_t, b_if):
    H = HEADS
    idx_f, idx_i = [], []
    for h in range(H):
        idx_f += [H + h] * 3 + [3 * H + h] * 3 + [None] * 2
        idx_i += [None, h, h, None, 2 * H + h, 2 * H + h, None, None]
    sel = np.zeros((128, 4 * H), np.float32)
    for row, src in enumerate(idx_f + idx_i):
        if src is not None:
            sel[row, src] = 1.0
    w_g = (sel[:, :, None] * w_if_t[None]).sum(axis=1)
    b_g = (sel * b_if.reshape(1, -1)).sum(axis=1).reshape(1, 128)
    return w_g, b_g


def kernel(x, c, w_ada, b_ada, norm1_g, norm2_g, w_in, b_if, conv_w, conv_b, mh_g,
           ln_v_g, ln_v_b, w_s, b_s, w_out, w1, w2, normf_g):
    B, S, D = x.shape
    H = HEADS
    assert w_ada.shape[0] == 1, "single layer"

    mod = _mod_call(c, w_ada[0], b_ada[0])
    sh1, sc1, g1, sh2, sc2, g2 = (mod[:, k * D:(k + 1) * D].reshape(B, 1, D) for k in range(6))

    w_in_t = w_in[0].T
    w_g, b_g = _gate_weights(w_in_t[8 * D:], b_if[0])
    z, vt, grow = _in_proj_call(x, norm1_g, sc1, sh1, w_in_t.astype(_BF), w_g.astype(_BF), b_g,
                                conv_w[0], conv_b, tm=512)

    ya = _mlstm_call(z, vt, grow.reshape(B, H, GATE_SLOTS, S), mh_g)

    bs_full = jnp.repeat(b_s[0].T, SGU_GROUP_DIM, axis=1)
    x1 = _mix_call(x, z, ya, g1, ln_v_g, ln_v_b, w_s[0].astype(_BF), bs_full,
                   w_out[0].astype(_BF), tm=512)

    return _ffn_call(x1, norm2_g, sc2, sh2, g2, w1[0].astype(_BF), w2[0].astype(_BF),
                     normf_g.reshape(1, D), tm=1024)
```

```python
import jax
import jax.numpy as jnp
import numpy as np
from jax import lax
from jax.experimental import pallas as pl
from jax.experimental.pallas import tpu as pltpu

D_MODEL = 1024
HEADS = 4
HEAD_DIM = 256
SGU_GROUPS = 8
SGU_GROUP_DIM = 128
SGU_CHUNK = 128
FF_DIM = 4096
EPS = 1e-6
Z_COLS = 7 * D_MODEL

MLSTM_CHUNK = 256
STATE_ROWS = HEAD_DIM + 16
STATE_BLOCK = 64
PASS0_GROUP = 8
PASS2_GROUP = 4
HALO = 16
GATE_SLOTS = 8
CONV_ROWS = 64
CONV_COLS = 128
MXU_COLS = 256

VMEM_LIMIT = 56 * 1024 * 1024

_BF = jnp.bfloat16
_F32 = jnp.float32
_NT = (((1,), (1,)), ((), ()))
_LOG2E = 1.4426950408889634


def _const_spec(shape):
    nd = len(shape)
    return pl.BlockSpec(shape, lambda *_: (0,) * nd, pipeline_mode=pl.Buffered(1))


def _sigmoid(x):
    return 1.0 / (1.0 + jnp.exp(-x))


def _log_sigmoid(x):
    return jnp.minimum(x, 0.0) - jnp.log(1.0 + jnp.exp(-jnp.abs(x)))


def _gelu(x):
    return 0.5 * x * (1.0 + lax.erf(x * (2.0 ** -0.5)))


def _mod_kernel(c_ref, w_ref, b_ref, o_ref):
    c = c_ref[...]
    ca = c * _sigmoid(c)
    o_ref[...] = jnp.dot(ca, w_ref[...], precision=lax.Precision.HIGHEST,
                         preferred_element_type=_F32) + b_ref[...]


def _mod_call(c, w_ada, b_ada):
    B, D = c.shape
    N = w_ada.shape[1]
    tn = 1536
    return pl.pallas_call(
        _mod_kernel,
        grid=(N // tn,),
        in_specs=[pl.BlockSpec((B, D), lambda j: (0, 0)),
                  pl.BlockSpec((D, tn), lambda j: (0, j)),
                  pl.BlockSpec((1, tn), lambda j: (0, j))],
        out_specs=pl.BlockSpec((B, tn), lambda j: (0, j)),
        out_shape=jax.ShapeDtypeStruct((B, N), _F32),
        name="mod",
    )(c, w_ada, b_ada.reshape(1, N))


def _lane_scan(x, op, ident, reverse):
    L = x.shape[1]
    lane = lax.broadcasted_iota(jnp.int32, x.shape, 1)
    d = 1
    while d < L:
        if reverse:
            shifted = jnp.where(lane < L - d, pltpu.roll(x, L - d, axis=1), ident)
        else:
            shifted = jnp.where(lane >= d, pltpu.roll(x, d, axis=1), ident)
        x = op(x, shifted)
        d *= 2
    return x


def _in_proj_kernel(x_ref, xp_ref, xn_ref, g_ref, sc_ref, sh_ref, w_ref, wg_ref,
                    bg_ref, cw_ref, cb_ref, z_ref, vt_ref, grow_ref, qk_s, hb_s):
    tm = x_ref.shape[1]
    L = MLSTM_CHUNK
    i = pl.program_id(1)

    gain = g_ref[...] * (1.0 + sc_ref[0])

    def norm_mod(x):
        ms = jnp.mean(x * x, axis=-1, keepdims=True)
        return (x * lax.rsqrt(ms + EPS) * gain + sh_ref[0]).astype(_BF)

    hb_s[HALO:HALO + tm, :] = norm_mod(x_ref[0])
    hb_s[:HALO, :] = jnp.where(i > 0, norm_mod(xp_ref[0]), jnp.zeros((HALO, D_MODEL), _BF))
    hb_s[HALO + tm:, :] = jnp.where(i < pl.num_programs(1) - 1, norm_mod(xn_ref[0]),
                                    jnp.zeros((HALO, D_MODEL), _BF))
    hb = hb_s.at[HALO:HALO + tm, :]

    def plain(c0):
        cols = slice(c0, c0 + MXU_COLS)
        wrows = slice(c0 + D_MODEL, c0 + D_MODEL + MXU_COLS)
        r = lax.dot_general(hb[...], w_ref[wrows, :], _NT, preferred_element_type=_F32)
        if 3 * D_MODEL <= c0 < 5 * D_MODEL:
            r = _gelu(r)
        else:
            r = 1.0 / (1.0 + jnp.exp2(r * -_LOG2E))
        z_ref[0, :, cols] = r.astype(_BF)

    def conv_matmul(c0):
        qk_s[:, c0:c0 + MXU_COLS] = lax.dot_general(
            hb_s[...], w_ref[c0:c0 + MXU_COLS, :], _NT, preferred_element_type=_F32)

    def conv_silu(c0):
        n = CONV_ROWS + 16
        mid = slice(8, 8 + CONV_ROWS)
        for c1 in range(c0, c0 + MXU_COLS, CONV_COLS):
            cols = slice(c1, c1 + CONV_COLS)
            for r0 in range(0, tm, CONV_ROWS):
                zb = qk_s[pl.ds(HALO - 8 + r0, n), cols]
                y = cb_ref[:, cols] + pltpu.roll(zb, 1, axis=0)[mid] * cw_ref[0:1, cols]
                y = y + zb[mid] * cw_ref[1:2, cols]
                y = y + pltpu.roll(zb, n - 1, axis=0)[mid] * cw_ref[2:3, cols]
                y = y / (1.0 + jnp.exp2(y * -_LOG2E))
                if c1 < D_MODEL:
                    y = y * (HEAD_DIM ** -0.5)
                z_ref[0, pl.ds(r0, CONV_ROWS), cols] = y.astype(_BF)

    def gates_pre():
        gc = lax.dot_general(hb[...], wg_ref[...], _NT,
                             preferred_element_type=_F32) + bg_ref[...]
        return gc.T

    def gates_row(gt):
        nr = HEADS * GATE_SLOTS
        lst = _log_sigmoid(gt[:nr])
        git = gt[nr:2 * nr]
        slot = lax.broadcasted_iota(jnp.int32, (nr, L), 0) & (GATE_SLOTS - 1)
        fwd = slot < 3
        for c in range(tm // L):
            cols = slice(c * L, (c + 1) * L)
            lsc = lst[:, cols]
            cum = jnp.where(fwd, _lane_scan(lsc, jnp.add, 0.0, False),
                            _lane_scan(lsc, jnp.add, 0.0, True))
            base = jnp.where((slot == 0) | (slot == 3), cum, git[:, cols] - cum)
            cmax = jnp.where(fwd, _lane_scan(base, jnp.maximum, -jnp.inf, False),
                             _lane_scan(base, jnp.maximum, -jnp.inf, True))
            grow_ref[0, :, cols] = jnp.where((slot == 2) | (slot == 5), cmax, base)

    gt = gates_pre()
    conv_matmul(0)
    for c0 in range(MXU_COLS, 2 * D_MODEL, MXU_COLS):
        conv_matmul(c0)
        conv_silu(c0 - MXU_COLS)
    plain(2 * D_MODEL)
    conv_silu(2 * D_MODEL - MXU_COLS)
    gates_row(gt)
    for c0 in range(2 * D_MODEL + MXU_COLS, Z_COLS, MXU_COLS):
        plain(c0)
    vt_ref[0] = lax.dot_general(w_ref[2 * D_MODEL:3 * D_MODEL, :], hb[...], _NT,
                                preferred_element_type=_F32).astype(_BF)


def _in_proj_call(x, norm_g, sc, sh, w_bf, wg_bf, bias_g, conv_w, conv_b, tm):
    B, S, D = x.shape
    r = tm // HALO
    last = S // HALO - 1
    return pl.pallas_call(
        _in_proj_kernel,
        grid=(B, S // tm),
        in_specs=[pl.BlockSpec((1, tm, D), lambda b, i: (b, i, 0)),
                  pl.BlockSpec((1, HALO, D), lambda b, i: (b, jnp.maximum(i * r - 1, 0), 0)),
                  pl.BlockSpec((1, HALO, D),
                               lambda b, i: (b, jnp.minimum((i + 1) * r, last), 0)),
                  _const_spec((1, D)),
                  pl.BlockSpec((1, 1, D), lambda b, i: (b, 0, 0)),
                  pl.BlockSpec((1, 1, D), lambda b, i: (b, 0, 0)),
                  _const_spec(w_bf.shape),
                  _const_spec((128, D)),
                  _const_spec((1, 128)),
                  _const_spec((3, 2 * D)),
                  _const_spec((1, 2 * D))],
        out_specs=[pl.BlockSpec((1, tm, Z_COLS), lambda b, i: (b, i, 0)),
                   pl.BlockSpec((1, D, tm), lambda b, i: (b, 0, i)),
                   pl.BlockSpec((1, HEADS * GATE_SLOTS, tm), lambda b, i: (b, 0, i))],
        out_shape=[jax.ShapeDtypeStruct((B, S, Z_COLS), _BF),
                   jax.ShapeDtypeStruct((B, D, S), _BF),
                   jax.ShapeDtypeStruct((B, HEADS * GATE_SLOTS, S), _F32)],
        scratch_shapes=[pltpu.VMEM((tm + 2 * HALO, 2 * D), _F32),
                        pltpu.VMEM((tm + 2 * HALO, D), _BF)],
        compiler_params=pltpu.CompilerParams(
            dimension_semantics=("arbitrary", "arbitrary"),
            vmem_limit_bytes=VMEM_LIMIT),
        name="in_proj",
    )(x, x, x, norm_g, sc, sh, w_bf, wg_bf, bias_g, conv_w, conv_b)


def _mlstm_kernel(q_ref, k_ref, vt_ref, zo_ref, gr_ref, mhg_ref,
                  y_ref,
                  pvf_s, pvb_s, u_s, snap_s, ct_s, mh_s, dec_s, ht_s):
    S = q_ref.shape[1]
    L = MLSTM_CHUNK
    NC = S // L
    dh = HEAD_DIM
    R = STATE_ROWS
    K0 = (0, 3)
    LAST = (L - 1, 0)

    def gate_rows(gr, d):
        k0 = K0[d]
        return gr[k0:k0 + 1], gr[k0 + 1:k0 + 2], gr[k0 + 2:k0 + 3]

    m = [jnp.zeros((1, 1), _F32), jnp.zeros((1, 1), _F32)]
    for i in range(NC):
        for d, c in enumerate((i, NC - 1 - i)):
            b_row, _, cmax_row = gate_rows(gr_ref[0, 0, :, c * L:(c + 1) * L], d)
            a_last = jnp.maximum(m[d], cmax_row[:, LAST[d]:LAST[d] + 1])
            mh_s[d * NC + c:d * NC + c + 1, :] = jnp.broadcast_to(m[d], (1, 128))
            dec_s[d * NC + c:d * NC + c + 1, :] = jnp.broadcast_to(jnp.exp(m[d] - a_last),
                                                                  (1, 128))
            m[d] = b_row[:, LAST[d]:LAST[d] + 1] + a_last

    si = lax.broadcasted_iota(jnp.int32, (L, L), 0)
    ti = lax.broadcasted_iota(jnp.int32, (L, L), 1)
    masks = (si <= ti, si >= ti)
    ones_row = (lax.broadcasted_iota(jnp.int32, (R - dh, L), 0) == 0).astype(_BF)

    def scores(c):
        t0 = pl.multiple_of(c * L, L)
        return lax.dot_general(k_ref[0, pl.ds(t0, L), :], q_ref[0, pl.ds(t0, L), :], _NT,
                               preferred_element_type=_F32)

    def pass0(g, carry):
        s_next = scores(g * PASS0_GROUP)
        for k in range(PASS0_GROUP):
            c = g * PASS0_GROUP + k
            t0 = pl.multiple_of(c * L, L)
            s_t = s_next
            vta = jnp.concatenate([vt_ref[0, :, pl.ds(t0, L)], ones_row], axis=0)
            vta32 = vta.astype(_F32)
            gr = gr_ref[0, 0, :, pl.ds(t0, L)]
            a_rows, vtw = [], []
            for d in range(2):
                _, r_row, cmax_row = gate_rows(gr, d)
                a_rows.append(jnp.maximum(mh_s[pl.ds(d * NC + c, 1), 0:1], cmax_row))
                w_row = jnp.exp(r_row - a_rows[d][:, LAST[d]:LAST[d] + 1])
                vtw.append((vta32 * w_row).astype(_BF))
            u_s[c] = jnp.dot(jnp.concatenate(vtw, axis=0), k_ref[0, pl.ds(t0, L), :],
                             preferred_element_type=_F32).astype(_BF)
            if k + 1 < PASS0_GROUP:
                s_next = scores(c + 1)
            p = []
            for d in range(2):
                r_row = gate_rows(gr, d)[1]
                r_rep = jnp.broadcast_to(r_row, (128, L)).T
                r_col = jnp.concatenate([r_rep] * (L // 128), axis=1)
                p.append((jnp.exp(jnp.where(masks[d], r_col - a_rows[d], -jnp.inf))
                          * s_t).astype(_BF))
            pv = jnp.dot(vta, jnp.concatenate(p, axis=1), preferred_element_type=_F32)
            pvf_s[:, pl.ds(t0, L)] = pv[:, :L]
            pvb_s[:, pl.ds(t0, L)] = pv[:, L:]
        return carry

    lax.fori_loop(0, NC // PASS0_GROUP, pass0, 0)

    ct_s[...] = jnp.zeros_like(ct_s)

    def pass1(i, carry):
        for d, c in enumerate((i, NC - 1 - i)):
            decay = dec_s[pl.ds(d * NC + c, 1), 0:1]
            for r0 in range(d * R, (d + 1) * R, STATE_BLOCK):
                rows = slice(r0, min(r0 + STATE_BLOCK, (d + 1) * R))
                ct = ct_s[rows, :]
                snap_s[c, rows, :] = ct.astype(_BF)
                ct_s[rows, :] = decay * ct + u_s[c, rows, :].astype(_F32)
        return carry

    lax.fori_loop(0, NC, pass1, 0, unroll=2)

    def inter_term(c):
        t0 = pl.multiple_of(c * L, L)
        return lax.dot_general(snap_s[c], q_ref[0, pl.ds(t0, L), :], _NT,
                               preferred_element_type=_F32)

    def pass2(g, carry):
        inter_next = inter_term(g * PASS2_GROUP)
        for k in range(PASS2_GROUP):
            c = g * PASS2_GROUP + k
            t0 = pl.multiple_of(c * L, L)
            inter = inter_next
            if k + 1 < PASS2_GROUP:
                inter_next = inter_term(c + 1)
            gr = gr_ref[0, 0, :, pl.ds(t0, L)]
            w_inter, scale = [], []
            for d, pv_s in enumerate((pvf_s, pvb_s)):
                b_row, _, cmax_row = gate_rows(gr, d)
                m_c = mh_s[pl.ds(d * NC + c, 1), 0:1]
                a_row = jnp.maximum(m_c, cmax_row)
                w_inter.append(jnp.exp(m_c - a_row))
                den = (w_inter[d] * inter[d * R + dh:d * R + dh + 1]
                       + pv_s[dh:dh + 1, pl.ds(t0, L)])
                scale.append(1.0 / jnp.maximum(jnp.abs(den), jnp.exp(-(b_row + a_row))))
            ss = jnp.zeros((1, L), _F32)
            for e0 in range(0, dh, STATE_BLOCK):
                blk = None
                for d, pv_s in enumerate((pvf_s, pvb_s)):
                    num = (w_inter[d] * inter[d * R + e0:d * R + e0 + STATE_BLOCK]
                           + pv_s[e0:e0 + STATE_BLOCK, pl.ds(t0, L)])
                    blk = num * scale[d] if blk is None else blk + num * scale[d]
                ss = ss + jnp.sum(blk * blk, axis=0, keepdims=True)
                ht_s[e0:e0 + STATE_BLOCK, :] = blk
            rs = lax.rsqrt(ss * (1.0 / dh) + EPS)
            for t1 in range(0, L, 128):
                h = (ht_s[:, t1:t1 + 128] * rs[:, t1:t1 + 128]).T
                tq = pl.multiple_of(t0 + t1, 128)
                o = zo_ref[0, pl.ds(tq, 128), :].astype(_F32)
                y_ref[0, pl.ds(tq, 128), :] = (h * mhg_ref[...] * o).astype(_BF)
        return carry

    lax.fori_loop(0, NC // PASS2_GROUP, pass2, 0)


def _mlstm_call(z, vt, grow, mh_g):
    B, S, _ = z.shape
    dh = HEAD_DIM
    H = HEADS
    nc = S // MLSTM_CHUNK

    def zspec(off):
        return pl.BlockSpec((1, S, dh), lambda b, h, off=off: (b, 0, off + h))

    return pl.pallas_call(
        _mlstm_kernel,
        grid=(B, H),
        in_specs=[zspec(0), zspec(H),
                  pl.BlockSpec((1, dh, S), lambda b, h: (b, h, 0)),
                  zspec(2 * H),
                  pl.BlockSpec((1, 1, 8, S), lambda b, h: (b, h, 0, 0)),
                  pl.BlockSpec((1, dh), lambda b, h: (0, h))],
        out_specs=pl.BlockSpec((1, S, dh), lambda b, h: (b, 0, h)),
        out_shape=jax.ShapeDtypeStruct((B, S, H * dh), _BF),
        scratch_shapes=[pltpu.VMEM((STATE_ROWS, S), _F32), pltpu.VMEM((STATE_ROWS, S), _F32),
                        pltpu.VMEM((nc, 2 * STATE_ROWS, dh), _BF),
                        pltpu.VMEM((nc, 2 * STATE_ROWS, dh), _BF),
                        pltpu.VMEM((2 * STATE_ROWS, dh), _F32),
                        pltpu.VMEM((2 * nc, 128), _F32), pltpu.VMEM((2 * nc, 128), _F32),
                        pltpu.VMEM((dh, MLSTM_CHUNK), _F32)],
        compiler_params=pltpu.CompilerParams(
            dimension_semantics=("arbitrary", "arbitrary"),
            vmem_limit_bytes=VMEM_LIMIT),
        name="mlstm",
    )(z, z, vt, z, grow, mh_g)


def _mix_kernel(x_ref, zu_ref, zv_ref, zga_ref, zgb_ref, ya_ref, g1_ref, lng_ref, lnb_ref,
                ws_ref, bs_ref, wo_ref, o_ref, mix_s):
    tm = x_ref.shape[1]
    P = SGU_CHUNK
    C = SGU_GROUP_DIM
    for j in range(tm // P):
        rows = slice(j * P, (j + 1) * P)
        gv = zv_ref[0, rows, :].astype(_F32)
        mu = jnp.mean(gv, axis=-1, keepdims=True)
        dv = gv - mu
        var = jnp.mean(dv * dv, axis=-1, keepdims=True)
        vn = (dv * lax.rsqrt(var + EPS) * lng_ref[...] + lnb_ref[...]).astype(_BF)
        s = jnp.concatenate(
            [jnp.dot(ws_ref[g], vn[:, g * C:(g + 1) * C], preferred_element_type=_F32)
             for g in range(SGU_GROUPS)], axis=1) + bs_ref[...]
        yb = zu_ref[0, rows, :].astype(_F32) * s
        ga = zga_ref[0, rows, :].astype(_F32)
        gb = zgb_ref[0, rows, :].astype(_F32)
        mix_s[rows, :] = (ga * ya_ref[0, rows, :].astype(_F32) + gb * yb).astype(_BF)
    upd = jnp.dot(mix_s[...], wo_ref[...], preferred_element_type=_F32)
    o_ref[0] = x_ref[0] + g1_ref[0] * upd


def _mix_call(x, z, ya, g1, ln_g, ln_b, ws_bf, bs_full, wo_bf, tm):
    B, S, D = x.shape

    def zspec(j):
        return pl.BlockSpec((1, tm, D), lambda b, i, j=j: (b, i, j))

    return pl.pallas_call(
        _mix_kernel,
        grid=(B, S // tm),
        in_specs=[zspec(0), zspec(3), zspec(4), zspec(5), zspec(6), zspec(0),
                  pl.BlockSpec((1, 1, D), lambda b, i: (b, 0, 0)),
                  _const_spec((1, D)), _const_spec((1, D)),
                  _const_spec((SGU_GROUPS, SGU_CHUNK, SGU_CHUNK)),
                  _const_spec((SGU_CHUNK, D)),
                  _const_spec((D, D))],
        out_specs=pl.BlockSpec((1, tm, D), lambda b, i: (b, i, 0)),
        out_shape=jax.ShapeDtypeStruct((B, S, D), _F32),
        scratch_shapes=[pltpu.VMEM((tm, D), _BF)],
        compiler_params=pltpu.CompilerParams(
            dimension_semantics=("arbitrary", "arbitrary"),
            vmem_limit_bytes=VMEM_LIMIT),
        name="mix",
    )(x, z, z, z, z, ya, g1, ln_g, ln_b, ws_bf, bs_full, wo_bf)


def _ffn_kernel(x_ref, g_ref, sc_ref, sh_ref, g2_ref, w1_ref, w2_ref, nf_ref, o_ref):
    x = x_ref[0]
    ms = jnp.mean(x * x, axis=-1, keepdims=True)
    gain = g_ref[...] * (1.0 + sc_ref[0])
    hb = (x * lax.rsqrt(ms + EPS) * gain + sh_ref[0]).astype(_BF)
    ff = jnp.zeros(x.shape, _F32)
    for j in range(FF_DIM // D_MODEL):
        cols = slice(j * D_MODEL, (j + 1) * D_MODEL)
        a = jnp.maximum(jnp.dot(hb, w1_ref[:, cols], preferred_element_type=_F32), 0.0)
        ff = ff + jnp.dot((a * a).astype(_BF), w2_ref[cols, :], preferred_element_type=_F32)
    x2 = x + g2_ref[0] * ff
    ms2 = jnp.mean(x2 * x2, axis=-1, keepdims=True)
    o_ref[0] = x2 * lax.rsqrt(ms2 + EPS) * nf_ref[...]


def _ffn_call(x1, norm_g, sc, sh, g2, w1_bf, w2_bf, normf_g, tm):
    B, S, D = x1.shape
    vec = pl.BlockSpec((1, 1, D), lambda b, i: (b, 0, 0))
    return pl.pallas_call(
        _ffn_kernel,
        grid=(B, S // tm),
        in_specs=[pl.BlockSpec((1, tm, D), lambda b, i: (b, i, 0)),
                  _const_spec((1, D)), vec, vec, vec,
                  _const_spec((D, FF_DIM)), _const_spec((FF_DIM, D)),
                  _const_spec((1, D))],
        out_specs=pl.BlockSpec((1, tm, D), lambda b, i: (b, i, 0)),
        out_shape=jax.ShapeDtypeStruct((B, S, D), _F32),
        compiler_params=pltpu.CompilerParams(
            dimension_semantics=("arbitrary", "arbitrary"),
            vmem_limit_bytes=VMEM_LIMIT),
        name="ffn",
    )(x1, norm_g, sc, sh, g2, w1_bf, w2_bf, normf_g)


def _gate_weights(w_if_t, b_if):
    H = HEADS
    idx_f, idx_i = [], []
    for h in range(H):
        idx_f += [H + h] * 3 + [3 * H + h] * 3 + [None] * 2
        idx_i += [None, h, h, None, 2 * H + h, 2 * H + h, None, None]
    sel = np.zeros((128, 4 * H), np.float32)
    for row, src in enumerate(idx_f + idx_i):
        if src is not None:
            sel[row, src] = 1.0
    w_g = (sel[:, :, None] * w_if_t[None]).sum(axis=1)
    b_g = (sel * b_if.reshape(1, -1)).sum(axis=1).reshape(1, 128)
    return w_g, b_g


def kernel(x, c, w_ada, b_ada, norm1_g, norm2_g, w_in, b_if, conv_w, conv_b, mh_g,
           ln_v_g, ln_v_b, w_s, b_s, w_out, w1, w2, normf_g):
    B, S, D = x.shape
    H = HEADS
    assert w_ada.shape[0] == 1, "single layer"

    mod = _mod_call(c, w_ada[0], b_ada[0])
    sh1, sc1, g1, sh2, sc2, g2 = (mod[:, k * D:(k + 1) * D].reshape(B, 1, D) for k in range(6))

    w_in_t = w_in[0].T
    w_g, b_g = _gate_weights(w_in_t[8 * D:], b_if[0])
    z, vt, grow = _in_proj_call(x, norm1_g, sc1, sh1, w_in_t.astype(_BF), w_g.astype(_BF), b_g,
                                conv_w[0], conv_b, tm=512)

    ya = _mlstm_call(z, vt, grow.reshape(B, H, GATE_SLOTS, S), mh_g)

    bs_full = jnp.repeat(b_s[0].T, SGU_GROUP_DIM, axis=1)
    x1 = _mix_call(x, z, ya, g1, ln_v_g, ln_v_b, w_s[0].astype(_BF), bs_full,
                   w_out[0].astype(_BF), tm=1024)

    return _ffn_call(x1, norm2_g, sc2, sh2, g2, w1[0].astype(_BF), w2[0].astype(_BF),
                     normf_g.reshape(1, D), tm=1024)
```

```python
import jax
import jax.numpy as jnp
import numpy as np
from jax import lax
from jax.experimental import pallas as pl
from jax.experimental.pallas import tpu as pltpu

D_MODEL = 1024
HEADS = 4
HEAD_DIM = 256
SGU_GROUPS = 8
SGU_GROUP_DIM = 128
SGU_CHUNK = 128
FF_DIM = 4096
EPS = 1e-6
Z_COLS = 7 * D_MODEL

MLSTM_CHUNK = 256
STATE_ROWS = HEAD_DIM + 16
STATE_BLOCK = 64
PASS0_GROUP = 8
PASS2_GROUP = 4
HALO = 16
GATE_SLOTS = 8
CONV_ROWS = 64
CONV_COLS = 128
MXU_COLS = 256

VMEM_LIMIT = 56 * 1024 * 1024

_BF = jnp.bfloat16
_F32 = jnp.float32
_NT = (((1,), (1,)), ((), ()))
_LOG2E = 1.4426950408889634


def _const_spec(shape):
    nd = len(shape)
    return pl.BlockSpec(shape, lambda *_: (0,) * nd, pipeline_mode=pl.Buffered(1))


def _sigmoid(x):
    return 1.0 / (1.0 + jnp.exp(-x))


def _log_sigmoid(x):
    return jnp.minimum(x, 0.0) - jnp.log(1.0 + jnp.exp(-jnp.abs(x)))


def _gelu(x):
    return 0.5 * x * (1.0 + lax.erf(x * (2.0 ** -0.5)))


def _mod_kernel(c_ref, w_ref, b_ref, o_ref):
    c = c_ref[...]
    ca = c * _sigmoid(c)
    o_ref[...] = jnp.dot(ca, w_ref[...], precision=lax.Precision.HIGHEST,
                         preferred_element_type=_F32) + b_ref[...]


def _mod_call(c, w_ada, b_ada):
    B, D = c.shape
    N = w_ada.shape[1]
    tn = 1536
    return pl.pallas_call(
        _mod_kernel,
        grid=(N // tn,),
        in_specs=[pl.BlockSpec((B, D), lambda j: (0, 0)),
                  pl.BlockSpec((D, tn), lambda j: (0, j)),
                  pl.BlockSpec((1, tn), lambda j: (0, j))],
        out_specs=pl.BlockSpec((B, tn), lambda j: (0, j)),
        out_shape=jax.ShapeDtypeStruct((B, N), _F32),
        name="mod",
    )(c, w_ada, b_ada.reshape(1, N))


def _lane_scan(x, op, ident, reverse):
    L = x.shape[1]
    lane = lax.broadcasted_iota(jnp.int32, x.shape, 1)
    d = 1
    while d < L:
        if reverse:
            shifted = jnp.where(lane < L - d, pltpu.roll(x, L - d, axis=1), ident)
        else:
            shifted = jnp.where(lane >= d, pltpu.roll(x, d, axis=1), ident)
        x = op(x, shifted)
        d *= 2
    return x


def _in_proj_kernel(x_ref, xp_ref, xn_ref, g_ref, sc_ref, sh_ref, w_ref, wg_ref,
                    bg_ref, cw_ref, cb_ref, z_ref, vt_ref, grow_ref, qk_s, hb_s):
    tm = x_ref.shape[1]
    L = MLSTM_CHUNK
    i = pl.program_id(1)

    gain = g_ref[...] * (1.0 + sc_ref[0])

    def norm_mod(x):
        ms = jnp.mean(x * x, axis=-1, keepdims=True)
        return (x * lax.rsqrt(ms + EPS) * gain + sh_ref[0]).astype(_BF)

    hb_s[HALO:HALO + tm, :] = norm_mod(x_ref[0])
    hb_s[:HALO, :] = jnp.where(i > 0, norm_mod(xp_ref[0]), jnp.zeros((HALO, D_MODEL), _BF))
    hb_s[HALO + tm:, :] = jnp.where(i < pl.num_programs(1) - 1, norm_mod(xn_ref[0]),
                                    jnp.zeros((HALO, D_MODEL), _BF))
    hb = hb_s.at[HALO:HALO + tm, :]

    def plain(c0):
        cols = slice(c0, c0 + MXU_COLS)
        wrows = slice(c0 + D_MODEL, c0 + D_MODEL + MXU_COLS)
        r = lax.dot_general(hb[...], w_ref[wrows, :], _NT, preferred_element_type=_F32)
        if 3 * D_MODEL <= c0 < 5 * D_MODEL:
            r = _gelu(r)
        else:
            r = 1.0 / (1.0 + jnp.exp2(r * -_LOG2E))
        z_ref[0, :, cols] = r.astype(_BF)

    def conv_matmul(c0):
        qk_s[:, c0:c0 + MXU_COLS] = lax.dot_general(
            hb_s[...], w_ref[c0:c0 + MXU_COLS, :], _NT, preferred_element_type=_F32)

    def conv_silu(c0):
        n = CONV_ROWS + 16
        mid = slice(8, 8 + CONV_ROWS)
        for c1 in range(c0, c0 + MXU_COLS, CONV_COLS):
            cols = slice(c1, c1 + CONV_COLS)
            for r0 in range(0, tm, CONV_ROWS):
                zb = qk_s[pl.ds(HALO - 8 + r0, n), cols]
                y = cb_ref[:, cols] + pltpu.roll(zb, 1, axis=0)[mid] * cw_ref[0:1, cols]
                y = y + zb[mid] * cw_ref[1:2, cols]
                y = y + pltpu.roll(zb, n - 1, axis=0)[mid] * cw_ref[2:3, cols]
                y = y / (1.0 + jnp.exp2(y * -_LOG2E))
                if c1 < D_MODEL:
                    y = y * (HEAD_DIM ** -0.5)
                z_ref[0, pl.ds(r0, CONV_ROWS), cols] = y.astype(_BF)

    def gates_pre():
        gc = lax.dot_general(hb[...], wg_ref[...], _NT,
                             preferred_element_type=_F32) + bg_ref[...]
        return gc.T

    def gates_row(gt):
        nr = HEADS * GATE_SLOTS
        lst = _log_sigmoid(gt[:nr])
        git = gt[nr:2 * nr]
        slot = lax.broadcasted_iota(jnp.int32, (nr, L), 0) & (GATE_SLOTS - 1)
        fwd = slot < 3
        for c in range(tm // L):
            cols = slice(c * L, (c + 1) * L)
            lsc = lst[:, cols]
            cum = jnp.where(fwd, _lane_scan(lsc, jnp.add, 0.0, False),
                            _lane_scan(lsc, jnp.add, 0.0, True))
            base = jnp.where((slot == 0) | (slot == 3), cum, git[:, cols] - cum)
            cmax = jnp.where(fwd, _lane_scan(base, jnp.maximum, -jnp.inf, False),
                             _lane_scan(base, jnp.maximum, -jnp.inf, True))
            grow_ref[0, :, cols] = jnp.where((slot == 2) | (slot == 5), cmax, base)

    gt = gates_pre()
    conv_matmul(0)
    for c0 in range(MXU_COLS, 2 * D_MODEL, MXU_COLS):
        conv_matmul(c0)
        conv_silu(c0 - MXU_COLS)
    plain(2 * D_MODEL)
    conv_silu(2 * D_MODEL - MXU_COLS)
    gates_row(gt)
    for c0 in range(2 * D_MODEL + MXU_COLS, Z_COLS, MXU_COLS):
        plain(c0)
    vt_ref[0] = lax.dot_general(w_ref[2 * D_MODEL:3 * D_MODEL, :], hb[...], _NT,
                                preferred_element_type=_F32).astype(_BF)


def _in_proj_call(x, norm_g, sc, sh, w_bf, wg_bf, bias_g, conv_w, conv_b, tm):
    B, S, D = x.shape
    r = tm // HALO
    last = S // HALO - 1
    return pl.pallas_call(
        _in_proj_kernel,
        grid=(B, S // tm),
        in_specs=[pl.BlockSpec((1, tm, D), lambda b, i: (b, i, 0)),
                  pl.BlockSpec((1, HALO, D), lambda b, i: (b, jnp.maximum(i * r - 1, 0), 0)),
                  pl.BlockSpec((1, HALO, D),
                               lambda b, i: (b, jnp.minimum((i + 1) * r, last), 0)),
                  _const_spec((1, D)),
                  pl.BlockSpec((1, 1, D), lambda b, i: (b, 0, 0)),
                  pl.BlockSpec((1, 1, D), lambda b, i: (b, 0, 0)),
                  _const_spec(w_bf.shape),
                  _const_spec((128, D)),
                  _const_spec((1, 128)),
                  _const_spec((3, 2 * D)),
                  _const_spec((1, 2 * D))],
        out_specs=[pl.BlockSpec((1, tm, Z_COLS), lambda b, i: (b, i, 0)),
                   pl.BlockSpec((1, D, tm), lambda b, i: (b, 0, i)),
                   pl.BlockSpec((1, HEADS * GATE_SLOTS, tm), lambda b, i: (b, 0, i))],
        out_shape=[jax.ShapeDtypeStruct((B, S, Z_COLS), _BF),
                   jax.ShapeDtypeStruct((B, D, S), _BF),
                   jax.ShapeDtypeStruct((B, HEADS * GATE_SLOTS, S), _F32)],
        scratch_shapes=[pltpu.VMEM((tm + 2 * HALO, 2 * D), _F32),
                        pltpu.VMEM((tm + 2 * HALO, D), _BF)],
        compiler_params=pltpu.CompilerParams(
            dimension_semantics=("arbitrary", "arbitrary"),
            vmem_limit_bytes=VMEM_LIMIT),
        name="in_proj",
    )(x, x, x, norm_g, sc, sh, w_bf, wg_bf, bias_g, conv_w, conv_b)


def _mlstm_kernel(q_ref, k_ref, vt_ref, zo_ref, gr_ref, mhg_ref,
                  y_ref,
                  pvf_s, pvb_s, u_s, snap_s, ct_s, mh_s, dec_s, ht_s):
    S = q_ref.shape[1]
    L = MLSTM_CHUNK
    NC = S // L
    dh = HEAD_DIM
    R = STATE_ROWS
    K0 = (0, 3)
    LAST = (L - 1, 0)
    NB = R - dh

    def n_rows(d):
        return slice(d * NB, (d + 1) * NB)

    def c_rows(d, e0):
        start = 2 * NB + 2 * e0 + d * STATE_BLOCK
        return slice(start, start + STATE_BLOCK)

    def state_blocks(d):
        return [n_rows(d)] + [c_rows(d, e0) for e0 in range(0, dh, STATE_BLOCK)]

    def gate_rows(gr, d):
        k0 = K0[d]
        return gr[k0:k0 + 1], gr[k0 + 1:k0 + 2], gr[k0 + 2:k0 + 3]

    m = [jnp.zeros((1, 1), _F32), jnp.zeros((1, 1), _F32)]
    for i in range(NC):
        for d, c in enumerate((i, NC - 1 - i)):
            b_row, _, cmax_row = gate_rows(gr_ref[0, 0, :, c * L:(c + 1) * L], d)
            a_last = jnp.maximum(m[d], cmax_row[:, LAST[d]:LAST[d] + 1])
            mh_s[d * NC + c:d * NC + c + 1, :] = jnp.broadcast_to(m[d], (1, 128))
            dec_s[d * NC + c:d * NC + c + 1, :] = jnp.broadcast_to(jnp.exp(m[d] - a_last),
                                                                  (1, 128))
            m[d] = b_row[:, LAST[d]:LAST[d] + 1] + a_last

    si = lax.broadcasted_iota(jnp.int32, (L, L), 0)
    ti = lax.broadcasted_iota(jnp.int32, (L, L), 1)
    masks = (si <= ti, si >= ti)
    ones_row = (lax.broadcasted_iota(jnp.int32, (R - dh, L), 0) == 0).astype(_BF)

    def scores(c):
        t0 = pl.multiple_of(c * L, L)
        return lax.dot_general(k_ref[0, pl.ds(t0, L), :], q_ref[0, pl.ds(t0, L), :], _NT,
                               preferred_element_type=_F32)

    def pass0(g, carry):
        s_next = scores(g * PASS0_GROUP)
        for k in range(PASS0_GROUP):
            c = g * PASS0_GROUP + k
            t0 = pl.multiple_of(c * L, L)
            s_t = s_next
            vta = jnp.concatenate([vt_ref[0, :, pl.ds(t0, L)], ones_row], axis=0)
            vta32 = vta.astype(_F32)
            gr = gr_ref[0, 0, :, pl.ds(t0, L)]
            a_rows, vtw = [], []
            for d in range(2):
                _, r_row, cmax_row = gate_rows(gr, d)
                a_rows.append(jnp.maximum(mh_s[pl.ds(d * NC + c, 1), 0:1], cmax_row))
                w_row = jnp.exp(r_row - a_rows[d][:, LAST[d]:LAST[d] + 1])
                vtw.append((vta32 * w_row).astype(_BF))
            stacked = [vtw[d][dh:] for d in range(2)]
            for e0 in range(0, dh, STATE_BLOCK):
                stacked += [vtw[d][e0:e0 + STATE_BLOCK] for d in range(2)]
            u_s[c] = jnp.dot(jnp.concatenate(stacked, axis=0), k_ref[0, pl.ds(t0, L), :],
                             preferred_element_type=_F32).astype(_BF)
            if k + 1 < PASS0_GROUP:
                s_next = scores(c + 1)
            p = []
            for d in range(2):
                r_row = gate_rows(gr, d)[1]
                r_rep = jnp.broadcast_to(r_row, (128, L)).T
                r_col = jnp.concatenate([r_rep] * (L // 128), axis=1)
                p.append((jnp.exp(jnp.where(masks[d], r_col - a_rows[d], -jnp.inf))
                          * s_t).astype(_BF))
            pv = jnp.dot(vta, jnp.concatenate(p, axis=1), preferred_element_type=_F32)
            pvf_s[:, pl.ds(t0, L)] = pv[:, :L]
            pvb_s[:, pl.ds(t0, L)] = pv[:, L:]
        return carry

    lax.fori_loop(0, NC // PASS0_GROUP, pass0, 0)

    ct_s[...] = jnp.zeros_like(ct_s)

    def pass1(i, carry):
        for d, c in enumerate((i, NC - 1 - i)):
            decay = dec_s[pl.ds(d * NC + c, 1), 0:1]
            for rows in state_blocks(d):
                ct = ct_s[rows, :]
                snap_s[c, rows, :] = ct.astype(_BF)
                ct_s[rows, :] = decay * ct + u_s[c, rows, :].astype(_F32)
        return carry

    lax.fori_loop(0, NC, pass1, 0, unroll=2)

    def inter_term(c):
        t0 = pl.multiple_of(c * L, L)
        return lax.dot_general(snap_s[c], q_ref[0, pl.ds(t0, L), :], _NT,
                               preferred_element_type=_F32)

    def pass2(g, carry):
        inter_next = inter_term(g * PASS2_GROUP)
        for k in range(PASS2_GROUP):
            c = g * PASS2_GROUP + k
            t0 = pl.multiple_of(c * L, L)
            inter = inter_next
            if k + 1 < PASS2_GROUP:
                inter_next = inter_term(c + 1)
            gr = gr_ref[0, 0, :, pl.ds(t0, L)]
            w_inter, scale = [], []
            for d, pv_s in enumerate((pvf_s, pvb_s)):
                b_row, _, cmax_row = gate_rows(gr, d)
                m_c = mh_s[pl.ds(d * NC + c, 1), 0:1]
                a_row = jnp.maximum(m_c, cmax_row)
                w_inter.append(jnp.exp(m_c - a_row))
                den = (w_inter[d] * inter[d * NB:d * NB + 1]
                       + pv_s[dh:dh + 1, pl.ds(t0, L)])
                scale.append(1.0 / jnp.maximum(jnp.abs(den), jnp.exp(-(b_row + a_row))))
            ss = jnp.zeros((1, L), _F32)
            for e0 in range(0, dh, STATE_BLOCK):
                blk = None
                for d, pv_s in enumerate((pvf_s, pvb_s)):
                    num = (w_inter[d] * inter[c_rows(d, e0)]
                           + pv_s[e0:e0 + STATE_BLOCK, pl.ds(t0, L)])
                    blk = num * scale[d] if blk is None else blk + num * scale[d]
                ss = ss + jnp.sum(blk * blk, axis=0, keepdims=True)
                ht_s[e0:e0 + STATE_BLOCK, :] = blk
            rs = lax.rsqrt(ss * (1.0 / dh) + EPS)
            for t1 in range(0, L, 128):
                h = (ht_s[:, t1:t1 + 128] * rs[:, t1:t1 + 128]).T
                tq = pl.multiple_of(t0 + t1, 128)
                o = zo_ref[0, pl.ds(tq, 128), :].astype(_F32)
                y_ref[0, pl.ds(tq, 128), :] = (h * mhg_ref[...] * o).astype(_BF)
        return carry

    lax.fori_loop(0, NC // PASS2_GROUP, pass2, 0)


def _mlstm_call(z, vt, grow, mh_g):
    B, S, _ = z.shape
    dh = HEAD_DIM
    H = HEADS
    nc = S // MLSTM_CHUNK

    def zspec(off):
        return pl.BlockSpec((1, S, dh), lambda b, h, off=off: (b, 0, off + h))

    return pl.pallas_call(
        _mlstm_kernel,
        grid=(B, H),
        in_specs=[zspec(0), zspec(H),
                  pl.BlockSpec((1, dh, S), lambda b, h: (b, h, 0)),
                  zspec(2 * H),
                  pl.BlockSpec((1, 1, 8, S), lambda b, h: (b, h, 0, 0)),
                  pl.BlockSpec((1, dh), lambda b, h: (0, h))],
        out_specs=pl.BlockSpec((1, S, dh), lambda b, h: (b, 0, h)),
        out_shape=jax.ShapeDtypeStruct((B, S, H * dh), _BF),
        scratch_shapes=[pltpu.VMEM((STATE_ROWS, S), _F32), pltpu.VMEM((STATE_ROWS, S), _F32),
                        pltpu.VMEM((nc, 2 * STATE_ROWS, dh), _BF),
                        pltpu.VMEM((nc, 2 * STATE_ROWS, dh), _BF),
                        pltpu.VMEM((2 * STATE_ROWS, dh), _F32),
                        pltpu.VMEM((2 * nc, 128), _F32), pltpu.VMEM((2 * nc, 128), _F32),
                        pltpu.VMEM((dh, MLSTM_CHUNK), _F32)],
        compiler_params=pltpu.CompilerParams(
            dimension_semantics=("arbitrary", "arbitrary"),
            vmem_limit_bytes=VMEM_LIMIT),
        name="mlstm",
    )(z, z, vt, z, grow, mh_g)


def _mix_kernel(x_ref, zu_ref, zv_ref, zga_ref, zgb_ref, ya_ref, g1_ref, lng_ref, lnb_ref,
                ws_ref, bs_ref, wo_ref, o_ref, mix_s):
    tm = x_ref.shape[1]
    P = SGU_CHUNK
    C = SGU_GROUP_DIM
    for j in range(tm // P):
        rows = slice(j * P, (j + 1) * P)
        gv = zv_ref[0, rows, :].astype(_F32)
        mu = jnp.mean(gv, axis=-1, keepdims=True)
        dv = gv - mu
        var = jnp.mean(dv * dv, axis=-1, keepdims=True)
        vn = (dv * lax.rsqrt(var + EPS) * lng_ref[...] + lnb_ref[...]).astype(_BF)
        s = jnp.concatenate(
            [jnp.dot(ws_ref[g], vn[:, g * C:(g + 1) * C], preferred_element_type=_F32)
             for g in range(SGU_GROUPS)], axis=1) + bs_ref[...]
        yb = zu_ref[0, rows, :].astype(_F32) * s
        ga = zga_ref[0, rows, :].astype(_F32)
        gb = zgb_ref[0, rows, :].astype(_F32)
        mix_s[rows, :] = (ga * ya_ref[0, rows, :].astype(_F32) + gb * yb).astype(_BF)
    upd = jnp.dot(mix_s[...], wo_ref[...], preferred_element_type=_F32)
    o_ref[0] = x_ref[0] + g1_ref[0] * upd


def _mix_call(x, z, ya, g1, ln_g, ln_b, ws_bf, bs_full, wo_bf, tm):
    B, S, D = x.shape

    def zspec(j):
        return pl.BlockSpec((1, tm, D), lambda b, i, j=j: (b, i, j))

    return pl.pallas_call(
        _mix_kernel,
        grid=(B, S // tm),
        in_specs=[zspec(0), zspec(3), zspec(4), zspec(5), zspec(6), zspec(0),
                  pl.BlockSpec((1, 1, D), lambda b, i: (b, 0, 0)),
                  _const_spec((1, D)), _const_spec((1, D)),
                  _const_spec((SGU_GROUPS, SGU_CHUNK, SGU_CHUNK)),
                  _const_spec((SGU_CHUNK, D)),
                  _const_spec((D, D))],
        out_specs=pl.BlockSpec((1, tm, D), lambda b, i: (b, i, 0)),
        out_shape=jax.ShapeDtypeStruct((B, S, D), _F32),
        scratch_shapes=[pltpu.VMEM((tm, D), _BF)],
        compiler_params=pltpu.CompilerParams(
            dimension_semantics=("arbitrary", "arbitrary"),
            vmem_limit_bytes=VMEM_LIMIT),
        name="mix",
    )(x, z, z, z, z, ya, g1, ln_g, ln_b, ws_bf, bs_full, wo_bf)


def _ffn_kernel(x_ref, g_ref, sc_ref, sh_ref, g2_ref, w1_ref, w2_ref, nf_ref, o_ref):
    x = x_ref[0]
    ms = jnp.mean(x * x, axis=-1, keepdims=True)
    gain = g_ref[...] * (1.0 + sc_ref[0])
    hb = (x * lax.rsqrt(ms + EPS) * gain + sh_ref[0]).astype(_BF)
    ff = jnp.zeros(x.shape, _F32)
    for j in range(FF_DIM // D_MODEL):
        cols = slice(j * D_MODEL, (j + 1) * D_MODEL)
        a = jnp.maximum(jnp.dot(hb, w1_ref[:, cols], preferred_element_type=_F32), 0.0)
        ff = ff + jnp.dot((a * a).astype(_BF), w2_ref[cols, :], preferred_element_type=_F32)
    x2 = x + g2_ref[0] * ff
    ms2 = jnp.mean(x2 * x2, axis=-1, keepdims=True)
    o_ref[0] = x2 * lax.rsqrt(ms2 + EPS) * nf_ref[...]


def _ffn_call(x1, norm_g, sc, sh, g2, w1_bf, w2_bf, normf_g, tm):
    B, S, D = x1.shape
    vec = pl.BlockSpec((1, 1, D), lambda b, i: (b, 0, 0))
    return pl.pallas_call(
        _ffn_kernel,
        grid=(B, S // tm),
        in_specs=[pl.BlockSpec((1, tm, D), lambda b, i: (b, i, 0)),
                  _const_spec((1, D)), vec, vec, vec,
                  _const_spec((D, FF_DIM)), _const_spec((FF_DIM, D)),
                  _const_spec((1, D))],
        out_specs=pl.BlockSpec((1, tm, D), lambda b, i: (b, i, 0)),
        out_shape=jax.ShapeDtypeStruct((B, S, D), _F32),
        compiler_params=pltpu.CompilerParams(
            dimension_semantics=("arbitrary", "arbitrary"),
            vmem_limit_bytes=VMEM_LIMIT),
        name="ffn",
    )(x1, norm_g, sc, sh, g2, w1_bf, w2_bf, normf_g)


def _gate_weights(w_if_t, b_if):
    H = HEADS
    idx_f, idx_i = [], []
    for h in range(H):
        idx_f += [H + h] * 3 + [3 * H + h] * 3 + [None] * 2
        idx_i += [None, h, h, None, 2 * H + h, 2 * H + h, None, None]
    sel = np.zeros((128, 4 * H), np.float32)
    for row, src in enumerate(idx_f + idx_i):
        if src is not None:
            sel[row, src] = 1.0
    w_g = (sel[:, :, None] * w_if_t[None]).sum(axis=1)
    b_g = (sel * b_if.reshape(1, -1)).sum(axis=1).reshape(1, 128)
    return w_g, b_g


def kernel(x, c, w_ada, b_ada, norm1_g, norm2_g, w_in, b_if, conv_w, conv_b, mh_g,
           ln_v_g, ln_v_b, w_s, b_s, w_out, w1, w2, normf_g):
    B, S, D = x.shape
    H = HEADS
    assert w_ada.shape[0] == 1, "single layer"

    mod = _mod_call(c, w_ada[0], b_ada[0])
    sh1, sc1, g1, sh2, sc2, g2 = (mod[:, k * D:(k + 1) * D].reshape(B, 1, D) for k in range(6))

    w_in_t = w_in[0].T
    w_g, b_g = _gate_weights(w_in_t[8 * D:], b_if[0])
    z, vt, grow = _in_proj_call(x, norm1_g, sc1, sh1, w_in_t.astype(_BF), w_g.astype(_BF), b_g,
                                conv_w[0], conv_b, tm=512)

    ya = _mlstm_call(z, vt, grow.reshape(B, H, GATE_SLOTS, S), mh_g)

    bs_full = jnp.repeat(b_s[0].T, SGU_GROUP_DIM, axis=1)
    x1 = _mix_call(x, z, ya, g1, ln_v_g, ln_v_b, w_s[0].astype(_BF), bs_full,
                   w_out[0].astype(_BF), tm=1024)

    return _ffn_call(x1, norm2_g, sc2, sh2, g2, w1[0].astype(_BF), w2[0].astype(_BF),
                     normf_g.reshape(1, D), tm=1024)
```

```python
import jax
import jax.numpy as jnp
import numpy as np
from jax import lax
from jax.experimental import pallas as pl
from jax.experimental.pallas import tpu as pltpu

D_MODEL = 1024
HEADS = 4
HEAD_DIM = 256
SGU_GROUPS = 8
SGU_GROUP_DIM = 128
SGU_CHUNK = 128
FF_DIM = 4096
EPS = 1e-6
Z_COLS = 7 * D_MODEL

MLSTM_CHUNK = 256
F32_ROWS = 8
BF16_ROWS = 16
STATE_ROWS = HEAD_DIM + BF16_ROWS
STATE_BLOCK = 64
PASS0_GROUP = 8
PASS2_GROUP = 16
HALO = BF16_ROWS
GATE_SLOTS = 8
CONV_ROWS = 64
CONV_COLS = 128
MXU_COLS = 256
LANES = 128

VMEM_LIMIT = 56 * 1024 * 1024
IN_PROJ_ROWS = 512
MIX_ROWS = 1024
FFN_ROWS = 1024

_BF = jnp.bfloat16
_F32 = jnp.float32
_NT = (((1,), (1,)), ((), ()))
_LOG2E = 1.4426950408889634


def _const_spec(shape):
    nd = len(shape)
    return pl.BlockSpec(shape, lambda *_: (0,) * nd, pipeline_mode=pl.Buffered(1))


def _sigmoid(x):
    return 1.0 / (1.0 + jnp.exp(-x))


def _log_sigmoid(x):
    return jnp.minimum(x, 0.0) - jnp.log(1.0 + jnp.exp(-jnp.abs(x)))


def _gelu(x):
    return 0.5 * x * (1.0 + lax.erf(x * (2.0 ** -0.5)))


def _mod_kernel(c_ref, w_ref, b_ref, o_ref):
    c = c_ref[...]
    ca = c * _sigmoid(c)
    o_ref[...] = jnp.dot(ca, w_ref[...], precision=lax.Precision.HIGHEST,
                         preferred_element_type=_F32) + b_ref[...]


def _mod_call(c, w_ada, b_ada):
    B, D = c.shape
    N = w_ada.shape[1]
    tn = N // 4
    return pl.pallas_call(
        _mod_kernel,
        grid=(N // tn,),
        in_specs=[pl.BlockSpec((B, D), lambda j: (0, 0)),
                  pl.BlockSpec((D, tn), lambda j: (0, j)),
                  pl.BlockSpec((1, tn), lambda j: (0, j))],
        out_specs=pl.BlockSpec((B, tn), lambda j: (0, j)),
        out_shape=jax.ShapeDtypeStruct((B, N), _F32),
        name="mod",
    )(c, w_ada, b_ada.reshape(1, N))


def _lane_scan(x, op, ident, reverse):
    L = x.shape[1]
    lane = lax.broadcasted_iota(jnp.int32, x.shape, 1)
    d = 1
    while d < L:
        if reverse:
            shifted = jnp.where(lane < L - d, pltpu.roll(x, L - d, axis=1), ident)
        else:
            shifted = jnp.where(lane >= d, pltpu.roll(x, d, axis=1), ident)
        x = op(x, shifted)
        d *= 2
    return x


def _in_proj_kernel(x_ref, xp_ref, xn_ref, g_ref, sc_ref, sh_ref, w_ref, wg_ref,
                    bg_ref, cw_ref, cb_ref, z_ref, vt_ref, grow_ref, qk_s, hb_s):
    tm = x_ref.shape[1]
    L = MLSTM_CHUNK
    i = pl.program_id(1)

    gain = g_ref[...] * (1.0 + sc_ref[0])

    def norm_mod(x):
        ms = jnp.mean(x * x, axis=-1, keepdims=True)
        return (x * lax.rsqrt(ms + EPS) * gain + sh_ref[0]).astype(_BF)

    hb_s[HALO:HALO + tm, :] = norm_mod(x_ref[0])
    hb_s[:HALO, :] = jnp.where(i > 0, norm_mod(xp_ref[0]), jnp.zeros((HALO, D_MODEL), _BF))
    hb_s[HALO + tm:, :] = jnp.where(i < pl.num_programs(1) - 1, norm_mod(xn_ref[0]),
                                    jnp.zeros((HALO, D_MODEL), _BF))
    hb = hb_s.at[HALO:HALO + tm, :]

    def plain(c0):
        cols = slice(c0, c0 + MXU_COLS)
        wrows = slice(c0 + D_MODEL, c0 + D_MODEL + MXU_COLS)
        r = lax.dot_general(hb[...], w_ref[wrows, :], _NT, preferred_element_type=_F32)
        if 3 * D_MODEL <= c0 < 5 * D_MODEL:
            r = _gelu(r)
        else:
            r = 1.0 / (1.0 + jnp.exp2(r * -_LOG2E))
        z_ref[0, :, cols] = r.astype(_BF)

    def conv_matmul(c0):
        qk_s[:, c0:c0 + MXU_COLS] = lax.dot_general(
            hb_s[...], w_ref[c0:c0 + MXU_COLS, :], _NT, preferred_element_type=_F32)

    def conv_silu(c0):
        n = CONV_ROWS + 2 * F32_ROWS
        mid = slice(F32_ROWS, F32_ROWS + CONV_ROWS)
        for c1 in range(c0, c0 + MXU_COLS, CONV_COLS):
            cols = slice(c1, c1 + CONV_COLS)
            for r0 in range(0, tm, CONV_ROWS):
                zb = qk_s[pl.ds(HALO - F32_ROWS + r0, n), cols]
                y = cb_ref[:, cols] + pltpu.roll(zb, 1, axis=0)[mid] * cw_ref[0:1, cols]
                y = y + zb[mid] * cw_ref[1:2, cols]
                y = y + pltpu.roll(zb, n - 1, axis=0)[mid] * cw_ref[2:3, cols]
                y = y / (1.0 + jnp.exp2(y * -_LOG2E))
                if c1 < D_MODEL:
                    y = y * (HEAD_DIM ** -0.5)
                z_ref[0, pl.ds(r0, CONV_ROWS), cols] = y.astype(_BF)

    def gates_pre():
        gc = lax.dot_general(hb[...], wg_ref[...], _NT,
                             preferred_element_type=_F32) + bg_ref[...]
        return gc.T

    def gates_row(gt):
        nr = HEADS * GATE_SLOTS
        lst = _log_sigmoid(gt[:nr])
        git = gt[nr:2 * nr]
        slot = lax.broadcasted_iota(jnp.int32, (nr, L), 0) & (GATE_SLOTS - 1)
        fwd = slot < 3
        for c in range(tm // L):
            cols = slice(c * L, (c + 1) * L)
            lsc = lst[:, cols]
            cum = jnp.where(fwd, _lane_scan(lsc, jnp.add, 0.0, False),
                            _lane_scan(lsc, jnp.add, 0.0, True))
            base = jnp.where((slot == 0) | (slot == 3), cum, git[:, cols] - cum)
            cmax = jnp.where(fwd, _lane_scan(base, jnp.maximum, -jnp.inf, False),
                             _lane_scan(base, jnp.maximum, -jnp.inf, True))
            grow_ref[0, :, cols] = jnp.where((slot == 2) | (slot == 5), cmax, base)

    gt = gates_pre()
    conv_matmul(0)
    for c0 in range(MXU_COLS, 2 * D_MODEL, MXU_COLS):
        conv_matmul(c0)
        conv_silu(c0 - MXU_COLS)
    plain(2 * D_MODEL)
    conv_silu(2 * D_MODEL - MXU_COLS)
    gates_row(gt)
    for c0 in range(2 * D_MODEL + MXU_COLS, Z_COLS, MXU_COLS):
        plain(c0)
    vt_ref[0] = lax.dot_general(w_ref[2 * D_MODEL:3 * D_MODEL, :], hb[...], _NT,
                                preferred_element_type=_F32).astype(_BF)


def _in_proj_call(x, norm_g, sc, sh, w_bf, wg_bf, bias_g, conv_w, conv_b, tm):
    B, S, D = x.shape
    r = tm // HALO
    last = S // HALO - 1
    return pl.pallas_call(
        _in_proj_kernel,
        grid=(B, S // tm),
        in_specs=[pl.BlockSpec((1, tm, D), lambda b, i: (b, i, 0)),
                  pl.BlockSpec((1, HALO, D), lambda b, i: (b, jnp.maximum(i * r - 1, 0), 0)),
                  pl.BlockSpec((1, HALO, D),
                               lambda b, i: (b, jnp.minimum((i + 1) * r, last), 0)),
                  _const_spec((1, D)),
                  pl.BlockSpec((1, 1, D), lambda b, i: (b, 0, 0)),
                  pl.BlockSpec((1, 1, D), lambda b, i: (b, 0, 0)),
                  _const_spec(w_bf.shape),
                  _const_spec((LANES, D)),
                  _const_spec((1, LANES)),
                  _const_spec((3, 2 * D)),
                  _const_spec((1, 2 * D))],
        out_specs=[pl.BlockSpec((1, tm, Z_COLS), lambda b, i: (b, i, 0)),
                   pl.BlockSpec((1, D, tm), lambda b, i: (b, 0, i)),
                   pl.BlockSpec((1, HEADS * GATE_SLOTS, tm), lambda b, i: (b, 0, i))],
        out_shape=[jax.ShapeDtypeStruct((B, S, Z_COLS), _BF),
                   jax.ShapeDtypeStruct((B, D, S), _BF),
                   jax.ShapeDtypeStruct((B, HEADS * GATE_SLOTS, S), _F32)],
        scratch_shapes=[pltpu.VMEM((tm + 2 * HALO, 2 * D), _F32),
                        pltpu.VMEM((tm + 2 * HALO, D), _BF)],
        compiler_params=pltpu.CompilerParams(
            dimension_semantics=("arbitrary", "arbitrary"),
            vmem_limit_bytes=VMEM_LIMIT),
        name="in_proj",
    )(x, x, x, norm_g, sc, sh, w_bf, wg_bf, bias_g, conv_w, conv_b)


def _mlstm_kernel(q_ref, k_ref, vt_ref, zo_ref, gr_ref, mhg_ref,
                  y_ref,
                  pvf_s, pvb_s, u_s, snap_s, ct_s, mh_s, dec_s, ht_s):
    S = q_ref.shape[1]
    L = MLSTM_CHUNK
    NC = S // L
    dh = HEAD_DIM
    R = STATE_ROWS
    K0 = (0, 3)
    LAST = (L - 1, 0)
    NB = R - dh

    def n_rows(d):
        return slice(d * NB, (d + 1) * NB)

    def c_rows(d, e0):
        start = 2 * NB + 2 * e0 + d * STATE_BLOCK
        return slice(start, start + STATE_BLOCK)

    def state_blocks(d):
        return [n_rows(d)] + [c_rows(d, e0) for e0 in range(0, dh, STATE_BLOCK)]

    def gate_rows(gr, d):
        k0 = K0[d]
        return gr[k0:k0 + 1], gr[k0 + 1:k0 + 2], gr[k0 + 2:k0 + 3]

    m = [jnp.zeros((1, 1), _F32), jnp.zeros((1, 1), _F32)]
    for i in range(NC):
        for d, c in enumerate((i, NC - 1 - i)):
            b_row, _, cmax_row = gate_rows(gr_ref[0, 0, :, c * L:(c + 1) * L], d)
            a_last = jnp.maximum(m[d], cmax_row[:, LAST[d]:LAST[d] + 1])
            mh_s[d * NC + c:d * NC + c + 1, :] = jnp.broadcast_to(m[d], (1, LANES))
            dec_s[d * NC + c:d * NC + c + 1, :] = jnp.broadcast_to(jnp.exp(m[d] - a_last),
                                                                  (1, LANES))
            m[d] = b_row[:, LAST[d]:LAST[d] + 1] + a_last

    si = lax.broadcasted_iota(jnp.int32, (L, L), 0)
    ti = lax.broadcasted_iota(jnp.int32, (L, L), 1)
    masks = (si <= ti, si >= ti)
    ones_row = (lax.broadcasted_iota(jnp.int32, (R - dh, L), 0) == 0).astype(_BF)

    def scores(c):
        t0 = pl.multiple_of(c * L, L)
        return lax.dot_general(k_ref[0, pl.ds(t0, L), :], q_ref[0, pl.ds(t0, L), :], _NT,
                               preferred_element_type=_F32)

    def pass0(g, carry):
        s_next = scores(g * PASS0_GROUP)
        for k in range(PASS0_GROUP):
            c = g * PASS0_GROUP + k
            t0 = pl.multiple_of(c * L, L)
            s_t = s_next
            vta = jnp.concatenate([vt_ref[0, :, pl.ds(t0, L)], ones_row], axis=0)
            vta32 = vta.astype(_F32)
            gr = gr_ref[0, 0, :, pl.ds(t0, L)]
            a_rows, vtw = [], []
            for d in range(2):
                _, r_row, cmax_row = gate_rows(gr, d)
                a_rows.append(jnp.maximum(mh_s[pl.ds(d * NC + c, 1), 0:1], cmax_row))
                w_row = jnp.exp(r_row - a_rows[d][:, LAST[d]:LAST[d] + 1])
                vtw.append((vta32 * w_row).astype(_BF))
            stacked = [vtw[d][dh:] for d in range(2)]
            for e0 in range(0, dh, STATE_BLOCK):
                stacked += [vtw[d][e0:e0 + STATE_BLOCK] for d in range(2)]
            u_s[c] = jnp.dot(jnp.concatenate(stacked, axis=0), k_ref[0, pl.ds(t0, L), :],
                             preferred_element_type=_F32).astype(_BF)
            if k + 1 < PASS0_GROUP:
                s_next = scores(c + 1)
            p = []
            for d in range(2):
                r_row = gate_rows(gr, d)[1]
                r_rep = jnp.broadcast_to(r_row, (LANES, L)).T
                r_col = jnp.concatenate([r_rep] * (L // LANES), axis=1)
                p.append((jnp.exp(jnp.where(masks[d], r_col - a_rows[d], -jnp.inf))
                          * s_t).astype(_BF))
            pv = jnp.dot(vta, jnp.concatenate(p, axis=1), preferred_element_type=_F32)
            pvf_s[:, pl.ds(t0, L)] = pv[:, :L]
            pvb_s[:, pl.ds(t0, L)] = pv[:, L:]
        return carry

    lax.fori_loop(0, NC // PASS0_GROUP, pass0, 0)

    ct_s[...] = jnp.zeros_like(ct_s)

    def pass1(i, carry):
        for d, c in enumerate((i, NC - 1 - i)):
            decay = dec_s[pl.ds(d * NC + c, 1), 0:1]
            for rows in state_blocks(d):
                ct = ct_s[rows, :]
                snap_s[c, rows, :] = ct.astype(_BF)
                ct_s[rows, :] = decay * ct + u_s[c, rows, :].astype(_F32)
        return carry

    lax.fori_loop(0, NC, pass1, 0, unroll=2)

    def inter_term(c):
        t0 = pl.multiple_of(c * L, L)
        return lax.dot_general(snap_s[c], q_ref[0, pl.ds(t0, L), :], _NT,
                               preferred_element_type=_F32)

    def pass2(g, carry):
        inter_next = inter_term(g * PASS2_GROUP)
        for k in range(PASS2_GROUP):
            c = g * PASS2_GROUP + k
            t0 = pl.multiple_of(c * L, L)
            inter = inter_next
            if k + 1 < PASS2_GROUP:
                inter_next = inter_term(c + 1)
            gr = gr_ref[0, 0, :, pl.ds(t0, L)]
            w_inter, scale = [], []
            for d, pv_s in enumerate((pvf_s, pvb_s)):
                b_row, _, cmax_row = gate_rows(gr, d)
                m_c = mh_s[pl.ds(d * NC + c, 1), 0:1]
                a_row = jnp.maximum(m_c, cmax_row)
                w_inter.append(jnp.exp(m_c - a_row))
                den = (w_inter[d] * inter[d * NB:d * NB + 1]
                       + pv_s[dh:dh + 1, pl.ds(t0, L)])
                scale.append(1.0 / jnp.maximum(jnp.abs(den), jnp.exp(-(b_row + a_row))))
            ss = jnp.zeros((1, L), _F32)
            for e0 in range(0, dh, STATE_BLOCK):
                blk = None
                for d, pv_s in enumerate((pvf_s, pvb_s)):
                    num = (w_inter[d] * inter[c_rows(d, e0)]
                           + pv_s[e0:e0 + STATE_BLOCK, pl.ds(t0, L)])
                    blk = num * scale[d] if blk is None else blk + num * scale[d]
                ss = ss + jnp.sum(blk * blk, axis=0, keepdims=True)
                ht_s[e0:e0 + STATE_BLOCK, :] = blk
            rs = lax.rsqrt(ss * (1.0 / dh) + EPS)
            for t1 in range(0, L, LANES):
                h = (ht_s[:, t1:t1 + LANES] * rs[:, t1:t1 + LANES]).T
                tq = pl.multiple_of(t0 + t1, LANES)
                o = zo_ref[0, pl.ds(tq, LANES), :].astype(_F32)
                y_ref[0, pl.ds(tq, LANES), :] = (h * mhg_ref[...] * o).astype(_BF)
        return carry

    lax.fori_loop(0, NC // PASS2_GROUP, pass2, 0)


def _mlstm_call(z, vt, grow, mh_g):
    B, S, _ = z.shape
    dh = HEAD_DIM
    H = HEADS
    nc = S // MLSTM_CHUNK

    def zspec(off):
        return pl.BlockSpec((1, S, dh), lambda b, h, off=off: (b, 0, off + h))

    return pl.pallas_call(
        _mlstm_kernel,
        grid=(B, H),
        in_specs=[zspec(0), zspec(H),
                  pl.BlockSpec((1, dh, S), lambda b, h: (b, h, 0)),
                  zspec(2 * H),
                  pl.BlockSpec((1, 1, GATE_SLOTS, S), lambda b, h: (b, h, 0, 0)),
                  pl.BlockSpec((1, dh), lambda b, h: (0, h))],
        out_specs=pl.BlockSpec((1, S, dh), lambda b, h: (b, 0, h)),
        out_shape=jax.ShapeDtypeStruct((B, S, H * dh), _BF),
        scratch_shapes=[pltpu.VMEM((STATE_ROWS, S), _F32), pltpu.VMEM((STATE_ROWS, S), _F32),
                        pltpu.VMEM((nc, 2 * STATE_ROWS, dh), _BF),
                        pltpu.VMEM((nc, 2 * STATE_ROWS, dh), _BF),
                        pltpu.VMEM((2 * STATE_ROWS, dh), _F32),
                        pltpu.VMEM((2 * nc, LANES), _F32), pltpu.VMEM((2 * nc, LANES), _F32),
                        pltpu.VMEM((dh, MLSTM_CHUNK), _F32)],
        compiler_params=pltpu.CompilerParams(
            dimension_semantics=("arbitrary", "arbitrary"),
            vmem_limit_bytes=VMEM_LIMIT),
        name="mlstm",
    )(z, z, vt, z, grow, mh_g)


def _mix_kernel(x_ref, zu_ref, zv_ref, zga_ref, zgb_ref, ya_ref, g1_ref, lng_ref, lnb_ref,
                ws_ref, bs_ref, wo_ref, o_ref, mix_s):
    tm = x_ref.shape[1]
    P = SGU_CHUNK
    C = SGU_GROUP_DIM
    for j in range(tm // P):
        rows = slice(j * P, (j + 1) * P)
        gv = zv_ref[0, rows, :].astype(_F32)
        mu = jnp.mean(gv, axis=-1, keepdims=True)
        dv = gv - mu
        var = jnp.mean(dv * dv, axis=-1, keepdims=True)
        vn = (dv * lax.rsqrt(var + EPS) * lng_ref[...] + lnb_ref[...]).astype(_BF)
        s = jnp.concatenate(
            [jnp.dot(ws_ref[g], vn[:, g * C:(g + 1) * C], preferred_element_type=_F32)
             for g in range(SGU_GROUPS)], axis=1) + bs_ref[...]
        yb = zu_ref[0, rows, :].astype(_F32) * s
        ga = zga_ref[0, rows, :].astype(_F32)
        gb = zgb_ref[0, rows, :].astype(_F32)
        mix_s[rows, :] = (ga * ya_ref[0, rows, :].astype(_F32) + gb * yb).astype(_BF)
    upd = jnp.dot(mix_s[...], wo_ref[...], preferred_element_type=_F32)
    o_ref[0] = x_ref[0] + g1_ref[0] * upd


def _mix_call(x, z, ya, g1, ln_g, ln_b, ws_bf, bs_full, wo_bf, tm):
    B, S, D = x.shape

    def zspec(j):
        return pl.BlockSpec((1, tm, D), lambda b, i, j=j: (b, i, j))

    return pl.pallas_call(
        _mix_kernel,
        grid=(B, S // tm),
        in_specs=[zspec(0), zspec(3), zspec(4), zspec(5), zspec(6), zspec(0),
                  pl.BlockSpec((1, 1, D), lambda b, i: (b, 0, 0)),
                  _const_spec((1, D)), _const_spec((1, D)),
                  _const_spec((SGU_GROUPS, SGU_CHUNK, SGU_CHUNK)),
                  _const_spec((SGU_CHUNK, D)),
                  _const_spec((D, D))],
        out_specs=pl.BlockSpec((1, tm, D), lambda b, i: (b, i, 0)),
        out_shape=jax.ShapeDtypeStruct((B, S, D), _F32),
        scratch_shapes=[pltpu.VMEM((tm, D), _BF)],
        compiler_params=pltpu.CompilerParams(
            dimension_semantics=("arbitrary", "arbitrary"),
            vmem_limit_bytes=VMEM_LIMIT),
        name="mix",
    )(x, z, z, z, z, ya, g1, ln_g, ln_b, ws_bf, bs_full, wo_bf)


def _ffn_kernel(x_ref, g_ref, sc_ref, sh_ref, g2_ref, w1_ref, w2_ref, nf_ref, o_ref):
    x = x_ref[0]
    ms = jnp.mean(x * x, axis=-1, keepdims=True)
    gain = g_ref[...] * (1.0 + sc_ref[0])
    hb = (x * lax.rsqrt(ms + EPS) * gain + sh_ref[0]).astype(_BF)
    ff = jnp.zeros(x.shape, _F32)
    for j in range(FF_DIM // D_MODEL):
        cols = slice(j * D_MODEL, (j + 1) * D_MODEL)
        a = jnp.maximum(jnp.dot(hb, w1_ref[:, cols], preferred_element_type=_F32), 0.0)
        ff = ff + jnp.dot((a * a).astype(_BF), w2_ref[cols, :], preferred_element_type=_F32)
    x2 = x + g2_ref[0] * ff
    ms2 = jnp.mean(x2 * x2, axis=-1, keepdims=True)
    o_ref[0] = x2 * lax.rsqrt(ms2 + EPS) * nf_ref[...]


def _ffn_call(x1, norm_g, sc, sh, g2, w1_bf, w2_bf, normf_g, tm):
    B, S, D = x1.shape
    vec = pl.BlockSpec((1, 1, D), lambda b, i: (b, 0, 0))
    return pl.pallas_call(
        _ffn_kernel,
        grid=(B, S // tm),
        in_specs=[pl.BlockSpec((1, tm, D), lambda b, i: (b, i, 0)),
                  _const_spec((1, D)), vec, vec, vec,
                  _const_spec((D, FF_DIM)), _const_spec((FF_DIM, D)),
                  _const_spec((1, D))],
        out_specs=pl.BlockSpec((1, tm, D), lambda b, i: (b, i, 0)),
        out_shape=jax.ShapeDtypeStruct((B, S, D), _F32),
        compiler_params=pltpu.CompilerParams(
            dimension_semantics=("arbitrary", "arbitrary"),
            vmem_limit_bytes=VMEM_LIMIT),
        name="ffn",
    )(x1, norm_g, sc, sh, g2, w1_bf, w2_bf, normf_g)


def _gate_weights(w_if_t, b_if):
    H = HEADS
    idx_f, idx_i = [], []
    for h in range(H):
        idx_f += [H + h] * 3 + [3 * H + h] * 3 + [None] * 2
        idx_i += [None, h, h, None, 2 * H + h, 2 * H + h, None, None]
    sel = np.zeros((LANES, 4 * H), np.float32)
    for row, src in enumerate(idx_f + idx_i):
        if src is not None:
            sel[row, src] = 1.0
    w_g = (sel[:, :, None] * w_if_t[None]).sum(axis=1)
    b_g = (sel * b_if.reshape(1, -1)).sum(axis=1).reshape(1, LANES)
    return w_g, b_g


def kernel(x, c, w_ada, b_ada, norm1_g, norm2_g, w_in, b_if, conv_w, conv_b, mh_g,
           ln_v_g, ln_v_b, w_s, b_s, w_out, w1, w2, normf_g):
    B, S, D = x.shape
    H = HEADS
    assert w_ada.shape[0] == 1, "single layer"

    mod = _mod_call(c, w_ada[0], b_ada[0])
    sh1, sc1, g1, sh2, sc2, g2 = (mod[:, k * D:(k + 1) * D].reshape(B, 1, D) for k in range(6))

    w_in_t = w_in[0].T
    w_g, b_g = _gate_weights(w_in_t[8 * D:], b_if[0])
    z, vt, grow = _in_proj_call(x, norm1_g, sc1, sh1, w_in_t.astype(_BF), w_g.astype(_BF), b_g,
                                conv_w[0], conv_b, tm=IN_PROJ_ROWS)

    ya = _mlstm_call(z, vt, grow.reshape(B, H, GATE_SLOTS, S), mh_g)

    bs_full = jnp.repeat(b_s[0].T, SGU_GROUP_DIM, axis=1)
    x1 = _mix_call(x, z, ya, g1, ln_v_g, ln_v_b, w_s[0].astype(_BF), bs_full,
                   w_out[0].astype(_BF), tm=MIX_ROWS)

    return _ffn_call(x1, norm2_g, sc2, sh2, g2, w1[0].astype(_BF), w2[0].astype(_BF),
                     normf_g.reshape(1, D), tm=FFN_ROWS)
```

```python
import jax
import jax.numpy as jnp
import numpy as np
from jax import lax
from jax.experimental import pallas as pl
from jax.experimental.pallas import tpu as pltpu

D_MODEL = 1024
HEADS = 4
HEAD_DIM = 256
SGU_GROUPS = 8
SGU_GROUP_DIM = 128
SGU_CHUNK = 128
FF_DIM = 4096
EPS = 1e-6
Z_COLS = 7 * D_MODEL

MLSTM_CHUNK = 256
F32_ROWS = 8
BF16_ROWS = 16
STATE_ROWS = HEAD_DIM + BF16_ROWS
STATE_BLOCK = 64
PASS0_GROUP = 8
PASS2_GROUP = 16
HALO = BF16_ROWS
GATE_SLOTS = 8
CONV_ROWS = 64
CONV_COLS = 128
MXU_COLS = 256
LANES = 128

VMEM_LIMIT = 56 * 1024 * 1024
IN_PROJ_ROWS = 512
TAIL_ROWS = 512

_BF = jnp.bfloat16
_F32 = jnp.float32
_NT = (((1,), (1,)), ((), ()))
_LOG2E = 1.4426950408889634


def _const_spec(shape):
    nd = len(shape)
    return pl.BlockSpec(shape, lambda *_: (0,) * nd, pipeline_mode=pl.Buffered(1))


def _sigmoid(x):
    return 1.0 / (1.0 + jnp.exp(-x))


def _log_sigmoid(x):
    return jnp.minimum(x, 0.0) - jnp.log(1.0 + jnp.exp(-jnp.abs(x)))


def _gelu(x):
    return 0.5 * x * (1.0 + lax.erf(x * (2.0 ** -0.5)))


def _mod_kernel(c_ref, w_ref, b_ref, o_ref):
    c = c_ref[...]
    ca = c * _sigmoid(c)
    o_ref[...] = jnp.dot(ca, w_ref[...], precision=lax.Precision.HIGHEST,
                         preferred_element_type=_F32) + b_ref[...]


def _mod_call(c, w_ada, b_ada):
    B, D = c.shape
    N = w_ada.shape[1]
    tn = N // 4
    return pl.pallas_call(
        _mod_kernel,
        grid=(N // tn,),
        in_specs=[pl.BlockSpec((B, D), lambda j: (0, 0)),
                  pl.BlockSpec((D, tn), lambda j: (0, j)),
                  pl.BlockSpec((1, tn), lambda j: (0, j))],
        out_specs=pl.BlockSpec((B, tn), lambda j: (0, j)),
        out_shape=jax.ShapeDtypeStruct((B, N), _F32),
        name="mod",
    )(c, w_ada, b_ada.reshape(1, N))


def _lane_scan(x, op, ident, reverse):
    L = x.shape[1]
    lane = lax.broadcasted_iota(jnp.int32, x.shape, 1)
    d = 1
    while d < L:
        if reverse:
            shifted = jnp.where(lane < L - d, pltpu.roll(x, L - d, axis=1), ident)
        else:
            shifted = jnp.where(lane >= d, pltpu.roll(x, d, axis=1), ident)
        x = op(x, shifted)
        d *= 2
    return x


def _in_proj_kernel(x_ref, xp_ref, xn_ref, g_ref, sc_ref, sh_ref, w_ref, wg_ref,
                    bg_ref, cw_ref, cb_ref, z_ref, vt_ref, grow_ref, qk_s, hb_s):
    tm = x_ref.shape[1]
    L = MLSTM_CHUNK
    i = pl.program_id(1)

    gain = g_ref[...] * (1.0 + sc_ref[0])

    def norm_mod(x):
        ms = jnp.mean(x * x, axis=-1, keepdims=True)
        return (x * lax.rsqrt(ms + EPS) * gain + sh_ref[0]).astype(_BF)

    hb_s[HALO:HALO + tm, :] = norm_mod(x_ref[0])
    hb_s[:HALO, :] = jnp.where(i > 0, norm_mod(xp_ref[0]), jnp.zeros((HALO, D_MODEL), _BF))
    hb_s[HALO + tm:, :] = jnp.where(i < pl.num_programs(1) - 1, norm_mod(xn_ref[0]),
                                    jnp.zeros((HALO, D_MODEL), _BF))
    hb = hb_s.at[HALO:HALO + tm, :]

    def plain(c0):
        cols = slice(c0, c0 + MXU_COLS)
        wrows = slice(c0 + D_MODEL, c0 + D_MODEL + MXU_COLS)
        r = lax.dot_general(hb[...], w_ref[wrows, :], _NT, preferred_element_type=_F32)
        if 3 * D_MODEL <= c0 < 5 * D_MODEL:
            r = _gelu(r)
        else:
            r = 1.0 / (1.0 + jnp.exp2(r * -_LOG2E))
        z_ref[0, :, cols] = r.astype(_BF)

    def conv_matmul(c0):
        qk_s[:, c0:c0 + MXU_COLS] = lax.dot_general(
            hb_s[...], w_ref[c0:c0 + MXU_COLS, :], _NT, preferred_element_type=_F32)

    def conv_silu(c0):
        n = CONV_ROWS + 2 * F32_ROWS
        mid = slice(F32_ROWS, F32_ROWS + CONV_ROWS)
        for c1 in range(c0, c0 + MXU_COLS, CONV_COLS):
            cols = slice(c1, c1 + CONV_COLS)
            for r0 in range(0, tm, CONV_ROWS):
                zb = qk_s[pl.ds(HALO - F32_ROWS + r0, n), cols]
                y = cb_ref[:, cols] + pltpu.roll(zb, 1, axis=0)[mid] * cw_ref[0:1, cols]
                y = y + zb[mid] * cw_ref[1:2, cols]
                y = y + pltpu.roll(zb, n - 1, axis=0)[mid] * cw_ref[2:3, cols]
                y = y / (1.0 + jnp.exp2(y * -_LOG2E))
                if c1 < D_MODEL:
                    y = y * (HEAD_DIM ** -0.5)
                z_ref[0, pl.ds(r0, CONV_ROWS), cols] = y.astype(_BF)

    def gates_pre():
        gc = lax.dot_general(hb[...], wg_ref[...], _NT,
                             preferred_element_type=_F32) + bg_ref[...]
        return gc.T

    def gates_row(gt):
        nr = HEADS * GATE_SLOTS
        lst = _log_sigmoid(gt[:nr])
        git = gt[nr:2 * nr]
        slot = lax.broadcasted_iota(jnp.int32, (nr, L), 0) & (GATE_SLOTS - 1)
        fwd = slot < 3
        for c in range(tm // L):
            cols = slice(c * L, (c + 1) * L)
            lsc = lst[:, cols]
            cum = jnp.where(fwd, _lane_scan(lsc, jnp.add, 0.0, False),
                            _lane_scan(lsc, jnp.add, 0.0, True))
            base = jnp.where((slot == 0) | (slot == 3), cum, git[:, cols] - cum)
            cmax = jnp.where(fwd, _lane_scan(base, jnp.maximum, -jnp.inf, False),
                             _lane_scan(base, jnp.maximum, -jnp.inf, True))
            grow_ref[0, :, cols] = jnp.where((slot == 2) | (slot == 5), cmax, base)

    gt = gates_pre()
    conv_matmul(0)
    for c0 in range(MXU_COLS, 2 * D_MODEL, MXU_COLS):
        conv_matmul(c0)
        conv_silu(c0 - MXU_COLS)
    plain(2 * D_MODEL)
    conv_silu(2 * D_MODEL - MXU_COLS)
    gates_row(gt)
    for c0 in range(2 * D_MODEL + MXU_COLS, Z_COLS, MXU_COLS):
        plain(c0)
    vt_ref[0] = lax.dot_general(w_ref[2 * D_MODEL:3 * D_MODEL, :], hb[...], _NT,
                                preferred_element_type=_F32).astype(_BF)


def _in_proj_call(x, norm_g, sc, sh, w_bf, wg_bf, bias_g, conv_w, conv_b, tm):
    B, S, D = x.shape
    r = tm // HALO
    last = S // HALO - 1
    return pl.pallas_call(
        _in_proj_kernel,
        grid=(B, S // tm),
        in_specs=[pl.BlockSpec((1, tm, D), lambda b, i: (b, i, 0)),
                  pl.BlockSpec((1, HALO, D), lambda b, i: (b, jnp.maximum(i * r - 1, 0), 0)),
                  pl.BlockSpec((1, HALO, D),
                               lambda b, i: (b, jnp.minimum((i + 1) * r, last), 0)),
                  _const_spec((1, D)),
                  pl.BlockSpec((1, 1, D), lambda b, i: (b, 0, 0)),
                  pl.BlockSpec((1, 1, D), lambda b, i: (b, 0, 0)),
                  _const_spec(w_bf.shape),
                  _const_spec((LANES, D)),
                  _const_spec((1, LANES)),
                  _const_spec((3, 2 * D)),
                  _const_spec((1, 2 * D))],
        out_specs=[pl.BlockSpec((1, tm, Z_COLS), lambda b, i: (b, i, 0)),
                   pl.BlockSpec((1, D, tm), lambda b, i: (b, 0, i)),
                   pl.BlockSpec((1, HEADS * GATE_SLOTS, tm), lambda b, i: (b, 0, i))],
        out_shape=[jax.ShapeDtypeStruct((B, S, Z_COLS), _BF),
                   jax.ShapeDtypeStruct((B, D, S), _BF),
                   jax.ShapeDtypeStruct((B, HEADS * GATE_SLOTS, S), _F32)],
        scratch_shapes=[pltpu.VMEM((tm + 2 * HALO, 2 * D), _F32),
                        pltpu.VMEM((tm + 2 * HALO, D), _BF)],
        compiler_params=pltpu.CompilerParams(
            dimension_semantics=("arbitrary", "arbitrary"),
            vmem_limit_bytes=VMEM_LIMIT),
        name="in_proj",
    )(x, x, x, norm_g, sc, sh, w_bf, wg_bf, bias_g, conv_w, conv_b)


def _mlstm_kernel(q_ref, k_ref, vt_ref, zo_ref, gr_ref, mhg_ref,
                  y_ref,
                  pvf_s, pvb_s, u_s, snap_s, ct_s, mh_s, dec_s, ht_s):
    S = q_ref.shape[1]
    L = MLSTM_CHUNK
    NC = S // L
    dh = HEAD_DIM
    R = STATE_ROWS
    K0 = (0, 3)
    LAST = (L - 1, 0)
    NB = R - dh

    def n_rows(d):
        return slice(d * NB, (d + 1) * NB)

    def c_rows(d, e0):
        start = 2 * NB + 2 * e0 + d * STATE_BLOCK
        return slice(start, start + STATE_BLOCK)

    def state_blocks(d):
        return [n_rows(d)] + [c_rows(d, e0) for e0 in range(0, dh, STATE_BLOCK)]

    def gate_rows(gr, d):
        k0 = K0[d]
        return gr[k0:k0 + 1], gr[k0 + 1:k0 + 2], gr[k0 + 2:k0 + 3]

    m = [jnp.zeros((1, 1), _F32), jnp.zeros((1, 1), _F32)]
    for i in range(NC):
        for d, c in enumerate((i, NC - 1 - i)):
            b_row, _, cmax_row = gate_rows(gr_ref[0, 0, :, c * L:(c + 1) * L], d)
            a_last = jnp.maximum(m[d], cmax_row[:, LAST[d]:LAST[d] + 1])
            mh_s[d * NC + c:d * NC + c + 1, :] = jnp.broadcast_to(m[d], (1, LANES))
            dec_s[d * NC + c:d * NC + c + 1, :] = jnp.broadcast_to(jnp.exp(m[d] - a_last),
                                                                  (1, LANES))
            m[d] = b_row[:, LAST[d]:LAST[d] + 1] + a_last

    si = lax.broadcasted_iota(jnp.int32, (L, L), 0)
    ti = lax.broadcasted_iota(jnp.int32, (L, L), 1)
    masks = (si <= ti, si >= ti)
    ones_row = (lax.broadcasted_iota(jnp.int32, (R - dh, L), 0) == 0).astype(_BF)

    def scores(c):
        t0 = pl.multiple_of(c * L, L)
        return lax.dot_general(k_ref[0, pl.ds(t0, L), :], q_ref[0, pl.ds(t0, L), :], _NT,
                               preferred_element_type=_F32)

    def pass0(g, carry):
        s_next = scores(g * PASS0_GROUP)
        for k in range(PASS0_GROUP):
            c = g * PASS0_GROUP + k
            t0 = pl.multiple_of(c * L, L)
            s_t = s_next
            vta = jnp.concatenate([vt_ref[0, :, pl.ds(t0, L)], ones_row], axis=0)
            vta32 = vta.astype(_F32)
            gr = gr_ref[0, 0, :, pl.ds(t0, L)]
            a_rows, vtw = [], []
            for d in range(2):
                _, r_row, cmax_row = gate_rows(gr, d)
                a_rows.append(jnp.maximum(mh_s[pl.ds(d * NC + c, 1), 0:1], cmax_row))
                w_row = jnp.exp(r_row - a_rows[d][:, LAST[d]:LAST[d] + 1])
                vtw.append((vta32 * w_row).astype(_BF))
            stacked = [vtw[d][dh:] for d in range(2)]
            for e0 in range(0, dh, STATE_BLOCK):
                stacked += [vtw[d][e0:e0 + STATE_BLOCK] for d in range(2)]
            u_s[c] = jnp.dot(jnp.concatenate(stacked, axis=0), k_ref[0, pl.ds(t0, L), :],
                             preferred_element_type=_F32).astype(_BF)
            if k + 1 < PASS0_GROUP:
                s_next = scores(c + 1)
            p = []
            for d in range(2):
                r_row = gate_rows(gr, d)[1]
                r_rep = jnp.broadcast_to(r_row, (LANES, L)).T
                r_col = jnp.concatenate([r_rep] * (L // LANES), axis=1)
                p.append((jnp.exp(jnp.where(masks[d], r_col - a_rows[d], -jnp.inf))
                          * s_t).astype(_BF))
            pv = jnp.dot(vta, jnp.concatenate(p, axis=1), preferred_element_type=_F32)
            pvf_s[:, pl.ds(t0, L)] = pv[:, :L]
            pvb_s[:, pl.ds(t0, L)] = pv[:, L:]
        return carry

    lax.fori_loop(0, NC // PASS0_GROUP, pass0, 0)

    ct_s[...] = jnp.zeros_like(ct_s)

    def pass1(i, carry):
        for d, c in enumerate((i, NC - 1 - i)):
            decay = dec_s[pl.ds(d * NC + c, 1), 0:1]
            for rows in state_blocks(d):
                ct = ct_s[rows, :]
                snap_s[c, rows, :] = ct.astype(_BF)
                ct_s[rows, :] = decay * ct + u_s[c, rows, :].astype(_F32)
        return carry

    lax.fori_loop(0, NC, pass1, 0, unroll=2)

    def inter_term(c):
        t0 = pl.multiple_of(c * L, L)
        return lax.dot_general(snap_s[c], q_ref[0, pl.ds(t0, L), :], _NT,
                               preferred_element_type=_F32)

    def pass2(g, carry):
        inter_next = inter_term(g * PASS2_GROUP)
        for k in range(PASS2_GROUP):
            c = g * PASS2_GROUP + k
            t0 = pl.multiple_of(c * L, L)
            inter = inter_next
            if k + 1 < PASS2_GROUP:
                inter_next = inter_term(c + 1)
            gr = gr_ref[0, 0, :, pl.ds(t0, L)]
            w_inter, scale = [], []
            for d, pv_s in enumerate((pvf_s, pvb_s)):
                b_row, _, cmax_row = gate_rows(gr, d)
                m_c = mh_s[pl.ds(d * NC + c, 1), 0:1]
                a_row = jnp.maximum(m_c, cmax_row)
                w_inter.append(jnp.exp(m_c - a_row))
                den = (w_inter[d] * inter[d * NB:d * NB + 1]
                       + pv_s[dh:dh + 1, pl.ds(t0, L)])
                scale.append(1.0 / jnp.maximum(jnp.abs(den), jnp.exp(-(b_row + a_row))))
            ss = jnp.zeros((1, L), _F32)
            for e0 in range(0, dh, STATE_BLOCK):
                blk = None
                for d, pv_s in enumerate((pvf_s, pvb_s)):
                    num = (w_inter[d] * inter[c_rows(d, e0)]
                           + pv_s[e0:e0 + STATE_BLOCK, pl.ds(t0, L)])
                    blk = num * scale[d] if blk is None else blk + num * scale[d]
                ss = ss + jnp.sum(blk * blk, axis=0, keepdims=True)
                ht_s[e0:e0 + STATE_BLOCK, :] = blk
            rs = lax.rsqrt(ss * (1.0 / dh) + EPS)
            for t1 in range(0, L, LANES):
                h = (ht_s[:, t1:t1 + LANES] * rs[:, t1:t1 + LANES]).T
                tq = pl.multiple_of(t0 + t1, LANES)
                o = zo_ref[0, pl.ds(tq, LANES), :].astype(_F32)
                y_ref[0, pl.ds(tq, LANES), :] = (h * mhg_ref[...] * o).astype(_BF)
        return carry

    lax.fori_loop(0, NC // PASS2_GROUP, pass2, 0)


def _mlstm_call(z, vt, grow, mh_g):
    B, S, _ = z.shape
    dh = HEAD_DIM
    H = HEADS
    nc = S // MLSTM_CHUNK

    def zspec(off):
        return pl.BlockSpec((1, S, dh), lambda b, h, off=off: (b, 0, off + h))

    return pl.pallas_call(
        _mlstm_kernel,
        grid=(B, H),
        in_specs=[zspec(0), zspec(H),
                  pl.BlockSpec((1, dh, S), lambda b, h: (b, h, 0)),
                  zspec(2 * H),
                  pl.BlockSpec((1, 1, GATE_SLOTS, S), lambda b, h: (b, h, 0, 0)),
                  pl.BlockSpec((1, dh), lambda b, h: (0, h))],
        out_specs=pl.BlockSpec((1, S, dh), lambda b, h: (b, 0, h)),
        out_shape=jax.ShapeDtypeStruct((B, S, H * dh), _BF),
        scratch_shapes=[pltpu.VMEM((STATE_ROWS, S), _F32), pltpu.VMEM((STATE_ROWS, S), _F32),
                        pltpu.VMEM((nc, 2 * STATE_ROWS, dh), _BF),
                        pltpu.VMEM((nc, 2 * STATE_ROWS, dh), _BF),
                        pltpu.VMEM((2 * STATE_ROWS, dh), _F32),
                        pltpu.VMEM((2 * nc, LANES), _F32), pltpu.VMEM((2 * nc, LANES), _F32),
                        pltpu.VMEM((dh, MLSTM_CHUNK), _F32)],
        compiler_params=pltpu.CompilerParams(
            dimension_semantics=("arbitrary", "arbitrary"),
            vmem_limit_bytes=VMEM_LIMIT),
        name="mlstm",
    )(z, z, vt, z, grow, mh_g)


def _mix_kernel(x_ref, zu_ref, zv_ref, zga_ref, zgb_ref, ya_ref, g1_ref, lng_ref, lnb_ref,
                ws_ref, bs_ref, wo_ref, o_ref, mix_s):
    tm = x_ref.shape[1]
    P = SGU_CHUNK
    C = SGU_GROUP_DIM
    for j in range(tm // P):
        rows = slice(j * P, (j + 1) * P)
        gv = zv_ref[0, rows, :].astype(_F32)
        mu = jnp.mean(gv, axis=-1, keepdims=True)
        dv = gv - mu
        var = jnp.mean(dv * dv, axis=-1, keepdims=True)
        vn = (dv * lax.rsqrt(var + EPS) * lng_ref[...] + lnb_ref[...]).astype(_BF)
        s = jnp.concatenate(
            [jnp.dot(ws_ref[g], vn[:, g * C:(g + 1) * C], preferred_element_type=_F32)
             for g in range(SGU_GROUPS)], axis=1) + bs_ref[...]
        yb = zu_ref[0, rows, :].astype(_F32) * s
        ga = zga_ref[0, rows, :].astype(_F32)
        gb = zgb_ref[0, rows, :].astype(_F32)
        mix_s[rows, :] = (ga * ya_ref[0, rows, :].astype(_F32) + gb * yb).astype(_BF)
    upd = jnp.dot(mix_s[...], wo_ref[...], preferred_element_type=_F32)
    o_ref[0] = x_ref[0] + g1_ref[0] * upd


def _mix_call(x, z, ya, g1, ln_g, ln_b, ws_bf, bs_full, wo_bf, tm):
    B, S, D = x.shape

    def zspec(j):
        return pl.BlockSpec((1, tm, D), lambda b, i, j=j: (b, i, j))

    return pl.pallas_call(
        _mix_kernel,
        grid=(B, S // tm),
        in_specs=[zspec(0), zspec(3), zspec(4), zspec(5), zspec(6), zspec(0),
                  pl.BlockSpec((1, 1, D), lambda b, i: (b, 0, 0)),
                  _const_spec((1, D)), _const_spec((1, D)),
                  _const_spec((SGU_GROUPS, SGU_CHUNK, SGU_CHUNK)),
                  _const_spec((SGU_CHUNK, D)),
                  _const_spec((D, D))],
        out_specs=pl.BlockSpec((1, tm, D), lambda b, i: (b, i, 0)),
        out_shape=jax.ShapeDtypeStruct((B, S, D), _F32),
        scratch_shapes=[pltpu.VMEM((tm, D), _BF)],
        compiler_params=pltpu.CompilerParams(
            dimension_semantics=("arbitrary", "arbitrary"),
            vmem_limit_bytes=VMEM_LIMIT),
        name="mix",
    )(x, z, z, z, z, ya, g1, ln_g, ln_b, ws_bf, bs_full, wo_bf)


def _ffn_kernel(x_ref, g_ref, sc_ref, sh_ref, g2_ref, w1_ref, w2_ref, nf_ref, o_ref):
    x = x_ref[0]
    ms = jnp.mean(x * x, axis=-1, keepdims=True)
    gain = g_ref[...] * (1.0 + sc_ref[0])
    hb = (x * lax.rsqrt(ms + EPS) * gain + sh_ref[0]).astype(_BF)
    ff = jnp.zeros(x.shape, _F32)
    for j in range(FF_DIM // D_MODEL):
        cols = slice(j * D_MODEL, (j + 1) * D_MODEL)
        a = jnp.maximum(jnp.dot(hb, w1_ref[:, cols], preferred_element_type=_F32), 0.0)
        ff = ff + jnp.dot((a * a).astype(_BF), w2_ref[cols, :], preferred_element_type=_F32)
    x2 = x + g2_ref[0] * ff
    ms2 = jnp.mean(x2 * x2, axis=-1, keepdims=True)
    o_ref[0] = x2 * lax.rsqrt(ms2 + EPS) * nf_ref[...]


def _ffn_call(x1, norm_g, sc, sh, g2, w1_bf, w2_bf, normf_g, tm):
    B, S, D = x1.shape
    vec = pl.BlockSpec((1, 1, D), lambda b, i: (b, 0, 0))
    return pl.pallas_call(
        _ffn_kernel,
        grid=(B, S // tm),
        in_specs=[pl.BlockSpec((1, tm, D), lambda b, i: (b, i, 0)),
                  _const_spec((1, D)), vec, vec, vec,
                  _const_spec((D, FF_DIM)), _const_spec((FF_DIM, D)),
                  _const_spec((1, D))],
        out_specs=pl.BlockSpec((1, tm, D), lambda b, i: (b, i, 0)),
        out_shape=jax.ShapeDtypeStruct((B, S, D), _F32),
        compiler_params=pltpu.CompilerParams(
            dimension_semantics=("arbitrary", "arbitrary"),
            vmem_limit_bytes=VMEM_LIMIT),
        name="ffn",
    )(x1, norm_g, sc, sh, g2, w1_bf, w2_bf, normf_g)


def _tail_kernel(x_ref, zu_ref, zv_ref, zga_ref, zgb_ref, ya_ref, g1_ref, lng_ref, lnb_ref,
                 ws_ref, bs_ref, wo_ref, g_ref, sc_ref, sh_ref, g2_ref, w1_ref, w2_ref, nf_ref,
                 o_ref, mix_s, x1_s):
    _mix_kernel(x_ref, zu_ref, zv_ref, zga_ref, zgb_ref, ya_ref, g1_ref, lng_ref, lnb_ref,
                ws_ref, bs_ref, wo_ref, x1_s, mix_s)
    _ffn_kernel(x1_s, g_ref, sc_ref, sh_ref, g2_ref, w1_ref, w2_ref, nf_ref, o_ref)


def _tail_call(x, z, ya, g1, ln_g, ln_b, ws_bf, bs_full, wo_bf,
               norm_g, sc, sh, g2, w1_bf, w2_bf, normf_g, tm):
    B, S, D = x.shape

    def zspec(j):
        return pl.BlockSpec((1, tm, D), lambda b, i, j=j: (b, i, j))

    vec = pl.BlockSpec((1, 1, D), lambda b, i: (b, 0, 0))
    return pl.pallas_call(
        _tail_kernel,
        grid=(B, S // tm),
        in_specs=[zspec(0), zspec(3), zspec(4), zspec(5), zspec(6), zspec(0), vec,
                  _const_spec((1, D)), _const_spec((1, D)),
                  _const_spec((SGU_GROUPS, SGU_CHUNK, SGU_CHUNK)),
                  _const_spec((SGU_CHUNK, D)),
                  _const_spec((D, D)),
                  _const_spec((1, D)), vec, vec, vec,
                  _const_spec((D, FF_DIM)), _const_spec((FF_DIM, D)),
                  _const_spec((1, D))],
        out_specs=pl.BlockSpec((1, tm, D), lambda b, i: (b, i, 0)),
        out_shape=jax.ShapeDtypeStruct((B, S, D), _F32),
        scratch_shapes=[pltpu.VMEM((tm, D), _BF), pltpu.VMEM((1, tm, D), _F32)],
        compiler_params=pltpu.CompilerParams(
            dimension_semantics=("arbitrary", "arbitrary"),
            vmem_limit_bytes=VMEM_LIMIT),
        name="tail",
    )(x, z, z, z, z, ya, g1, ln_g, ln_b, ws_bf, bs_full, wo_bf,
      norm_g, sc, sh, g2, w1_bf, w2_bf, normf_g)


def _gate_weights(w_if_t, b_if):
    H = HEADS
    idx_f, idx_i = [], []
    for h in range(H):
        idx_f += [H + h] * 3 + [3 * H + h] * 3 + [None] * 2
        idx_i += [None, h, h, None, 2 * H + h, 2 * H + h, None, None]
    sel = np.zeros((LANES, 4 * H), np.float32)
    for row, src in enumerate(idx_f + idx_i):
        if src is not None:
            sel[row, src] = 1.0
    w_g = (sel[:, :, None] * w_if_t[None]).sum(axis=1)
    b_g = (sel * b_if.reshape(1, -1)).sum(axis=1).reshape(1, LANES)
    return w_g, b_g


def kernel(x, c, w_ada, b_ada, norm1_g, norm2_g, w_in, b_if, conv_w, conv_b, mh_g,
           ln_v_g, ln_v_b, w_s, b_s, w_out, w1, w2, normf_g):
    B, S, D = x.shape
    H = HEADS
    assert w_ada.shape[0] == 1, "single layer"

    mod = _mod_call(c, w_ada[0], b_ada[0])
    sh1, sc1, g1, sh2, sc2, g2 = (mod[:, k * D:(k + 1) * D].reshape(B, 1, D) for k in range(6))

    w_in_t = w_in[0].T
    w_g, b_g = _gate_weights(w_in_t[8 * D:], b_if[0])
    z, vt, grow = _in_proj_call(x, norm1_g, sc1, sh1, w_in_t.astype(_BF), w_g.astype(_BF), b_g,
                                conv_w[0], conv_b, tm=IN_PROJ_ROWS)

    ya = _mlstm_call(z, vt, grow.reshape(B, H, GATE_SLOTS, S), mh_g)

    bs_full = jnp.repeat(b_s[0].T, SGU_GROUP_DIM, axis=1)
    return _tail_call(x, z, ya, g1, ln_v_g, ln_v_b, w_s[0].astype(_BF), bs_full,
                      w_out[0].astype(_BF), norm2_g, sc2, sh2, g2, w1[0].astype(_BF),
                      w2[0].astype(_BF), normf_g.reshape(1, D), tm=TAIL_ROWS)
```

```python
import jax
import jax.numpy as jnp
import numpy as np
from jax import lax
from jax.experimental import pallas as pl
from jax.experimental.pallas import tpu as pltpu

D_MODEL = 1024
HEADS = 4
HEAD_DIM = 256
SGU_GROUPS = 8
SGU_GROUP_DIM = 128
SGU_CHUNK = 128
FF_DIM = 4096
EPS = 1e-6
Z_COLS = 7 * D_MODEL

MLSTM_CHUNK = 256
F32_ROWS = 8
BF16_ROWS = 16
STATE_ROWS = HEAD_DIM + BF16_ROWS
STATE_BLOCK = 64
PASS0_GROUP = 8
PASS2_GROUP = 16
HALO = BF16_ROWS
GATE_SLOTS = 8
CONV_ROWS = 64
CONV_COLS = 128
MXU_COLS = 256
LANES = 128

VMEM_LIMIT = 56 * 1024 * 1024
IN_PROJ_ROWS = 512
TAIL_ROWS = 512

_BF = jnp.bfloat16
_F32 = jnp.float32
_NT = (((1,), (1,)), ((), ()))
_LOG2E = 1.4426950408889634


def _const_spec(shape):
    nd = len(shape)
    return pl.BlockSpec(shape, lambda *_: (0,) * nd, pipeline_mode=pl.Buffered(1))


def _sigmoid(x):
    return 1.0 / (1.0 + jnp.exp(-x))


def _log_sigmoid(x):
    return jnp.minimum(x, 0.0) - jnp.log(1.0 + jnp.exp(-jnp.abs(x)))


def _gelu(x):
    return 0.5 * x * (1.0 + lax.erf(x * (2.0 ** -0.5)))


def _mod_kernel(c_ref, w_ref, b_ref, o_ref):
    c = c_ref[...]
    ca = c * _sigmoid(c)
    o_ref[...] = jnp.dot(ca, w_ref[...], precision=lax.Precision.HIGHEST,
                         preferred_element_type=_F32) + b_ref[...]


def _mod_call(c, w_ada, b_ada):
    B, D = c.shape
    N = w_ada.shape[1]
    tn = N // 4
    return pl.pallas_call(
        _mod_kernel,
        grid=(N // tn,),
        in_specs=[pl.BlockSpec((B, D), lambda j: (0, 0)),
                  pl.BlockSpec((D, tn), lambda j: (0, j)),
                  pl.BlockSpec((1, tn), lambda j: (0, j))],
        out_specs=pl.BlockSpec((B, tn), lambda j: (0, j)),
        out_shape=jax.ShapeDtypeStruct((B, N), _F32),
        name="mod",
    )(c, w_ada, b_ada.reshape(1, N))


def _lane_scan(x, op, ident, reverse):
    L = x.shape[1]
    lane = lax.broadcasted_iota(jnp.int32, x.shape, 1)
    d = 1
    while d < L:
        if reverse:
            shifted = jnp.where(lane < L - d, pltpu.roll(x, L - d, axis=1), ident)
        else:
            shifted = jnp.where(lane >= d, pltpu.roll(x, d, axis=1), ident)
        x = op(x, shifted)
        d *= 2
    return x


def _in_proj_kernel(x_ref, xp_ref, xn_ref, g_ref, sc_ref, sh_ref, w_ref, wg_ref,
                    bg_ref, cw_ref, cb_ref, z_ref, vt_ref, grow_ref, qk_s, hb_s):
    tm = x_ref.shape[1]
    L = MLSTM_CHUNK
    i = pl.program_id(1)

    gain = g_ref[...] * (1.0 + sc_ref[0])

    def norm_mod(x):
        ms = jnp.mean(x * x, axis=-1, keepdims=True)
        return (x * lax.rsqrt(ms + EPS) * gain + sh_ref[0]).astype(_BF)

    hb_s[HALO:HALO + tm, :] = norm_mod(x_ref[0])
    hb_s[:HALO, :] = jnp.where(i > 0, norm_mod(xp_ref[0]), jnp.zeros((HALO, D_MODEL), _BF))
    hb_s[HALO + tm:, :] = jnp.where(i < pl.num_programs(1) - 1, norm_mod(xn_ref[0]),
                                    jnp.zeros((HALO, D_MODEL), _BF))
    hb = hb_s.at[HALO:HALO + tm, :]

    def plain(c0):
        cols = slice(c0, c0 + MXU_COLS)
        wrows = slice(c0 + D_MODEL, c0 + D_MODEL + MXU_COLS)
        r = lax.dot_general(hb[...], w_ref[wrows, :], _NT, preferred_element_type=_F32)
        if 3 * D_MODEL <= c0 < 5 * D_MODEL:
            r = _gelu(r)
        else:
            r = 1.0 / (1.0 + jnp.exp2(r * -_LOG2E))
        z_ref[0, :, cols] = r.astype(_BF)

    def conv_matmul(c0):
        qk_s[:, c0:c0 + MXU_COLS] = lax.dot_general(
            hb_s[...], w_ref[c0:c0 + MXU_COLS, :], _NT, preferred_element_type=_F32)

    def conv_silu(c0):
        n = CONV_ROWS + 2 * F32_ROWS
        mid = slice(F32_ROWS, F32_ROWS + CONV_ROWS)
        for c1 in range(c0, c0 + MXU_COLS, CONV_COLS):
            cols = slice(c1, c1 + CONV_COLS)
            for r0 in range(0, tm, CONV_ROWS):
                zb = qk_s[pl.ds(HALO - F32_ROWS + r0, n), cols]
                y = cb_ref[:, cols] + pltpu.roll(zb, 1, axis=0)[mid] * cw_ref[0:1, cols]
                y = y + zb[mid] * cw_ref[1:2, cols]
                y = y + pltpu.roll(zb, n - 1, axis=0)[mid] * cw_ref[2:3, cols]
                y = y / (1.0 + jnp.exp2(y * -_LOG2E))
                if c1 < D_MODEL:
                    y = y * (HEAD_DIM ** -0.5)
                z_ref[0, pl.ds(r0, CONV_ROWS), cols] = y.astype(_BF)

    def gates_pre():
        gc = lax.dot_general(hb[...], wg_ref[...], _NT,
                             preferred_element_type=_F32) + bg_ref[...]
        return gc.T

    def gates_row(gt):
        nr = HEADS * GATE_SLOTS
        lst = _log_sigmoid(gt[:nr])
        git = gt[nr:2 * nr]
        slot = lax.broadcasted_iota(jnp.int32, (nr, L), 0) & (GATE_SLOTS - 1)
        fwd = slot < 3
        for c in range(tm // L):
            cols = slice(c * L, (c + 1) * L)
            lsc = lst[:, cols]
            cum = jnp.where(fwd, _lane_scan(lsc, jnp.add, 0.0, False),
                            _lane_scan(lsc, jnp.add, 0.0, True))
            base = jnp.where((slot == 0) | (slot == 3), cum, git[:, cols] - cum)
            cmax = jnp.where(fwd, _lane_scan(base, jnp.maximum, -jnp.inf, False),
                             _lane_scan(base, jnp.maximum, -jnp.inf, True))
            grow_ref[0, :, cols] = jnp.where((slot == 2) | (slot == 5), cmax, base)

    gt = gates_pre()
    conv_matmul(0)
    for c0 in range(MXU_COLS, 2 * D_MODEL, MXU_COLS):
        conv_matmul(c0)
        conv_silu(c0 - MXU_COLS)
    plain(2 * D_MODEL)
    conv_silu(2 * D_MODEL - MXU_COLS)
    gates_row(gt)
    for c0 in range(2 * D_MODEL + MXU_COLS, Z_COLS, MXU_COLS):
        plain(c0)
    vt_ref[0] = lax.dot_general(w_ref[2 * D_MODEL:3 * D_MODEL, :], hb[...], _NT,
                                preferred_element_type=_F32).astype(_BF)


def _in_proj_call(x, norm_g, sc, sh, w_bf, wg_bf, bias_g, conv_w, conv_b, tm):
    B, S, D = x.shape
    r = tm // HALO
    last = S // HALO - 1
    return pl.pallas_call(
        _in_proj_kernel,
        grid=(B, S // tm),
        in_specs=[pl.BlockSpec((1, tm, D), lambda b, i: (b, i, 0)),
                  pl.BlockSpec((1, HALO, D), lambda b, i: (b, jnp.maximum(i * r - 1, 0), 0)),
                  pl.BlockSpec((1, HALO, D),
                               lambda b, i: (b, jnp.minimum((i + 1) * r, last), 0)),
                  _const_spec((1, D)),
                  pl.BlockSpec((1, 1, D), lambda b, i: (b, 0, 0)),
                  pl.BlockSpec((1, 1, D), lambda b, i: (b, 0, 0)),
                  _const_spec(w_bf.shape),
                  _const_spec((LANES, D)),
                  _const_spec((1, LANES)),
                  _const_spec((3, 2 * D)),
                  _const_spec((1, 2 * D))],
        out_specs=[pl.BlockSpec((1, tm, Z_COLS), lambda b, i: (b, i, 0)),
                   pl.BlockSpec((1, D, tm), lambda b, i: (b, 0, i)),
                   pl.BlockSpec((1, HEADS * GATE_SLOTS, tm), lambda b, i: (b, 0, i))],
        out_shape=[jax.ShapeDtypeStruct((B, S, Z_COLS), _BF),
                   jax.ShapeDtypeStruct((B, D, S), _BF),
                   jax.ShapeDtypeStruct((B, HEADS * GATE_SLOTS, S), _F32)],
        scratch_shapes=[pltpu.VMEM((tm + 2 * HALO, 2 * D), _F32),
                        pltpu.VMEM((tm + 2 * HALO, D), _BF)],
        compiler_params=pltpu.CompilerParams(
            dimension_semantics=("arbitrary", "arbitrary"),
            vmem_limit_bytes=VMEM_LIMIT),
        name="in_proj",
    )(x, x, x, norm_g, sc, sh, w_bf, wg_bf, bias_g, conv_w, conv_b)


def _mlstm_kernel(q_ref, k_ref, vt_ref, zo_ref, gr_ref, mhg_ref,
                  y_ref,
                  pvf_s, pvb_s, u_s, snap_s, ct_s, mh_s, dec_s, ht_s):
    S = q_ref.shape[1]
    L = MLSTM_CHUNK
    NC = S // L
    dh = HEAD_DIM
    R = STATE_ROWS
    K0 = (0, 3)
    LAST = (L - 1, 0)
    NB = R - dh

    def n_rows(d):
        return slice(d * NB, (d + 1) * NB)

    def c_rows(d, e0):
        start = 2 * NB + 2 * e0 + d * STATE_BLOCK
        return slice(start, start + STATE_BLOCK)

    def state_blocks(d):
        return [n_rows(d)] + [c_rows(d, e0) for e0 in range(0, dh, STATE_BLOCK)]

    def gate_rows(gr, d):
        k0 = K0[d]
        return gr[k0:k0 + 1], gr[k0 + 1:k0 + 2], gr[k0 + 2:k0 + 3]

    m = [jnp.zeros((1, 1), _F32), jnp.zeros((1, 1), _F32)]
    for i in range(NC):
        for d, c in enumerate((i, NC - 1 - i)):
            b_row, _, cmax_row = gate_rows(gr_ref[0, 0, :, c * L:(c + 1) * L], d)
            a_last = jnp.maximum(m[d], cmax_row[:, LAST[d]:LAST[d] + 1])
            mh_s[d * NC + c:d * NC + c + 1, :] = jnp.broadcast_to(m[d], (1, LANES))
            dec_s[d * NC + c:d * NC + c + 1, :] = jnp.broadcast_to(jnp.exp(m[d] - a_last),
                                                                  (1, LANES))
            m[d] = b_row[:, LAST[d]:LAST[d] + 1] + a_last

    si = lax.broadcasted_iota(jnp.int32, (L, L), 0)
    ti = lax.broadcasted_iota(jnp.int32, (L, L), 1)
    masks = (si <= ti, si >= ti)
    ones_row = (lax.broadcasted_iota(jnp.int32, (R - dh, L), 0) == 0).astype(_BF)

    def scores(c):
        t0 = pl.multiple_of(c * L, L)
        return lax.dot_general(k_ref[0, pl.ds(t0, L), :], q_ref[0, pl.ds(t0, L), :], _NT,
                               preferred_element_type=_F32)

    def pass0(g, carry):
        s_next = scores(g * PASS0_GROUP)
        for k in range(PASS0_GROUP):
            c = g * PASS0_GROUP + k
            t0 = pl.multiple_of(c * L, L)
            s_t = s_next
            vta = jnp.concatenate([vt_ref[0, :, pl.ds(t0, L)], ones_row], axis=0)
            vta32 = vta.astype(_F32)
            gr = gr_ref[0, 0, :, pl.ds(t0, L)]
            a_rows, vtw = [], []
            for d in range(2):
                _, r_row, cmax_row = gate_rows(gr, d)
                a_rows.append(jnp.maximum(mh_s[pl.ds(d * NC + c, 1), 0:1], cmax_row))
                w_row = jnp.exp(r_row - a_rows[d][:, LAST[d]:LAST[d] + 1])
                vtw.append((vta32 * w_row).astype(_BF))
            stacked = [vtw[d][dh:] for d in range(2)]
            for e0 in range(0, dh, STATE_BLOCK):
                stacked += [vtw[d][e0:e0 + STATE_BLOCK] for d in range(2)]
            u_s[c] = jnp.dot(jnp.concatenate(stacked, axis=0), k_ref[0, pl.ds(t0, L), :],
                             preferred_element_type=_F32).astype(_BF)
            if k + 1 < PASS0_GROUP:
                s_next = scores(c + 1)
            p = []
            for d in range(2):
                r_row = gate_rows(gr, d)[1]
                r_rep = jnp.broadcast_to(r_row, (LANES, L)).T
                r_col = jnp.concatenate([r_rep] * (L // LANES), axis=1)
                p.append((jnp.exp(jnp.where(masks[d], r_col - a_rows[d], -jnp.inf))
                          * s_t).astype(_BF))
            pv = jnp.dot(vta, jnp.concatenate(p, axis=1), preferred_element_type=_F32)
            pvf_s[:, pl.ds(t0, L)] = pv[:, :L]
            pvb_s[:, pl.ds(t0, L)] = pv[:, L:]
        return carry

    lax.fori_loop(0, NC // PASS0_GROUP, pass0, 0)

    ct_s[...] = jnp.zeros_like(ct_s)

    def pass1(i, carry):
        for d, c in enumerate((i, NC - 1 - i)):
            decay = dec_s[pl.ds(d * NC + c, 1), 0:1]
            for rows in state_blocks(d):
                ct = ct_s[rows, :]
                snap_s[c, rows, :] = ct.astype(_BF)
                ct_s[rows, :] = decay * ct + u_s[c, rows, :].astype(_F32)
        return carry

    lax.fori_loop(0, NC, pass1, 0, unroll=2)

    def inter_term(c):
        t0 = pl.multiple_of(c * L, L)
        return lax.dot_general(snap_s[c], q_ref[0, pl.ds(t0, L), :], _NT,
                               preferred_element_type=_F32)

    def pass2(g, carry):
        inter_next = inter_term(g * PASS2_GROUP)
        for k in range(PASS2_GROUP):
            c = g * PASS2_GROUP + k
            t0 = pl.multiple_of(c * L, L)
            inter = inter_next
            if k + 1 < PASS2_GROUP:
                inter_next = inter_term(c + 1)
            gr = gr_ref[0, 0, :, pl.ds(t0, L)]
            w_inter, scale = [], []
            for d, pv_s in enumerate((pvf_s, pvb_s)):
                b_row, _, cmax_row = gate_rows(gr, d)
                m_c = mh_s[pl.ds(d * NC + c, 1), 0:1]
                a_row = jnp.maximum(m_c, cmax_row)
                w_inter.append(jnp.exp(m_c - a_row))
                den = (w_inter[d] * inter[d * NB:d * NB + 1]
                       + pv_s[dh:dh + 1, pl.ds(t0, L)])
                scale.append(1.0 / jnp.maximum(jnp.abs(den), jnp.exp(-(b_row + a_row))))
            ss = jnp.zeros((1, L), _F32)
            for e0 in range(0, dh, STATE_BLOCK):
                blk = None
                for d, pv_s in enumerate((pvf_s, pvb_s)):
                    num = (w_inter[d] * inter[c_rows(d, e0)]
                           + pv_s[e0:e0 + STATE_BLOCK, pl.ds(t0, L)])
                    blk = num * scale[d] if blk is None else blk + num * scale[d]
                ss = ss + jnp.sum(blk * blk, axis=0, keepdims=True)
                ht_s[e0:e0 + STATE_BLOCK, :] = blk
            rs = lax.rsqrt(ss * (1.0 / dh) + EPS)
            for t1 in range(0, L, LANES):
                h = (ht_s[:, t1:t1 + LANES] * rs[:, t1:t1 + LANES]).T
                tq = pl.multiple_of(t0 + t1, LANES)
                o = zo_ref[0, pl.ds(tq, LANES), :].astype(_F32)
                y_ref[0, pl.ds(tq, LANES), :] = (h * mhg_ref[...] * o).astype(_BF)
        return carry

    lax.fori_loop(0, NC // PASS2_GROUP, pass2, 0)


def _mlstm_call(z, vt, grow, mh_g):
    B, S, _ = z.shape
    dh = HEAD_DIM
    H = HEADS
    nc = S // MLSTM_CHUNK

    def zspec(off):
        return pl.BlockSpec((1, S, dh), lambda b, h, off=off: (b, 0, off + h))

    return pl.pallas_call(
        _mlstm_kernel,
        grid=(B, H),
        in_specs=[zspec(0), zspec(H),
                  pl.BlockSpec((1, dh, S), lambda b, h: (b, h, 0)),
                  zspec(2 * H),
                  pl.BlockSpec((1, 1, GATE_SLOTS, S), lambda b, h: (b, h, 0, 0)),
                  pl.BlockSpec((1, dh), lambda b, h: (0, h))],
        out_specs=pl.BlockSpec((1, S, dh), lambda b, h: (b, 0, h)),
        out_shape=jax.ShapeDtypeStruct((B, S, H * dh), _BF),
        scratch_shapes=[pltpu.VMEM((STATE_ROWS, S), _F32), pltpu.VMEM((STATE_ROWS, S), _F32),
                        pltpu.VMEM((nc, 2 * STATE_ROWS, dh), _BF),
                        pltpu.VMEM((nc, 2 * STATE_ROWS, dh), _BF),
                        pltpu.VMEM((2 * STATE_ROWS, dh), _F32),
                        pltpu.VMEM((2 * nc, LANES), _F32), pltpu.VMEM((2 * nc, LANES), _F32),
                        pltpu.VMEM((dh, MLSTM_CHUNK), _F32)],
        compiler_params=pltpu.CompilerParams(
            dimension_semantics=("arbitrary", "arbitrary"),
            vmem_limit_bytes=VMEM_LIMIT),
        name="mlstm",
    )(z, z, vt, z, grow, mh_g)


def _mix_kernel(x_ref, zu_ref, zv_ref, zga_ref, zgb_ref, ya_ref, g1_ref, lng_ref, lnb_ref,
                ws_ref, bs_ref, wo_ref, o_ref, mix_s):
    tm = x_ref.shape[1]
    P = SGU_CHUNK
    C = SGU_GROUP_DIM
    for j in range(tm // P):
        rows = slice(j * P, (j + 1) * P)
        gv = zv_ref[0, rows, :].astype(_F32)
        mu = jnp.mean(gv, axis=-1, keepdims=True)
        dv = gv - mu
        var = jnp.mean(dv * dv, axis=-1, keepdims=True)
        vn = (dv * lax.rsqrt(var + EPS) * lng_ref[...] + lnb_ref[...]).astype(_BF)
        s = jnp.concatenate(
            [jnp.dot(ws_ref[g], vn[:, g * C:(g + 1) * C], preferred_element_type=_F32)
             for g in range(SGU_GROUPS)], axis=1) + bs_ref[...]
        yb = zu_ref[0, rows, :].astype(_F32) * s
        ga = zga_ref[0, rows, :].astype(_F32)
        gb = zgb_ref[0, rows, :].astype(_F32)
        mix_s[rows, :] = (ga * ya_ref[0, rows, :].astype(_F32) + gb * yb).astype(_BF)
    upd = jnp.dot(mix_s[...], wo_ref[...], preferred_element_type=_F32)
    o_ref[0] = x_ref[0] + g1_ref[0] * upd


def _mix_call(x, z, ya, g1, ln_g, ln_b, ws_bf, bs_full, wo_bf, tm):
    B, S, D = x.shape

    def zspec(j):
        return pl.BlockSpec((1, tm, D), lambda b, i, j=j: (b, i, j))

    return pl.pallas_call(
        _mix_kernel,
        grid=(B, S // tm),
        in_specs=[zspec(0), zspec(3), zspec(4), zspec(5), zspec(6), zspec(0),
                  pl.BlockSpec((1, 1, D), lambda b, i: (b, 0, 0)),
                  _const_spec((1, D)), _const_spec((1, D)),
                  _const_spec((SGU_GROUPS, SGU_CHUNK, SGU_CHUNK)),
                  _const_spec((SGU_CHUNK, D)),
                  _const_spec((D, D))],
        out_specs=pl.BlockSpec((1, tm, D), lambda b, i: (b, i, 0)),
        out_shape=jax.ShapeDtypeStruct((B, S, D), _F32),
        scratch_shapes=[pltpu.VMEM((tm, D), _BF)],
        compiler_params=pltpu.CompilerParams(
            dimension_semantics=("arbitrary", "arbitrary"),
            vmem_limit_bytes=VMEM_LIMIT),
        name="mix",
    )(x, z, z, z, z, ya, g1, ln_g, ln_b, ws_bf, bs_full, wo_bf)


def _ffn_kernel(x_ref, g_ref, sc_ref, sh_ref, g2_ref, w1_ref, w2_ref, nf_ref, o_ref):
    x = x_ref[0]
    ms = jnp.mean(x * x, axis=-1, keepdims=True)
    gain = g_ref[...] * (1.0 + sc_ref[0])
    hb = (x * lax.rsqrt(ms + EPS) * gain + sh_ref[0]).astype(_BF)
    ff = jnp.zeros(x.shape, _F32)
    for j in range(FF_DIM // D_MODEL):
        cols = slice(j * D_MODEL, (j + 1) * D_MODEL)
        a = jnp.maximum(jnp.dot(hb, w1_ref[:, cols], preferred_element_type=_F32), 0.0)
        ff = ff + jnp.dot((a * a).astype(_BF), w2_ref[cols, :], preferred_element_type=_F32)
    x2 = x + g2_ref[0] * ff
    ms2 = jnp.mean(x2 * x2, axis=-1, keepdims=True)
    o_ref[0] = x2 * lax.rsqrt(ms2 + EPS) * nf_ref[...]


def _ffn_call(x1, norm_g, sc, sh, g2, w1_bf, w2_bf, normf_g, tm):
    B, S, D = x1.shape
    vec = pl.BlockSpec((1, 1, D), lambda b, i: (b, 0, 0))
    return pl.pallas_call(
        _ffn_kernel,
        grid=(B, S // tm),
        in_specs=[pl.BlockSpec((1, tm, D), lambda b, i: (b, i, 0)),
                  _const_spec((1, D)), vec, vec, vec,
                  _const_spec((D, FF_DIM)), _const_spec((FF_DIM, D)),
                  _const_spec((1, D))],
        out_specs=pl.BlockSpec((1, tm, D), lambda b, i: (b, i, 0)),
        out_shape=jax.ShapeDtypeStruct((B, S, D), _F32),
        compiler_params=pltpu.CompilerParams(
            dimension_semantics=("arbitrary", "arbitrary"),
            vmem_limit_bytes=VMEM_LIMIT),
        name="ffn",
    )(x1, norm_g, sc, sh, g2, w1_bf, w2_bf, normf_g)


def _tail_kernel(x_ref, zu_ref, zv_ref, zga_ref, zgb_ref, ya_ref, g1_ref, lng_ref, lnb_ref,
                 ws_ref, bs_ref, wo_ref, g_ref, sc_ref, sh_ref, g2_ref, w1_ref, w2_ref, nf_ref,
                 o_ref, mix_s):
    t = pl.program_id(0)
    tm = x_ref.shape[1]
    P = SGU_CHUNK
    C = SGU_GROUP_DIM
    n_seg = FF_DIM // D_MODEL
    assert tm // P == n_seg
    new = mix_s.at[t % 2]
    old = mix_s.at[(t + 1) % 2]

    @pl.when(t == 0)
    def _():
        mix_s[1] = jnp.zeros((tm, D_MODEL), _BF)

    def merge(j):
        rows = slice(j * P, (j + 1) * P)
        gv = zv_ref[0, rows, :].astype(_F32)
        mu = jnp.mean(gv, axis=-1, keepdims=True)
        dv = gv - mu
        var = jnp.mean(dv * dv, axis=-1, keepdims=True)
        vn = (dv * lax.rsqrt(var + EPS) * lng_ref[...] + lnb_ref[...]).astype(_BF)
        s = jnp.concatenate(
            [jnp.dot(ws_ref[g], vn[:, g * C:(g + 1) * C], preferred_element_type=_F32)
             for g in range(SGU_GROUPS)], axis=1) + bs_ref[...]
        yb = zu_ref[0, rows, :].astype(_F32) * s
        ga = zga_ref[0, rows, :].astype(_F32)
        gb = zgb_ref[0, rows, :].astype(_F32)
        new[rows, :] = (ga * ya_ref[0, rows, :].astype(_F32) + gb * yb).astype(_BF)

    x1 = x_ref[0] + g1_ref[0] * jnp.dot(old[...], wo_ref[...], preferred_element_type=_F32)
    ms = jnp.mean(x1 * x1, axis=-1, keepdims=True)
    gain = g_ref[...] * (1.0 + sc_ref[0])
    hb = (x1 * lax.rsqrt(ms + EPS) * gain + sh_ref[0]).astype(_BF)
    ff = jnp.zeros(x1.shape, _F32)
    for j in range(n_seg):
        cols = slice(j * D_MODEL, (j + 1) * D_MODEL)
        a = jnp.maximum(jnp.dot(hb, w1_ref[:, cols], preferred_element_type=_F32), 0.0)
        merge(j)
        ff = ff + jnp.dot((a * a).astype(_BF), w2_ref[cols, :], preferred_element_type=_F32)
    x2 = x1 + g2_ref[0] * ff
    ms2 = jnp.mean(x2 * x2, axis=-1, keepdims=True)
    o_ref[0] = x2 * lax.rsqrt(ms2 + EPS) * nf_ref[...]


def _tail_call(x, z, ya, g1, ln_g, ln_b, ws_bf, bs_full, wo_bf,
               norm_g, sc, sh, g2, w1_bf, w2_bf, normf_g, tm):
    B, S, D = x.shape
    tpb = S // tm
    nt = B * tpb

    def merged(j):
        def index(t):
            n = jnp.minimum(t, nt - 1)
            return n // tpb, n % tpb, j
        return pl.BlockSpec((1, tm, D), index)

    def done_tile(t):
        o = jnp.maximum(t - 1, 0)
        return o // tpb, o % tpb, 0

    tile = pl.BlockSpec((1, tm, D), done_tile)
    vec = pl.BlockSpec((1, 1, D), lambda t: (done_tile(t)[0], 0, 0))
    return pl.pallas_call(
        _tail_kernel,
        grid=(nt + 1,),
        in_specs=[tile, merged(3), merged(4), merged(5), merged(6), merged(0), vec,
                  _const_spec((1, D)), _const_spec((1, D)),
                  _const_spec((SGU_GROUPS, SGU_CHUNK, SGU_CHUNK)),
                  _const_spec((SGU_CHUNK, D)),
                  _const_spec((D, D)),
                  _const_spec((1, D)), vec, vec, vec,
                  _const_spec((D, FF_DIM)), _const_spec((FF_DIM, D)),
                  _const_spec((1, D))],
        out_specs=tile,
        out_shape=jax.ShapeDtypeStruct((B, S, D), _F32),
        scratch_shapes=[pltpu.VMEM((2, tm, D), _BF)],
        compiler_params=pltpu.CompilerParams(
            dimension_semantics=("arbitrary",),
            vmem_limit_bytes=VMEM_LIMIT),
        name="tail",
    )(x, z, z, z, z, ya, g1, ln_g, ln_b, ws_bf, bs_full, wo_bf,
      norm_g, sc, sh, g2, w1_bf, w2_bf, normf_g)


def _gate_weights(w_if_t, b_if):
    H = HEADS
    idx_f, idx_i = [], []
    for h in range(H):
        idx_f += [H + h] * 3 + [3 * H + h] * 3 + [None] * 2
        idx_i += [None, h, h, None, 2 * H + h, 2 * H + h, None, None]
    sel = np.zeros((LANES, 4 * H), np.float32)
    for row, src in enumerate(idx_f + idx_i):
        if src is not None:
            sel[row, src] = 1.0
    w_g = (sel[:, :, None] * w_if_t[None]).sum(axis=1)
    b_g = (sel * b_if.reshape(1, -1)).sum(axis=1).reshape(1, LANES)
    return w_g, b_g


def kernel(x, c, w_ada, b_ada, norm1_g, norm2_g, w_in, b_if, conv_w, conv_b, mh_g,
           ln_v_g, ln_v_b, w_s, b_s, w_out, w1, w2, normf_g):
    B, S, D = x.shape
    H = HEADS
    assert w_ada.shape[0] == 1, "single layer"

    mod = _mod_call(c, w_ada[0], b_ada[0])
    sh1, sc1, g1, sh2, sc2, g2 = (mod[:, k * D:(k + 1) * D].reshape(B, 1, D) for k in range(6))

    w_in_t = w_in[0].T
    w_g, b_g = _gate_weights(w_in_t[8 * D:], b_if[0])
    z, vt, grow = _in_proj_call(x, norm1_g, sc1, sh1, w_in_t.astype(_BF), w_g.astype(_BF), b_g,
                                conv_w[0], conv_b, tm=IN_PROJ_ROWS)

    ya = _mlstm_call(z, vt, grow.reshape(B, H, GATE_SLOTS, S), mh_g)

    bs_full = jnp.repeat(b_s[0].T, SGU_GROUP_DIM, axis=1)
    return _tail_call(x, z, ya, g1, ln_v_g, ln_v_b, w_s[0].astype(_BF), bs_full,
                      w_out[0].astype(_BF), norm2_g, sc2, sh2, g2, w1[0].astype(_BF),
                      w2[0].astype(_BF), normf_g.reshape(1, D), tm=TAIL_ROWS)
```

```python
import jax
import jax.numpy as jnp
import numpy as np
from jax import lax
from jax.experimental import pallas as pl
from jax.experimental.pallas import tpu as pltpu

D_MODEL = 1024
HEADS = 4
HEAD_DIM = 256
SGU_GROUPS = 8
SGU_GROUP_DIM = 128
SGU_CHUNK = 128
FF_DIM = 4096
EPS = 1e-6
Z_COLS = 7 * D_MODEL

MLSTM_CHUNK = 256
F32_ROWS = 8
BF16_ROWS = 16
STATE_ROWS = HEAD_DIM + BF16_ROWS
STATE_BLOCK = 64
PASS0_GROUP = 8
PASS2_GROUP = 16
HALO = BF16_ROWS
GATE_SLOTS = 8
CONV_ROWS = 64
CONV_COLS = 128
MXU_COLS = 256
LANES = 128

VMEM_LIMIT = 56 * 1024 * 1024
IN_PROJ_ROWS = 512
TAIL_ROWS = 512

_BF = jnp.bfloat16
_F32 = jnp.float32
_NT = (((1,), (1,)), ((), ()))
_LOG2E = 1.4426950408889634


def _const_spec(shape):
    nd = len(shape)
    return pl.BlockSpec(shape, lambda *_: (0,) * nd, pipeline_mode=pl.Buffered(1))


def _sigmoid(x):
    return 1.0 / (1.0 + jnp.exp(-x))


def _log_sigmoid(x):
    return jnp.minimum(x, 0.0) - jnp.log(1.0 + jnp.exp(-jnp.abs(x)))


def _gelu(x):
    return 0.5 * x * (1.0 + lax.erf(x * (2.0 ** -0.5)))


def _mod_kernel(c_ref, w_ref, b_ref, o_ref):
    c = c_ref[...]
    ca = c * _sigmoid(c)
    o_ref[...] = jnp.dot(ca, w_ref[...], precision=lax.Precision.HIGHEST,
                         preferred_element_type=_F32) + b_ref[...]


def _mod_call(c, w_ada, b_ada):
    B, D = c.shape
    N = w_ada.shape[1]
    tn = N // 4
    return pl.pallas_call(
        _mod_kernel,
        grid=(N // tn,),
        in_specs=[pl.BlockSpec((B, D), lambda j: (0, 0)),
                  pl.BlockSpec((D, tn), lambda j: (0, j)),
                  pl.BlockSpec((1, tn), lambda j: (0, j))],
        out_specs=pl.BlockSpec((B, tn), lambda j: (0, j)),
        out_shape=jax.ShapeDtypeStruct((B, N), _F32),
        name="mod",
    )(c, w_ada, b_ada.reshape(1, N))


def _lane_scan(x, op, ident, reverse):
    L = x.shape[1]
    lane = lax.broadcasted_iota(jnp.int32, x.shape, 1)
    d = 1
    while d < L:
        if reverse:
            shifted = jnp.where(lane < L - d, pltpu.roll(x, L - d, axis=1), ident)
        else:
            shifted = jnp.where(lane >= d, pltpu.roll(x, d, axis=1), ident)
        x = op(x, shifted)
        d *= 2
    return x


def _in_proj_kernel(x_ref, xp_ref, xn_ref, g_ref, sc_ref, sh_ref, w_ref, wg_ref,
                    bg_ref, cw_ref, cb_ref, z_ref, vt_ref, grow_ref, qk_s, hb_s):
    tm = x_ref.shape[1]
    L = MLSTM_CHUNK
    i = pl.program_id(1)

    gain = g_ref[...] * (1.0 + sc_ref[0])

    def norm_mod(x):
        ms = jnp.mean(x * x, axis=-1, keepdims=True)
        return (x * lax.rsqrt(ms + EPS) * gain + sh_ref[0]).astype(_BF)

    hb_s[HALO:HALO + tm, :] = norm_mod(x_ref[0])
    hb_s[:HALO, :] = jnp.where(i > 0, norm_mod(xp_ref[0]), jnp.zeros((HALO, D_MODEL), _BF))
    hb_s[HALO + tm:, :] = jnp.where(i < pl.num_programs(1) - 1, norm_mod(xn_ref[0]),
                                    jnp.zeros((HALO, D_MODEL), _BF))
    hb = hb_s.at[HALO:HALO + tm, :]

    def plain(c0):
        cols = slice(c0, c0 + MXU_COLS)
        wrows = slice(c0 + D_MODEL, c0 + D_MODEL + MXU_COLS)
        r = lax.dot_general(hb[...], w_ref[wrows, :], _NT, preferred_element_type=_F32)
        if 3 * D_MODEL <= c0 < 5 * D_MODEL:
            r = _gelu(r)
        else:
            r = 1.0 / (1.0 + jnp.exp2(r * -_LOG2E))
        z_ref[0, :, cols] = r.astype(_BF)

    def conv_matmul(c0):
        qk_s[:, c0:c0 + MXU_COLS] = lax.dot_general(
            hb_s[...], w_ref[c0:c0 + MXU_COLS, :], _NT, preferred_element_type=_F32)

    def conv_silu(c0):
        n = CONV_ROWS + 2 * F32_ROWS
        mid = slice(F32_ROWS, F32_ROWS + CONV_ROWS)
        for c1 in range(c0, c0 + MXU_COLS, CONV_COLS):
            cols = slice(c1, c1 + CONV_COLS)
            for r0 in range(0, tm, CONV_ROWS):
                zb = qk_s[pl.ds(HALO - F32_ROWS + r0, n), cols]
                y = cb_ref[:, cols] + pltpu.roll(zb, 1, axis=0)[mid] * cw_ref[0:1, cols]
                y = y + zb[mid] * cw_ref[1:2, cols]
                y = y + pltpu.roll(zb, n - 1, axis=0)[mid] * cw_ref[2:3, cols]
                y = y / (1.0 + jnp.exp2(y * -_LOG2E))
                if c1 < D_MODEL:
                    y = y * (HEAD_DIM ** -0.5)
                z_ref[0, pl.ds(r0, CONV_ROWS), cols] = y.astype(_BF)

    def gates_pre():
        gc = lax.dot_general(hb[...], wg_ref[...], _NT,
                             preferred_element_type=_F32) + bg_ref[...]
        return gc.T

    def gates_row(gt):
        nr = HEADS * GATE_SLOTS
        lst = _log_sigmoid(gt[:nr])
        git = gt[nr:2 * nr]
        slot = lax.broadcasted_iota(jnp.int32, (nr, L), 0) & (GATE_SLOTS - 1)
        fwd = slot < 3
        for c in range(tm // L):
            cols = slice(c * L, (c + 1) * L)
            lsc = lst[:, cols]
            cum = jnp.where(fwd, _lane_scan(lsc, jnp.add, 0.0, False),
                            _lane_scan(lsc, jnp.add, 0.0, True))
            base = jnp.where((slot == 0) | (slot == 3), cum, git[:, cols] - cum)
            cmax = jnp.where(fwd, _lane_scan(base, jnp.maximum, -jnp.inf, False),
                             _lane_scan(base, jnp.maximum, -jnp.inf, True))
            grow_ref[0, :, cols] = jnp.where((slot == 2) | (slot == 5), cmax, base)

    gt = gates_pre()
    conv_matmul(0)
    for c0 in range(MXU_COLS, 2 * D_MODEL, MXU_COLS):
        conv_matmul(c0)
        conv_silu(c0 - MXU_COLS)
    plain(2 * D_MODEL)
    conv_silu(2 * D_MODEL - MXU_COLS)
    gates_row(gt)
    for c0 in range(2 * D_MODEL + MXU_COLS, Z_COLS, MXU_COLS):
        plain(c0)
    vt_ref[0] = lax.dot_general(w_ref[2 * D_MODEL:3 * D_MODEL, :], hb[...], _NT,
                                preferred_element_type=_F32).astype(_BF)


def _in_proj_call(x, norm_g, sc, sh, w_bf, wg_bf, bias_g, conv_w, conv_b, tm):
    B, S, D = x.shape
    r = tm // HALO
    last = S // HALO - 1
    return pl.pallas_call(
        _in_proj_kernel,
        grid=(B, S // tm),
        in_specs=[pl.BlockSpec((1, tm, D), lambda b, i: (b, i, 0)),
                  pl.BlockSpec((1, HALO, D), lambda b, i: (b, jnp.maximum(i * r - 1, 0), 0)),
                  pl.BlockSpec((1, HALO, D),
                               lambda b, i: (b, jnp.minimum((i + 1) * r, last), 0)),
                  _const_spec((1, D)),
                  pl.BlockSpec((1, 1, D), lambda b, i: (b, 0, 0)),
                  pl.BlockSpec((1, 1, D), lambda b, i: (b, 0, 0)),
                  _const_spec(w_bf.shape),
                  _const_spec((LANES, D)),
                  _const_spec((1, LANES)),
                  _const_spec((3, 2 * D)),
                  _const_spec((1, 2 * D))],
        out_specs=[pl.BlockSpec((1, tm, Z_COLS), lambda b, i: (b, i, 0)),
                   pl.BlockSpec((1, D, tm), lambda b, i: (b, 0, i)),
                   pl.BlockSpec((1, HEADS * GATE_SLOTS, tm), lambda b, i: (b, 0, i))],
        out_shape=[jax.ShapeDtypeStruct((B, S, Z_COLS), _BF),
                   jax.ShapeDtypeStruct((B, D, S), _BF),
                   jax.ShapeDtypeStruct((B, HEADS * GATE_SLOTS, S), _F32)],
        scratch_shapes=[pltpu.VMEM((tm + 2 * HALO, 2 * D), _F32),
                        pltpu.VMEM((tm + 2 * HALO, D), _BF)],
        compiler_params=pltpu.CompilerParams(
            dimension_semantics=("arbitrary", "arbitrary"),
            vmem_limit_bytes=VMEM_LIMIT),
        name="in_proj",
    )(x, x, x, norm_g, sc, sh, w_bf, wg_bf, bias_g, conv_w, conv_b)


def _mlstm_kernel(q_ref, k_ref, vt_ref, zo_ref, gr_ref, mhg_ref,
                  y_ref,
                  pvf_s, pvb_s, u_s, snap_s, ct_s, mh_s, dec_s, ht_s):
    S = q_ref.shape[1]
    L = MLSTM_CHUNK
    NC = S // L
    dh = HEAD_DIM
    R = STATE_ROWS
    K0 = (0, 3)
    LAST = (L - 1, 0)
    NB = R - dh

    def n_rows(d):
        return slice(d * NB, (d + 1) * NB)

    def c_rows(d, e0):
        start = 2 * NB + 2 * e0 + d * STATE_BLOCK
        return slice(start, start + STATE_BLOCK)

    def state_blocks(d):
        return [n_rows(d)] + [c_rows(d, e0) for e0 in range(0, dh, STATE_BLOCK)]

    def gate_rows(gr, d):
        k0 = K0[d]
        return gr[k0:k0 + 1], gr[k0 + 1:k0 + 2], gr[k0 + 2:k0 + 3]

    m = [jnp.zeros((1, 1), _F32), jnp.zeros((1, 1), _F32)]
    for i in range(NC):
        for d, c in enumerate((i, NC - 1 - i)):
            b_row, _, cmax_row = gate_rows(gr_ref[0, 0, :, c * L:(c + 1) * L], d)
            a_last = jnp.maximum(m[d], cmax_row[:, LAST[d]:LAST[d] + 1])
            mh_s[d * NC + c:d * NC + c + 1, :] = jnp.broadcast_to(m[d], (1, LANES))
            dec_s[d * NC + c:d * NC + c + 1, :] = jnp.broadcast_to(jnp.exp(m[d] - a_last),
                                                                  (1, LANES))
            m[d] = b_row[:, LAST[d]:LAST[d] + 1] + a_last

    si = lax.broadcasted_iota(jnp.int32, (L, L), 0)
    ti = lax.broadcasted_iota(jnp.int32, (L, L), 1)
    masks = (si <= ti, si >= ti)
    ones_row = (lax.broadcasted_iota(jnp.int32, (R - dh, L), 0) == 0).astype(_BF)

    def scores(c):
        t0 = pl.multiple_of(c * L, L)
        return lax.dot_general(k_ref[0, pl.ds(t0, L), :], q_ref[0, pl.ds(t0, L), :], _NT,
                               preferred_element_type=_F32)

    def pass0(g, carry):
        s_next = scores(g * PASS0_GROUP)
        for k in range(PASS0_GROUP):
            c = g * PASS0_GROUP + k
            t0 = pl.multiple_of(c * L, L)
            s_t = s_next
            vta = jnp.concatenate([vt_ref[0, :, pl.ds(t0, L)], ones_row], axis=0)
            vta32 = vta.astype(_F32)
            gr = gr_ref[0, 0, :, pl.ds(t0, L)]
            a_rows, vtw = [], []
            for d in range(2):
                _, r_row, cmax_row = gate_rows(gr, d)
                a_rows.append(jnp.maximum(mh_s[pl.ds(d * NC + c, 1), 0:1], cmax_row))
                w_row = jnp.exp(r_row - a_rows[d][:, LAST[d]:LAST[d] + 1])
                vtw.append((vta32 * w_row).astype(_BF))
            stacked = [vtw[d][dh:] for d in range(2)]
            for e0 in range(0, dh, STATE_BLOCK):
                stacked += [vtw[d][e0:e0 + STATE_BLOCK] for d in range(2)]
            u_s[c] = jnp.dot(jnp.concatenate(stacked, axis=0), k_ref[0, pl.ds(t0, L), :],
                             preferred_element_type=_F32).astype(_BF)
            if k + 1 < PASS0_GROUP:
                s_next = scores(c + 1)
            p = []
            for d in range(2):
                r_row = gate_rows(gr, d)[1]
                r_rep = jnp.broadcast_to(r_row, (LANES, L)).T
                r_col = jnp.concatenate([r_rep] * (L // LANES), axis=1)
                p.append((jnp.exp(jnp.where(masks[d], r_col - a_rows[d], -jnp.inf))
                          * s_t).astype(_BF))
            pv = jnp.dot(vta, jnp.concatenate(p, axis=1), preferred_element_type=_F32)
            pvf_s[:, pl.ds(t0, L)] = pv[:, :L]
            pvb_s[:, pl.ds(t0, L)] = pv[:, L:]
        return carry

    lax.fori_loop(0, NC // PASS0_GROUP, pass0, 0)

    ct_s[...] = jnp.zeros_like(ct_s)

    def pass1(i, carry):
        for d, c in enumerate((i, NC - 1 - i)):
            decay = dec_s[pl.ds(d * NC + c, 1), 0:1]
            for rows in state_blocks(d):
                ct = ct_s[rows, :]
                snap_s[c, rows, :] = ct.astype(_BF)
                ct_s[rows, :] = decay * ct + u_s[c, rows, :].astype(_F32)
        return carry

    lax.fori_loop(0, NC, pass1, 0, unroll=2)

    def inter_term(c):
        t0 = pl.multiple_of(c * L, L)
        return lax.dot_general(snap_s[c], q_ref[0, pl.ds(t0, L), :], _NT,
                               preferred_element_type=_F32)

    def pass2(g, carry):
        inter_next = inter_term(g * PASS2_GROUP)
        for k in range(PASS2_GROUP):
            c = g * PASS2_GROUP + k
            t0 = pl.multiple_of(c * L, L)
            inter = inter_next
            if k + 1 < PASS2_GROUP:
                inter_next = inter_term(c + 1)
            gr = gr_ref[0, 0, :, pl.ds(t0, L)]
            w_inter, scale = [], []
            for d, pv_s in enumerate((pvf_s, pvb_s)):
                b_row, _, cmax_row = gate_rows(gr, d)
                m_c = mh_s[pl.ds(d * NC + c, 1), 0:1]
                a_row = jnp.maximum(m_c, cmax_row)
                w_inter.append(jnp.exp(m_c - a_row))
                den = (w_inter[d] * inter[d * NB:d * NB + 1]
                       + pv_s[dh:dh + 1, pl.ds(t0, L)])
                scale.append(1.0 / jnp.maximum(jnp.abs(den), jnp.exp(-(b_row + a_row))))
            ss = jnp.zeros((1, L), _F32)
            for e0 in range(0, dh, STATE_BLOCK):
                blk = None
                for d, pv_s in enumerate((pvf_s, pvb_s)):
                    num = (w_inter[d] * inter[c_rows(d, e0)]
                           + pv_s[e0:e0 + STATE_BLOCK, pl.ds(t0, L)])
                    blk = num * scale[d] if blk is None else blk + num * scale[d]
                ss = ss + jnp.sum(blk * blk, axis=0, keepdims=True)
                ht_s[e0:e0 + STATE_BLOCK, :] = blk
            rs = lax.rsqrt(ss * (1.0 / dh) + EPS)
            for t1 in range(0, L, LANES):
                h = (ht_s[:, t1:t1 + LANES] * rs[:, t1:t1 + LANES]).T
                tq = pl.multiple_of(t0 + t1, LANES)
                o = zo_ref[0, pl.ds(tq, LANES), :].astype(_F32)
                y_ref[0, pl.ds(tq, LANES), :] = (h * mhg_ref[...] * o).astype(_BF)
        return carry

    lax.fori_loop(0, NC // PASS2_GROUP, pass2, 0)


def _mlstm_call(z, vt, grow, mh_g):
    B, S, _ = z.shape
    dh = HEAD_DIM
    H = HEADS
    nc = S // MLSTM_CHUNK

    def zspec(off):
        return pl.BlockSpec((1, S, dh), lambda b, h, off=off: (b, 0, off + h))

    return pl.pallas_call(
        _mlstm_kernel,
        grid=(B, H),
        in_specs=[zspec(0), zspec(H),
                  pl.BlockSpec((1, dh, S), lambda b, h: (b, h, 0)),
                  zspec(2 * H),
                  pl.BlockSpec((1, 1, GATE_SLOTS, S), lambda b, h: (b, h, 0, 0)),
                  pl.BlockSpec((1, dh), lambda b, h: (0, h))],
        out_specs=pl.BlockSpec((1, S, dh), lambda b, h: (b, 0, h)),
        out_shape=jax.ShapeDtypeStruct((B, S, H * dh), _BF),
        scratch_shapes=[pltpu.VMEM((STATE_ROWS, S), _F32), pltpu.VMEM((STATE_ROWS, S), _F32),
                        pltpu.VMEM((nc, 2 * STATE_ROWS, dh), _BF),
                        pltpu.VMEM((nc, 2 * STATE_ROWS, dh), _BF),
                        pltpu.VMEM((2 * STATE_ROWS, dh), _F32),
                        pltpu.VMEM((2 * nc, LANES), _F32), pltpu.VMEM((2 * nc, LANES), _F32),
                        pltpu.VMEM((dh, MLSTM_CHUNK), _F32)],
        compiler_params=pltpu.CompilerParams(
            dimension_semantics=("arbitrary", "arbitrary"),
            vmem_limit_bytes=VMEM_LIMIT),
        name="mlstm",
    )(z, z, vt, z, grow, mh_g)


def _tail_kernel(x_ref, zu_ref, zv_ref, zga_ref, zgb_ref, ya_ref, g1_ref, lng_ref, lnb_ref,
                 ws_ref, bs_ref, wo_ref, g_ref, sc_ref, sh_ref, g2_ref, w1_ref, w2_ref, nf_ref,
                 o_ref, mix_s):
    t = pl.program_id(0)
    tm = x_ref.shape[1]
    P = SGU_CHUNK
    C = SGU_GROUP_DIM
    n_seg = FF_DIM // D_MODEL
    assert tm // P == n_seg
    new = mix_s.at[t % 2]
    old = mix_s.at[(t + 1) % 2]

    @pl.when(t == 0)
    def _():
        mix_s[1] = jnp.zeros((tm, D_MODEL), _BF)

    def merge(j):
        rows = slice(j * P, (j + 1) * P)
        gv = zv_ref[0, rows, :].astype(_F32)
        mu = jnp.mean(gv, axis=-1, keepdims=True)
        dv = gv - mu
        var = jnp.mean(dv * dv, axis=-1, keepdims=True)
        vn = (dv * lax.rsqrt(var + EPS) * lng_ref[...] + lnb_ref[...]).astype(_BF)
        s = jnp.concatenate(
            [jnp.dot(ws_ref[g], vn[:, g * C:(g + 1) * C], preferred_element_type=_F32)
             for g in range(SGU_GROUPS)], axis=1) + bs_ref[...]
        yb = zu_ref[0, rows, :].astype(_F32) * s
        ga = zga_ref[0, rows, :].astype(_F32)
        gb = zgb_ref[0, rows, :].astype(_F32)
        new[rows, :] = (ga * ya_ref[0, rows, :].astype(_F32) + gb * yb).astype(_BF)

    x1 = x_ref[0] + g1_ref[0] * jnp.dot(old[...], wo_ref[...], preferred_element_type=_F32)
    ms = jnp.mean(x1 * x1, axis=-1, keepdims=True)
    gain = g_ref[...] * (1.0 + sc_ref[0])
    hb = (x1 * lax.rsqrt(ms + EPS) * gain + sh_ref[0]).astype(_BF)
    ff = jnp.zeros(x1.shape, _F32)
    for j in range(n_seg):
        cols = slice(j * D_MODEL, (j + 1) * D_MODEL)
        a = jnp.maximum(jnp.dot(hb, w1_ref[:, cols], preferred_element_type=_F32), 0.0)
        merge(j)
        ff = ff + jnp.dot((a * a).astype(_BF), w2_ref[cols, :], preferred_element_type=_F32)
    x2 = x1 + g2_ref[0] * ff
    ms2 = jnp.mean(x2 * x2, axis=-1, keepdims=True)
    o_ref[0] = x2 * lax.rsqrt(ms2 + EPS) * nf_ref[...]


def _tail_call(x, z, ya, g1, ln_g, ln_b, ws_bf, bs_full, wo_bf,
               norm_g, sc, sh, g2, w1_bf, w2_bf, normf_g, tm):
    B, S, D = x.shape
    tpb = S // tm
    nt = B * tpb

    def merged(j):
        def index(t):
            n = jnp.minimum(t, nt - 1)
            return n // tpb, n % tpb, j
        return pl.BlockSpec((1, tm, D), index)

    def done_tile(t):
        o = jnp.maximum(t - 1, 0)
        return o // tpb, o % tpb, 0

    tile = pl.BlockSpec((1, tm, D), done_tile)
    vec = pl.BlockSpec((1, 1, D), lambda t: (done_tile(t)[0], 0, 0))
    return pl.pallas_call(
        _tail_kernel,
        grid=(nt + 1,),
        in_specs=[tile, merged(3), merged(4), merged(5), merged(6), merged(0), vec,
                  _const_spec((1, D)), _const_spec((1, D)),
                  _const_spec((SGU_GROUPS, SGU_CHUNK, SGU_CHUNK)),
                  _const_spec((SGU_CHUNK, D)),
                  _const_spec((D, D)),
                  _const_spec((1, D)), vec, vec, vec,
                  _const_spec((D, FF_DIM)), _const_spec((FF_DIM, D)),
                  _const_spec((1, D))],
        out_specs=tile,
        out_shape=jax.ShapeDtypeStruct((B, S, D), _F32),
        scratch_shapes=[pltpu.VMEM((2, tm, D), _BF)],
        compiler_params=pltpu.CompilerParams(
            dimension_semantics=("arbitrary",),
            vmem_limit_bytes=VMEM_LIMIT),
        name="tail",
    )(x, z, z, z, z, ya, g1, ln_g, ln_b, ws_bf, bs_full, wo_bf,
      norm_g, sc, sh, g2, w1_bf, w2_bf, normf_g)


def _gate_weights(w_if_t, b_if):
    H = HEADS
    idx_f, idx_i = [], []
    for h in range(H):
        idx_f += [H + h] * 3 + [3 * H + h] * 3 + [None] * 2
        idx_i += [None, h, h, None, 2 * H + h, 2 * H + h, None, None]
    sel = np.zeros((LANES, 4 * H), np.float32)
    for row, src in enumerate(idx_f + idx_i):
        if src is not None:
            sel[row, src] = 1.0
    w_g = (sel[:, :, None] * w_if_t[None]).sum(axis=1)
    b_g = (sel * b_if.reshape(1, -1)).sum(axis=1).reshape(1, LANES)
    return w_g, b_g


def kernel(x, c, w_ada, b_ada, norm1_g, norm2_g, w_in, b_if, conv_w, conv_b, mh_g,
           ln_v_g, ln_v_b, w_s, b_s, w_out, w1, w2, normf_g):
    B, S, D = x.shape
    H = HEADS
    assert w_ada.shape[0] == 1, "single layer"

    mod = _mod_call(c, w_ada[0], b_ada[0])
    sh1, sc1, g1, sh2, sc2, g2 = (mod[:, k * D:(k + 1) * D].reshape(B, 1, D) for k in range(6))

    w_in_t = w_in[0].T
    w_g, b_g = _gate_weights(w_in_t[8 * D:], b_if[0])
    z, vt, grow = _in_proj_call(x, norm1_g, sc1, sh1, w_in_t.astype(_BF), w_g.astype(_BF), b_g,
                                conv_w[0], conv_b, tm=IN_PROJ_ROWS)

    ya = _mlstm_call(z, vt, grow.reshape(B, H, GATE_SLOTS, S), mh_g)

    bs_full = jnp.repeat(b_s[0].T, SGU_GROUP_DIM, axis=1)
    return _tail_call(x, z, ya, g1, ln_v_g, ln_v_b, w_s[0].astype(_BF), bs_full,
                      w_out[0].astype(_BF), norm2_g, sc2, sh2, g2, w1[0].astype(_BF),
                      w2[0].astype(_BF), normf_g.reshape(1, D), tm=TAIL_ROWS)
```

```python
import jax
import jax.numpy as jnp
import numpy as np
from jax import lax
from jax.experimental import pallas as pl
from jax.experimental.pallas import tpu as pltpu

D_MODEL = 1024
HEADS = 4
HEAD_DIM = 256
SGU_GROUPS = 8
SGU_GROUP_DIM = 128
SGU_CHUNK = 128
FF_DIM = 4096
EPS = 1e-6
Z_COLS = 7 * D_MODEL

MLSTM_CHUNK = 256
F32_ROWS = 8
BF16_ROWS = 16
STATE_ROWS = HEAD_DIM + BF16_ROWS
STATE_BLOCK = 64
PASS0_GROUP = 8
PASS2_GROUP = 16
HALO = BF16_ROWS
GATE_SLOTS = 8
CONV_ROWS = 64
CONV_COLS = 128
MXU_COLS = 256
LANES = 128

VMEM_LIMIT = 56 * 1024 * 1024
IN_PROJ_ROWS = 512
TAIL_ROWS = 512

_BF = jnp.bfloat16
_F32 = jnp.float32
_NT = (((1,), (1,)), ((), ()))
_LOG2E = 1.4426950408889634


def _const_spec(shape):
    nd = len(shape)
    return pl.BlockSpec(shape, lambda *_: (0,) * nd, pipeline_mode=pl.Buffered(1))


def _sigmoid(x):
    return 1.0 / (1.0 + jnp.exp(-x))


def _log_sigmoid(x):
    return jnp.minimum(x, 0.0) - jnp.log(1.0 + jnp.exp(-jnp.abs(x)))


def _gelu(x):
    return 0.5 * x * (1.0 + lax.erf(x * (2.0 ** -0.5)))


def _mod_kernel(c_ref, w_ref, b_ref, o_ref):
    c = c_ref[...]
    ca = c * _sigmoid(c)
    o_ref[...] = jnp.dot(ca, w_ref[...], precision=lax.Precision.HIGHEST,
                         preferred_element_type=_F32) + b_ref[...]


def _mod_call(c, w_ada, b_ada):
    B, D = c.shape
    N = w_ada.shape[1]
    tn = N // 4
    return pl.pallas_call(
        _mod_kernel,
        grid=(N // tn,),
        in_specs=[pl.BlockSpec((B, D), lambda j: (0, 0)),
                  pl.BlockSpec((D, tn), lambda j: (0, j)),
                  pl.BlockSpec((1, tn), lambda j: (0, j))],
        out_specs=pl.BlockSpec((B, tn), lambda j: (0, j)),
        out_shape=jax.ShapeDtypeStruct((B, N), _F32),
        name="mod",
    )(c, w_ada, b_ada.reshape(1, N))


def _lane_scan(x, op, ident, reverse):
    L = x.shape[1]
    lane = lax.broadcasted_iota(jnp.int32, x.shape, 1)
    d = 1
    while d < L:
        if reverse:
            shifted = jnp.where(lane < L - d, pltpu.roll(x, L - d, axis=1), ident)
        else:
            shifted = jnp.where(lane >= d, pltpu.roll(x, d, axis=1), ident)
        x = op(x, shifted)
        d *= 2
    return x


def _in_proj_kernel(x_ref, xp_ref, xn_ref, g_ref, sc_ref, sh_ref, w_ref, wg_ref,
                    bg_ref, cw_ref, cb_ref, z_ref, vt_ref, grow_ref, qk_s, hb_s):
    tm = x_ref.shape[1]
    L = MLSTM_CHUNK
    i = pl.program_id(1)

    gain = g_ref[...] * (1.0 + sc_ref[0])

    def norm_mod(x):
        ms = jnp.mean(x * x, axis=-1, keepdims=True)
        return (x * lax.rsqrt(ms + EPS) * gain + sh_ref[0]).astype(_BF)

    hb_s[HALO:HALO + tm, :] = norm_mod(x_ref[0])
    hb_s[:HALO, :] = jnp.where(i > 0, norm_mod(xp_ref[0]), jnp.zeros((HALO, D_MODEL), _BF))
    hb_s[HALO + tm:, :] = jnp.where(i < pl.num_programs(1) - 1, norm_mod(xn_ref[0]),
                                    jnp.zeros((HALO, D_MODEL), _BF))
    hb = hb_s.at[HALO:HALO + tm, :]

    def plain(c0):
        cols = slice(c0, c0 + MXU_COLS)
        wrows = slice(c0 + D_MODEL, c0 + D_MODEL + MXU_COLS)
        r = lax.dot_general(hb[...], w_ref[wrows, :], _NT, preferred_element_type=_F32)
        if 3 * D_MODEL <= c0 < 5 * D_MODEL:
            r = _gelu(r)
        else:
            r = 1.0 / (1.0 + jnp.exp2(r * -_LOG2E))
        z_ref[0, :, cols] = r.astype(_BF)

    def conv_matmul(c0):
        qk_s[:, c0:c0 + MXU_COLS] = lax.dot_general(
            hb_s[...], w_ref[c0:c0 + MXU_COLS, :], _NT, preferred_element_type=_F32)

    def conv_silu(c0):
        n = CONV_ROWS + 2 * F32_ROWS
        mid = slice(F32_ROWS, F32_ROWS + CONV_ROWS)
        for c1 in range(c0, c0 + MXU_COLS, CONV_COLS):
            cols = slice(c1, c1 + CONV_COLS)
            for r0 in range(0, tm, CONV_ROWS):
                zb = qk_s[pl.ds(HALO - F32_ROWS + r0, n), cols]
                y = cb_ref[:, cols] + pltpu.roll(zb, 1, axis=0)[mid] * cw_ref[0:1, cols]
                y = y + zb[mid] * cw_ref[1:2, cols]
                y = y + pltpu.roll(zb, n - 1, axis=0)[mid] * cw_ref[2:3, cols]
                y = y / (1.0 + jnp.exp2(y * -_LOG2E))
                if c1 < D_MODEL:
                    y = y * (HEAD_DIM ** -0.5)
                z_ref[0, pl.ds(r0, CONV_ROWS), cols] = y.astype(_BF)

    def gates_pre():
        gc = lax.dot_general(hb[...], wg_ref[...], _NT,
                             preferred_element_type=_F32) + bg_ref[...]
        return gc.T

    def gates_row(gt):
        nr = HEADS * GATE_SLOTS
        lst = _log_sigmoid(gt[:nr])
        git = gt[nr:2 * nr]
        slot = lax.broadcasted_iota(jnp.int32, (nr, L), 0) & (GATE_SLOTS - 1)
        fwd = slot < 3
        for c in range(tm // L):
            cols = slice(c * L, (c + 1) * L)
            lsc = lst[:, cols]
            cum = jnp.where(fwd, _lane_scan(lsc, jnp.add, 0.0, False),
                            _lane_scan(lsc, jnp.add, 0.0, True))
            base = jnp.where((slot == 0) | (slot == 3), cum, git[:, cols] - cum)
            cmax = jnp.where(fwd, _lane_scan(base, jnp.maximum, -jnp.inf, False),
                             _lane_scan(base, jnp.maximum, -jnp.inf, True))
            grow_ref[0, :, cols] = jnp.where((slot == 2) | (slot == 5), cmax, base)

    gt = gates_pre()
    conv_matmul(0)
    for c0 in range(MXU_COLS, 2 * D_MODEL, MXU_COLS):
        conv_matmul(c0)
        conv_silu(c0 - MXU_COLS)
    plain(2 * D_MODEL)
    conv_silu(2 * D_MODEL - MXU_COLS)
    gates_row(gt)
    for c0 in range(2 * D_MODEL + MXU_COLS, Z_COLS, MXU_COLS):
        plain(c0)
    vt_ref[0] = lax.dot_general(w_ref[2 * D_MODEL:3 * D_MODEL, :], hb[...], _NT,
                                preferred_element_type=_F32).astype(_BF)


def _in_proj_call(x, norm_g, sc, sh, w_bf, wg_bf, bias_g, conv_w, conv_b, tm):
    B, S, D = x.shape
    r = tm // HALO
    last = S // HALO - 1
    return pl.pallas_call(
        _in_proj_kernel,
        grid=(B, S // tm),
        in_specs=[pl.BlockSpec((1, tm, D), lambda b, i: (b, i, 0)),
                  pl.BlockSpec((1, HALO, D), lambda b, i: (b, jnp.maximum(i * r - 1, 0), 0)),
                  pl.BlockSpec((1, HALO, D),
                               lambda b, i: (b, jnp.minimum((i + 1) * r, last), 0)),
                  _const_spec((1, D)),
                  pl.BlockSpec((1, 1, D), lambda b, i: (b, 0, 0)),
                  pl.BlockSpec((1, 1, D), lambda b, i: (b, 0, 0)),
                  _const_spec(w_bf.shape),
                  _const_spec((LANES, D)),
                  _const_spec((1, LANES)),
                  _const_spec((3, 2 * D)),
                  _const_spec((1, 2 * D))],
        out_specs=[pl.BlockSpec((1, tm, Z_COLS), lambda b, i: (b, i, 0)),
                   pl.BlockSpec((1, D, tm), lambda b, i: (b, 0, i)),
                   pl.BlockSpec((1, HEADS * GATE_SLOTS, tm), lambda b, i: (b, 0, i))],
        out_shape=[jax.ShapeDtypeStruct((B, S, Z_COLS), _BF),
                   jax.ShapeDtypeStruct((B, D, S), _BF),
                   jax.ShapeDtypeStruct((B, HEADS * GATE_SLOTS, S), _F32)],
        scratch_shapes=[pltpu.VMEM((tm + 2 * HALO, 2 * D), _F32),
                        pltpu.VMEM((tm + 2 * HALO, D), _BF)],
        compiler_params=pltpu.CompilerParams(
            dimension_semantics=("arbitrary", "arbitrary"),
            vmem_limit_bytes=VMEM_LIMIT),
        name="in_proj",
    )(x, x, x, norm_g, sc, sh, w_bf, wg_bf, bias_g, conv_w, conv_b)


def _mlstm_kernel(q_ref, k_ref, vt_ref, zo_ref, gr_ref, mhg_ref,
                  y_ref,
                  pvf_s, pvb_s, u_s, snap_s, ct_s, mh_s, dec_s, ht_s):
    S = q_ref.shape[1]
    L = MLSTM_CHUNK
    NC = S // L
    dh = HEAD_DIM
    R = STATE_ROWS
    K0 = (0, 3)
    LAST = (L - 1, 0)
    NB = R - dh

    def n_rows(d):
        return slice(d * NB, (d + 1) * NB)

    def c_rows(d, e0):
        start = 2 * NB + 2 * e0 + d * STATE_BLOCK
        return slice(start, start + STATE_BLOCK)

    def state_blocks(d):
        return [n_rows(d)] + [c_rows(d, e0) for e0 in range(0, dh, STATE_BLOCK)]

    def gate_rows(gr, d):
        k0 = K0[d]
        return gr[k0:k0 + 1], gr[k0 + 1:k0 + 2], gr[k0 + 2:k0 + 3]

    m = [jnp.zeros((1, 1), _F32), jnp.zeros((1, 1), _F32)]
    for i in range(NC):
        for d, c in enumerate((i, NC - 1 - i)):
            b_row, _, cmax_row = gate_rows(gr_ref[0, 0, :, c * L:(c + 1) * L], d)
            a_last = jnp.maximum(m[d], cmax_row[:, LAST[d]:LAST[d] + 1])
            mh_s[d * NC + c:d * NC + c + 1, :] = jnp.broadcast_to(m[d], (1, LANES))
            dec_s[d * NC + c:d * NC + c + 1, :] = jnp.broadcast_to(jnp.exp(m[d] - a_last),
                                                                  (1, LANES))
            m[d] = b_row[:, LAST[d]:LAST[d] + 1] + a_last

    si = lax.broadcasted_iota(jnp.int32, (L, L), 0)
    ti = lax.broadcasted_iota(jnp.int32, (L, L), 1)
    masks = (si <= ti, si >= ti)
    ones_row = (lax.broadcasted_iota(jnp.int32, (R - dh, L), 0) == 0).astype(_BF)

    def scores(c):
        t0 = pl.multiple_of(c * L, L)
        return lax.dot_general(k_ref[0, pl.ds(t0, L), :], q_ref[0, pl.ds(t0, L), :], _NT,
                               preferred_element_type=_F32)

    def pass0(g, carry):
        s_next = scores(g * PASS0_GROUP)
        for k in range(PASS0_GROUP):
            c = g * PASS0_GROUP + k
            t0 = pl.multiple_of(c * L, L)
            s_t = s_next
            vta = jnp.concatenate([vt_ref[0, :, pl.ds(t0, L)], ones_row], axis=0)
            vta32 = vta.astype(_F32)
            gr = gr_ref[0, 0, :, pl.ds(t0, L)]
            a_rows, vtw = [], []
            for d in range(2):
                _, r_row, cmax_row = gate_rows(gr, d)
                a_rows.append(jnp.maximum(mh_s[pl.ds(d * NC + c, 1), 0:1], cmax_row))
                w_row = jnp.exp(r_row - a_rows[d][:, LAST[d]:LAST[d] + 1])
                vtw.append((vta32 * w_row).astype(_BF))
            stacked = [vtw[d][dh:] for d in range(2)]
            for e0 in range(0, dh, STATE_BLOCK):
                stacked += [vtw[d][e0:e0 + STATE_BLOCK] for d in range(2)]
            u_s[c] = jnp.dot(jnp.concatenate(stacked, axis=0), k_ref[0, pl.ds(t0, L), :],
                             preferred_element_type=_F32).astype(_BF)
            if k + 1 < PASS0_GROUP:
                s_next = scores(c + 1)
            p = []
            for d in range(2):
                r_row = gate_rows(gr, d)[1]
                r_rep = jnp.broadcast_to(r_row, (LANES, L)).T
                r_col = jnp.concatenate([r_rep] * (L // LANES), axis=1)
                p.append((jnp.exp(jnp.where(masks[d], r_col - a_rows[d], -jnp.inf))
                          * s_t).astype(_BF))
            pv = jnp.dot(vta, jnp.concatenate(p, axis=1), preferred_element_type=_F32)
            pvf_s[:, pl.ds(t0, L)] = pv[:, :L]
            pvb_s[:, pl.ds(t0, L)] = pv[:, L:]
        return carry

    lax.fori_loop(0, NC // PASS0_GROUP, pass0, 0)

    ct_s[...] = jnp.zeros_like(ct_s)

    def pass1(i, carry):
        for d, c in enumerate((i, NC - 1 - i)):
            decay = dec_s[pl.ds(d * NC + c, 1), 0:1]
            for rows in state_blocks(d):
                ct = ct_s[rows, :]
                snap_s[c, rows, :] = ct.astype(_BF)
                ct_s[rows, :] = decay * ct + u_s[c, rows, :].astype(_F32)
        return carry

    lax.fori_loop(0, NC, pass1, 0, unroll=2)

    def inter_term(c):
        t0 = pl.multiple_of(c * L, L)
        return lax.dot_general(snap_s[c], q_ref[0, pl.ds(t0, L), :], _NT,
                               preferred_element_type=_F32)

    def pass2(g, carry):
        inter_next = inter_term(g * PASS2_GROUP)
        for k in range(PASS2_GROUP):
            c = g * PASS2_GROUP + k
            t0 = pl.multiple_of(c * L, L)
            inter = inter_next
            if k + 1 < PASS2_GROUP:
                inter_next = inter_term(c + 1)
            gr = gr_ref[0, 0, :, pl.ds(t0, L)]
            w_inter, scale = [], []
            for d, pv_s in enumerate((pvf_s, pvb_s)):
                b_row, _, cmax_row = gate_rows(gr, d)
                m_c = mh_s[pl.ds(d * NC + c, 1), 0:1]
                a_row = jnp.maximum(m_c, cmax_row)
                w_inter.append(jnp.exp(m_c - a_row))
                den = (w_inter[d] * inter[d * NB:d * NB + 1]
                       + pv_s[dh:dh + 1, pl.ds(t0, L)])
                scale.append(1.0 / jnp.maximum(jnp.abs(den), jnp.exp(-(b_row + a_row))))
            ss = jnp.zeros((1, L), _F32)
            for e0 in range(0, dh, STATE_BLOCK):
                blk = None
                for d, pv_s in enumerate((pvf_s, pvb_s)):
                    num = (w_inter[d] * inter[c_rows(d, e0)]
                           + pv_s[e0:e0 + STATE_BLOCK, pl.ds(t0, L)])
                    blk = num * scale[d] if blk is None else blk + num * scale[d]
                ss = ss + jnp.sum(blk * blk, axis=0, keepdims=True)
                ht_s[e0:e0 + STATE_BLOCK, :] = blk
            rs = lax.rsqrt(ss * (1.0 / dh) + EPS)
            for t1 in range(0, L, LANES):
                h = (ht_s[:, t1:t1 + LANES] * rs[:, t1:t1 + LANES]).T
                tq = pl.multiple_of(t0 + t1, LANES)
                o = zo_ref[0, pl.ds(tq, LANES), :].astype(_F32)
                y_ref[0, pl.ds(tq, LANES), :] = (h * mhg_ref[...] * o).astype(_BF)
        return carry

    lax.fori_loop(0, NC // PASS2_GROUP, pass2, 0)


def _mlstm_call(z, vt, grow, mh_g):
    B, S, _ = z.shape
    dh = HEAD_DIM
    H = HEADS
    nc = S // MLSTM_CHUNK

    def zspec(off):
        return pl.BlockSpec((1, S, dh), lambda b, h, off=off: (b, 0, off + h))

    return pl.pallas_call(
        _mlstm_kernel,
        grid=(B, H),
        in_specs=[zspec(0), zspec(H),
                  pl.BlockSpec((1, dh, S), lambda b, h: (b, h, 0)),
                  zspec(2 * H),
                  pl.BlockSpec((1, 1, GATE_SLOTS, S), lambda b, h: (b, h, 0, 0)),
                  pl.BlockSpec((1, dh), lambda b, h: (0, h))],
        out_specs=pl.BlockSpec((1, S, dh), lambda b, h: (b, 0, h)),
        out_shape=jax.ShapeDtypeStruct((B, S, H * dh), _BF),
        scratch_shapes=[pltpu.VMEM((STATE_ROWS, S), _F32), pltpu.VMEM((STATE_ROWS, S), _F32),
                        pltpu.VMEM((nc, 2 * STATE_ROWS, dh), _BF),
                        pltpu.VMEM((nc, 2 * STATE_ROWS, dh), _BF),
                        pltpu.VMEM((2 * STATE_ROWS, dh), _F32),
                        pltpu.VMEM((2 * nc, LANES), _F32), pltpu.VMEM((2 * nc, LANES), _F32),
                        pltpu.VMEM((dh, MLSTM_CHUNK), _F32)],
        compiler_params=pltpu.CompilerParams(
            dimension_semantics=("arbitrary", "arbitrary"),
            vmem_limit_bytes=VMEM_LIMIT),
        name="mlstm",
    )(z, z, vt, z, grow, mh_g)


def _tail_kernel(x_ref, zu_ref, zv_ref, zga_ref, zgb_ref, ya_ref, g1_ref, lng_ref, lnb_ref,
                 ws_ref, bs_ref, wo_ref, g_ref, sc_ref, sh_ref, g2_ref, w1_ref, w2_ref, nf_ref,
                 o_ref, mix_s):
    t = pl.program_id(0)
    tm = x_ref.shape[1]
    P = SGU_CHUNK
    C = SGU_GROUP_DIM
    n_seg = FF_DIM // D_MODEL
    assert tm // P == n_seg
    new = mix_s.at[t % 2]
    old = mix_s.at[(t + 1) % 2]

    def merge(j):
        rows = slice(j * P, (j + 1) * P)
        gv = zv_ref[0, rows, :].astype(_F32)
        mu = jnp.mean(gv, axis=-1, keepdims=True)
        dv = gv - mu
        var = jnp.mean(dv * dv, axis=-1, keepdims=True)
        vn = (dv * lax.rsqrt(var + EPS) * lng_ref[...] + lnb_ref[...]).astype(_BF)
        s = jnp.concatenate(
            [jnp.dot(ws_ref[g], vn[:, g * C:(g + 1) * C], preferred_element_type=_F32)
             for g in range(SGU_GROUPS)], axis=1) + bs_ref[...]
        yb = zu_ref[0, rows, :].astype(_F32) * s
        ga = zga_ref[0, rows, :].astype(_F32)
        gb = zgb_ref[0, rows, :].astype(_F32)
        new[rows, :] = (ga * ya_ref[0, rows, :].astype(_F32) + gb * yb).astype(_BF)

    def mlp(merge_chunk):
        x1 = x_ref[0] + g1_ref[0] * jnp.dot(old[...], wo_ref[...], preferred_element_type=_F32)
        ms = jnp.mean(x1 * x1, axis=-1, keepdims=True)
        gain = g_ref[...] * (1.0 + sc_ref[0])
        hb = (x1 * lax.rsqrt(ms + EPS) * gain + sh_ref[0]).astype(_BF)
        ff = jnp.zeros(x1.shape, _F32)
        for j in range(n_seg):
            cols = slice(j * D_MODEL, (j + 1) * D_MODEL)
            a = jnp.maximum(jnp.dot(hb, w1_ref[:, cols], preferred_element_type=_F32), 0.0)
            merge_chunk(j)
            ff = ff + jnp.dot((a * a).astype(_BF), w2_ref[cols, :],
                              preferred_element_type=_F32)
        x2 = x1 + g2_ref[0] * ff
        ms2 = jnp.mean(x2 * x2, axis=-1, keepdims=True)
        o_ref[0] = x2 * lax.rsqrt(ms2 + EPS) * nf_ref[...]

    last = pl.num_programs(0) - 1

    @pl.when(t == 0)
    def _():
        for j in range(n_seg):
            merge(j)

    @pl.when((t > 0) & (t < last))
    def _():
        mlp(merge)

    @pl.when(t == last)
    def _():
        mlp(lambda j: None)


def _tail_call(x, z, ya, g1, ln_g, ln_b, ws_bf, bs_full, wo_bf,
               norm_g, sc, sh, g2, w1_bf, w2_bf, normf_g, tm):
    B, S, D = x.shape
    tpb = S // tm
    nt = B * tpb

    def merged(j):
        def index(t):
            n = jnp.minimum(t, nt - 1)
            return n // tpb, n % tpb, j
        return pl.BlockSpec((1, tm, D), index)

    def done_tile(t):
        o = jnp.maximum(t - 1, 0)
        return o // tpb, o % tpb, 0

    tile = pl.BlockSpec((1, tm, D), done_tile)
    vec = pl.BlockSpec((1, 1, D), lambda t: (done_tile(t)[0], 0, 0))
    return pl.pallas_call(
        _tail_kernel,
        grid=(nt + 1,),
        in_specs=[tile, merged(3), merged(4), merged(5), merged(6), merged(0), vec,
                  _const_spec((1, D)), _const_spec((1, D)),
                  _const_spec((SGU_GROUPS, SGU_CHUNK, SGU_CHUNK)),
                  _const_spec((SGU_CHUNK, D)),
                  _const_spec((D, D)),
                  _const_spec((1, D)), vec, vec, vec,
                  _const_spec((D, FF_DIM)), _const_spec((FF_DIM, D)),
                  _const_spec((1, D))],
        out_specs=tile,
        out_shape=jax.ShapeDtypeStruct((B, S, D), _F32),
        scratch_shapes=[pltpu.VMEM((2, tm, D), _BF)],
        compiler_params=pltpu.CompilerParams(
            dimension_semantics=("arbitrary",),
            vmem_limit_bytes=VMEM_LIMIT),
        name="tail",
    )(x, z, z, z, z, ya, g1, ln_g, ln_b, ws_bf, bs_full, wo_bf,
      norm_g, sc, sh, g2, w1_bf, w2_bf, normf_g)


def _gate_weights(w_if_t, b_if):
    H = HEADS
    idx_f, idx_i = [], []
    for h in range(H):
        idx_f += [H + h] * 3 + [3 * H + h] * 3 + [None] * 2
        idx_i += [None, h, h, None, 2 * H + h, 2 * H + h, None, None]
    sel = np.zeros((LANES, 4 * H), np.float32)
    for row, src in enumerate(idx_f + idx_i):
        if src is not None:
            sel[row, src] = 1.0
    w_g = (sel[:, :, None] * w_if_t[None]).sum(axis=1)
    b_g = (sel * b_if.reshape(1, -1)).sum(axis=1).reshape(1, LANES)
    return w_g, b_g


def kernel(x, c, w_ada, b_ada, norm1_g, norm2_g, w_in, b_if, conv_w, conv_b, mh_g,
           ln_v_g, ln_v_b, w_s, b_s, w_out, w1, w2, normf_g):
    B, S, D = x.shape
    H = HEADS
    assert w_ada.shape[0] == 1, "single layer"

    mod = _mod_call(c, w_ada[0], b_ada[0])
    sh1, sc1, g1, sh2, sc2, g2 = (mod[:, k * D:(k + 1) * D].reshape(B, 1, D) for k in range(6))

    w_in_t = w_in[0].T
    w_g, b_g = _gate_weights(w_in_t[8 * D:], b_if[0])
    z, vt, grow = _in_proj_call(x, norm1_g, sc1, sh1, w_in_t.astype(_BF), w_g.astype(_BF), b_g,
                                conv_w[0], conv_b, tm=IN_PROJ_ROWS)

    ya = _mlstm_call(z, vt, grow.reshape(B, H, GATE_SLOTS, S), mh_g)

    bs_full = jnp.repeat(b_s[0].T, SGU_GROUP_DIM, axis=1)
    return _tail_call(x, z, ya, g1, ln_v_g, ln_v_b, w_s[0].astype(_BF), bs_full,
                      w_out[0].astype(_BF), norm2_g, sc2, sh2, g2, w1[0].astype(_BF),
                      w2[0].astype(_BF), normf_g.reshape(1, D), tm=TAIL_ROWS)
```
